```python
import jax, jax.numpy as jnp
from jax import lax
import numpy as np

D_MODEL = 2048
BATCH = 8
SEQ = 2048
DEPTH = 2

N_A_LAYERS = DEPTH // 2
N_B_LAYERS = DEPTH - N_A_LAYERS

RW_HEAD = 64
RW_HEADS = D_MODEL // RW_HEAD
RW_DECAY_LORA = max(32, int(round(1.8 * D_MODEL ** 0.5 / 32)) * 32)
RW_A_LORA = max(32, int(round(1.8 * D_MODEL ** 0.5 / 32)) * 32)
RW_GATE_LORA = max(32, int(round(0.6 * D_MODEL ** 0.8 / 32)) * 32)
RW_GN_EPS = RW_HEAD * 1e-5

MLA_HEAD_V = 128
MLA_HEADS = D_MODEL // MLA_HEAD_V
MLA_NOPE = 128
MLA_ROPE = 64
MLA_Q_LORA = D_MODEL // 4
MLA_KV_LORA = 512
ROPE_THETA = 10000.0
BLOCK_Q = 128

FFN_HIDDEN = 4 * D_MODEL
NORM_EPS = 1e-6
N_MOD = 6

kernel_name = "yoco_rwkv7_mla_sandwich_adaln"


def rms_norm(x, g, eps=NORM_EPS):
    xf = x.astype(jnp.float32)
    y = xf * lax.rsqrt(jnp.mean(xf * xf, axis=-1, keepdims=True) + eps)
    return (y * g.astype(jnp.float32)).astype(x.dtype)


def token_shift(h):
    return jnp.pad(h, ((0, 0), (1, 0), (0, 0)))[:, :-1]


def apply_rope(t, cos, sin):
    half = t.shape[-1] // 2
    t1, t2 = t[..., :half], t[..., half:]
    return jnp.concatenate([t1 * cos - t2 * sin, t2 * cos + t1 * sin], axis=-1)


def squared_relu_mlp(h, w_up, w_down):
    u = jax.nn.relu(h @ w_up)
    return (u * u) @ w_down


def rwkv7_time_mix(h, mu, w_rkv, w0, w1, w2, a0, a1, a2, g1, g2, k_k, k_a, r_k, ln_x, w_o):
    B, T, D = h.shape
    H, N = RW_HEADS, RW_HEAD
    f32 = jnp.float32
    xx = token_shift(h) - h
    xs = h[None] + xx[None] * mu[:, None, None, :]
    rkv = jnp.einsum('jbtd,jde->jbte', xs[:3], w_rkv)
    r, k, v = rkv[0], rkv[1], rkv[2]
    xw, xa, xg = xs[3], xs[4], xs[5]
    w_log = -jax.nn.softplus(-(w0 + jnp.tanh(xw @ w1) @ w2)) - 0.5
    decay = jnp.exp(-jnp.exp(w_log.astype(f32)))
    a = jax.nn.sigmoid(a0 + (xa @ a1) @ a2)
    g = jax.nn.sigmoid(xg @ g1) @ g2

    kk = (k * k_k).reshape(B, T, H, N).astype(f32)
    kk = kk / jnp.maximum(jnp.sqrt(jnp.sum(kk * kk, axis=-1, keepdims=True)), 1e-12)
    k = k * (1.0 + (a - 1.0) * k_a)

    def heads(t):
        return t.reshape(B, T, H, N).astype(f32)

    rh, kh, vh, ah, wh = heads(r), heads(k), heads(v), heads(a), heads(decay)
    seq_first = lambda t: jnp.moveaxis(t, 1, 0)

    def step(S, inp):
        r_t, w_t, k_t, v_t, kk_t, a_t = inp
        sa = jnp.einsum('bhvk,bhk->bhv', S, -kk_t)
        S = (S * w_t[:, :, None, :]
             + sa[..., None] * (kk_t * a_t)[:, :, None, :]
             + v_t[..., None] * k_t[:, :, None, :])
        y_t = jnp.einsum('bhvk,bhk->bhv', S, r_t)
        return S, y_t

    S0 = jnp.zeros((B, H, N, N), f32)
    _, y = lax.scan(step, S0, (seq_first(rh), seq_first(wh), seq_first(kh),
                               seq_first(vh), seq_first(kk), seq_first(ah)))
    y = jnp.moveaxis(y, 0, 1)

    mean = jnp.mean(y, axis=-1, keepdims=True)
    var = jnp.mean(jnp.square(y - mean), axis=-1, keepdims=True)
    y = ((y - mean) * lax.rsqrt(var + RW_GN_EPS)).reshape(B, T, D)
    y = y * ln_x[0].astype(f32) + ln_x[1].astype(f32)
    bonus = jnp.sum(rh * kh * r_k.astype(f32), axis=-1, keepdims=True) * vh
    out = ((y + bonus.reshape(B, T, D)).astype(h.dtype)) * g
    return out @ w_o


def mla_shared_kv(x, kv_in_g, kv_down, kv_norm, kv_uk, kv_uv, cos, sin):
    hs = rms_norm(x, kv_in_g)
    ckr = hs @ kv_down
    c_kv = rms_norm(ckr[..., :MLA_KV_LORA], kv_norm)
    k_rope = apply_rope(ckr[..., MLA_KV_LORA:], cos, sin)
    k_nope = jnp.einsum('bsc,chd->bshd', c_kv, kv_uk)
    v = jnp.einsum('bsc,chd->bshd', c_kv, kv_uv)
    return k_nope, k_rope, v


def mla_attention(h, w_dq, q_norm, w_uq, w_o, k_nope, k_rope, v, cos, sin):
    B, S, _ = h.shape
    cq = rms_norm(h @ w_dq, q_norm)
    q = jnp.einsum('bsc,chd->bshd', cq, w_uq)
    q_nope = q[..., :MLA_NOPE]
    q_rope = apply_rope(q[..., MLA_NOPE:], cos[:, :, None, :], sin[:, :, None, :])
    nb = S // BLOCK_Q
    scale = (MLA_NOPE + MLA_ROPE) ** -0.5
    key_idx = jnp.arange(S)
    neg = jnp.finfo(jnp.float32).min

    def to_blocks(t):
        return t.reshape(B, nb, BLOCK_Q, *t.shape[2:]).swapaxes(0, 1)

    def attend(args):
        qn, qr, start = args
        s = (jnp.einsum('bqhd,bkhd->bhqk', qn, k_nope)
             + jnp.einsum('bqhd,bkd->bhqk', qr, k_rope)).astype(jnp.float32) * scale
        mask = (start + jnp.arange(BLOCK_Q))[:, None] >= key_idx[None, :]
        s = jnp.where(mask[None, None], s, neg)
        p = jax.nn.softmax(s, axis=-1).astype(v.dtype)
        return jnp.einsum('bhqk,bkhd->bqhd', p, v)

    o = lax.map(attend, (to_blocks(q_nope), to_blocks(q_rope), jnp.arange(nb) * BLOCK_Q))
    o = o.swapaxes(0, 1).reshape(B, S, MLA_HEADS * MLA_HEAD_V)
    return o @ w_o


def setup_inputs(seed: int = 0) -> dict:
    key = jax.random.key(seed)
    ks = iter(jax.random.split(key, 48))
    f32 = jnp.float32
    D, F = D_MODEL, FFN_HIDDEN
    NA, NB = N_A_LAYERS, N_B_LAYERS
    H, N = RW_HEADS, RW_HEAD
    DL, AL, GL = RW_DECAY_LORA, RW_A_LORA, RW_GATE_LORA
    QH, QK = MLA_HEADS, MLA_NOPE + MLA_ROPE

    def nrm(shape, scale):
        return jax.random.normal(next(ks), shape, f32) * scale

    x = nrm((BATCH, SEQ, D), 1.0)
    c = nrm((BATCH, D), 1.0)
    offset = jax.random.randint(next(ks), (BATCH, 1), 0, SEQ, dtype=jnp.int32)
    positions = offset + jnp.arange(SEQ, dtype=jnp.int32)[None, :]

    ada_w = nrm((DEPTH, D, N_MOD * D), D ** -0.5)
    ada_b = nrm((DEPTH, N_MOD * D), 0.02)
    norm_g = 1.0 + nrm((DEPTH, 4, D), 0.05)
    mlp_up = nrm((DEPTH, D, F), D ** -0.5)
    mlp_down = nrm((DEPTH, F, D), F ** -0.5)

    rw_mu = jax.random.uniform(next(ks), (NA, 6, D), f32)
    rw_rkv = nrm((NA, 3, D, D), D ** -0.5)
    rw_w0 = jax.random.uniform(next(ks), (NA, D), f32, -6.5, -1.5)
    rw_w1 = nrm((NA, D, DL), D ** -0.5)
    rw_w2 = nrm((NA, DL, D), 0.1 * DL ** -0.5)
    rw_a0 = nrm((NA, D), 0.1)
    rw_a1 = nrm((NA, D, AL), D ** -0.5)
    rw_a2 = nrm((NA, AL, D), 0.5 * AL ** -0.5)
    rw_g1 = nrm((NA, D, GL), D ** -0.5)
    rw_g2 = nrm((NA, GL, D), GL ** -0.5)
    rw_kk = 0.85 + nrm((NA, D), 0.05)
    rw_ka = 1.0 + nrm((NA, D), 0.05)
    rw_rk = nrm((NA, H, N), 0.1)
    rw_lnx = jnp.stack([1.0 + nrm((NA, D), 0.05), nrm((NA, D), 0.02)], axis=1)
    rw_o = nrm((NA, D, D), D ** -0.5)

    mla_dq = nrm((NB, D, MLA_Q_LORA), D ** -0.5)
    mla_qnorm = 1.0 + nrm((NB, MLA_Q_LORA), 0.05)
    mla_uq = nrm((NB, MLA_Q_LORA, QH, QK), MLA_Q_LORA ** -0.5)
    mla_o = nrm((NB, QH * MLA_HEAD_V, D), (QH * MLA_HEAD_V) ** -0.5)

    kv_in_g = 1.0 + nrm((D,), 0.05)
    kv_down = nrm((D, MLA_KV_LORA + MLA_ROPE), D ** -0.5)
    kv_norm = 1.0 + nrm((MLA_KV_LORA,), 0.05)
    kv_uk = nrm((MLA_KV_LORA, QH, MLA_NOPE), MLA_KV_LORA ** -0.5)
    kv_uv = nrm((MLA_KV_LORA, QH, MLA_HEAD_V), MLA_KV_LORA ** -0.5)

    return {
        "x": x, "c": c, "positions": positions,
        "ada_w": ada_w, "ada_b": ada_b, "norm_g": norm_g,
        "mlp_up": mlp_up, "mlp_down": mlp_down,
        "rw_mu": rw_mu, "rw_rkv": rw_rkv, "rw_w0": rw_w0, "rw_w1": rw_w1, "rw_w2": rw_w2,
        "rw_a0": rw_a0, "rw_a1": rw_a1, "rw_a2": rw_a2, "rw_g1": rw_g1, "rw_g2": rw_g2,
        "rw_kk": rw_kk, "rw_ka": rw_ka, "rw_rk": rw_rk, "rw_lnx": rw_lnx, "rw_o": rw_o,
        "mla_dq": mla_dq, "mla_qnorm": mla_qnorm, "mla_uq": mla_uq, "mla_o": mla_o,
        "kv_in_g": kv_in_g, "kv_down": kv_down, "kv_norm": kv_norm, "kv_uk": kv_uk, "kv_uv": kv_uv,
    }


def reference(x, c, positions, ada_w, ada_b, norm_g, mlp_up, mlp_down,
              rw_mu, rw_rkv, rw_w0, rw_w1, rw_w2, rw_a0, rw_a1, rw_a2, rw_g1, rw_g2,
              rw_kk, rw_ka, rw_rk, rw_lnx, rw_o,
              mla_dq, mla_qnorm, mla_uq, mla_o,
              kv_in_g, kv_down, kv_norm, kv_uk, kv_uv):
    B, S, D = x.shape
    inv_freq = 1.0 / (ROPE_THETA ** (jnp.arange(0, MLA_ROPE, 2, dtype=jnp.float32) / MLA_ROPE))
    ang = positions.astype(jnp.float32)[..., None] * inv_freq
    cos, sin = jnp.cos(ang).astype(x.dtype), jnp.sin(ang).astype(x.dtype)
    c_act = jax.nn.silu(c)
    shared = None

    for l in range(DEPTH):
        mod = (c_act @ ada_w[l] + ada_b[l]).reshape(B, N_MOD, 1, D)
        shift_m, scale_m, gate_m = mod[:, 0], mod[:, 1], mod[:, 2]
        shift_f, scale_f, gate_f = mod[:, 3], mod[:, 4], mod[:, 5]

        h = rms_norm(x, norm_g[l, 0]) * (1.0 + scale_m) + shift_m
        if l < N_A_LAYERS:
            i = l
            y = rwkv7_time_mix(h, rw_mu[i], rw_rkv[i], rw_w0[i], rw_w1[i], rw_w2[i],
                               rw_a0[i], rw_a1[i], rw_a2[i], rw_g1[i], rw_g2[i],
                               rw_kk[i], rw_ka[i], rw_rk[i], rw_lnx[i], rw_o[i])
        else:
            if shared is None:
                shared = mla_shared_kv(x, kv_in_g, kv_down, kv_norm, kv_uk, kv_uv, cos, sin)
            k_nope, k_rope, v = shared
            i = l - N_A_LAYERS
            y = mla_attention(h, mla_dq[i], mla_qnorm[i], mla_uq[i], mla_o[i],
                              k_nope, k_rope, v, cos, sin)
        x = x + gate_m * rms_norm(y, norm_g[l, 1])

        h = rms_norm(x, norm_g[l, 2]) * (1.0 + scale_f) + shift_f
        y = squared_relu_mlp(h, mlp_up[l], mlp_down[l])
        x = x + gate_f * rms_norm(y, norm_g[l, 3])

    return x
```

```python
import functools

import jax
import jax.numpy as jnp
from jax import lax
from jax.experimental import pallas as pl
from jax.experimental.pallas import tpu as pltpu

F32 = jnp.float32
BF16 = jnp.bfloat16

LANES = 128
NORM_EPS = 1e-6
RW_HEAD = 64
RW_GN_EPS = RW_HEAD * 1e-5
RW_CHUNK = 64
MLA_HEAD_V = 128
MLA_NOPE = 128
MLA_ROPE = 64
ROPE_THETA = 10000.0
VMEM_LIMIT = 56 * 1024 * 1024


def _params(sem):
    return pltpu.CompilerParams(dimension_semantics=sem, vmem_limit_bytes=VMEM_LIMIT)


def _dot(a, b):
    return jnp.dot(a.astype(BF16), b.astype(BF16), preferred_element_type=F32)


def _dot_nt(a, b):
    return lax.dot_general(a.astype(BF16), b.astype(BF16), (((1,), (1,)), ((), ())),
                           preferred_element_type=F32)


def _rms(x):
    return x * lax.rsqrt(jnp.mean(x * x, axis=-1, keepdims=True) + NORM_EPS)


def _normmod(x, g, scale, shift):
    return _rms(x) * g * (1.0 + scale) + shift


def _sigmoid(x):
    return 1.0 / (1.0 + jnp.exp(-x))


def _ada_kernel(c_ref, w_ref, b_ref, o_ref):
    c = c_ref[...]
    ca = c * _sigmoid(c)
    o_ref[...] = jnp.dot(ca, w_ref[...], preferred_element_type=F32,
                         precision=lax.Precision.HIGHEST) + b_ref[...]


def _ada_mod(c, ada_w, ada_b):
    depth, d, n = ada_w.shape
    b = c.shape[0]
    tn = 1536 if n % 1536 == 0 else n
    return pl.pallas_call(
        _ada_kernel,
        grid=(depth, n // tn),
        in_specs=[
            pl.BlockSpec((b, d), lambda l, j: (0, 0)),
            pl.BlockSpec((None, d, tn), lambda l, j: (l, 0, j)),
            pl.BlockSpec((None, 1, tn), lambda l, j: (l, 0, j)),
        ],
        out_specs=pl.BlockSpec((None, b, tn), lambda l, j: (l, 0, j)),
        out_shape=jax.ShapeDtypeStruct((depth, b, n), F32),
        compiler_params=_params(("arbitrary", "arbitrary")),
    )(c, ada_w, ada_b.reshape(depth, 1, n))


def _rope_kernel(pos_ref, invf_ref, cos_ref, sin_ref):
    ang = pos_ref[...] * invf_ref[...]
    cos_ref[...] = jnp.cos(ang)
    sin_ref[...] = jnp.sin(ang)


def _rope_tables(positions):
    b, s = positions.shape
    half = MLA_ROPE // 2
    inv_freq = 1.0 / (ROPE_THETA ** (jnp.arange(0, MLA_ROPE, 2, dtype=F32) / MLA_ROPE))
    invf = jnp.tile(inv_freq, LANES // half).reshape(1, LANES)
    pos = jnp.broadcast_to(positions.astype(F32)[..., None], (b, s, LANES))
    tm = min(s, 512)
    spec = pl.BlockSpec((None, tm, LANES), lambda bi, i: (bi, i, 0))
    return pl.pallas_call(
        _rope_kernel,
        grid=(b, s // tm),
        in_specs=[spec, pl.BlockSpec((1, LANES), lambda bi, i: (0, 0))],
        out_specs=[spec, spec],
        out_shape=[jax.ShapeDtypeStruct((b, s, LANES), F32)] * 2,
        compiler_params=_params(("arbitrary", "arbitrary")),
    )(pos, invf)


def _shifted(x_ref, xp_ref, mod_ref, ng_ref, first_tile):
    g = ng_ref[0:1, :]
    shift = mod_ref[0:1, :]
    scale = mod_ref[1:2, :]
    h = _normmod(x_ref[...], g, scale, shift)
    hp = _normmod(xp_ref[7:8, :], g, scale, shift)
    hp = jnp.where(first_tile, 0.0, hp)
    hs = pltpu.roll(h, 1, 0)
    row = lax.broadcasted_iota(jnp.int32, (8, h.shape[1]), 0)
    top = jnp.where(row == 0, hp, hs[0:8, :])
    hs = jnp.concatenate([top, hs[8:, :]], axis=0)
    return h, hs - h


def _rw_proj_kernel(x_ref, xp_ref, mod_ref, ng_ref, mu_ref, w_ref, o_ref, xs_ref, *, nt):
    i = pl.program_id(1)
    j = pl.program_id(2)

    @pl.when(j == 0)
    def _():
        h, xx = _shifted(x_ref, xp_ref, mod_ref, ng_ref, i == 0)
        for s in range(3):
            xs_ref[s] = (h + xx * mu_ref[s:s + 1, :]).astype(BF16)

    o_ref[...] = jnp.dot(xs_ref[j // nt], w_ref[...], preferred_element_type=F32)


def _rw_proj(x, mod, ng, mu, w_rkv, tm, tn):
    b, s, d = x.shape
    nt = d // tn
    return pl.pallas_call(
        functools.partial(_rw_proj_kernel, nt=nt),
        grid=(b, s // tm, 3 * nt),
        in_specs=[
            pl.BlockSpec((None, tm, d), lambda bi, i, j: (bi, i, 0)),
            pl.BlockSpec((None, 8, d), lambda bi, i, j: (bi, jnp.maximum(i * (tm // 8) - 1, 0), 0)),
            pl.BlockSpec((None, 6, d), lambda bi, i, j: (bi, 0, 0)),
            pl.BlockSpec((4, d), lambda bi, i, j: (0, 0)),
            pl.BlockSpec((6, d), lambda bi, i, j: (0, 0)),
            pl.BlockSpec((None, d, tn), lambda bi, i, j: (j // nt, 0, j % nt)),
        ],
        out_specs=pl.BlockSpec((None, None, tm, tn), lambda bi, i, j: (j // nt, bi, i, j % nt)),
        out_shape=jax.ShapeDtypeStruct((3, b, s, d), F32),
        scratch_shapes=[pltpu.VMEM((3, tm, d), BF16)],
        compiler_params=_params(("arbitrary", "arbitrary", "arbitrary")),
    )(x, x, mod, ng, mu, w_rkv)


def _rw_lora_kernel(x_ref, xp_ref, mod_ref, ng_ref, mu_ref, w0_ref, w1_ref, w2_ref,
                    a0_ref, a1_ref, a2_ref, g1_ref, g2_ref, lw_ref, a_ref, g_ref):
    i = pl.program_id(1)
    h, xx = _shifted(x_ref, xp_ref, mod_ref, ng_ref, i == 0)
    xw = h + xx * mu_ref[3:4, :]
    xa = h + xx * mu_ref[4:5, :]
    xg = h + xx * mu_ref[5:6, :]
    wpre = w0_ref[...] + _dot(jnp.tanh(_dot(xw, w1_ref[...])), w2_ref[...])
    z = -wpre
    softplus = jnp.maximum(z, 0.0) + jnp.log(1.0 + jnp.exp(-jnp.abs(z)))
    lw_ref[...] = -jnp.exp(-softplus - 0.5)
    a_ref[...] = _sigmoid(a0_ref[...] + _dot(_dot(xa, a1_ref[...]), a2_ref[...]))
    g_ref[...] = _dot(_sigmoid(_dot(xg, g1_ref[...])), g2_ref[...])


def _rw_lora(x, mod, ng, mu, w0, w1, w2, a0, a1, a2, g1, g2, tm):
    b, s, d = x.shape
    full = lambda arr: pl.BlockSpec(arr.shape, lambda bi, i: (0,) * arr.ndim)
    row = pl.BlockSpec((None, tm, d), lambda bi, i: (bi, i, 0))
    return pl.pallas_call(
        _rw_lora_kernel,
        grid=(b, s // tm),
        in_specs=[
            row,
            pl.BlockSpec((None, 8, d), lambda bi, i: (bi, jnp.maximum(i * (tm // 8) - 1, 0), 0)),
            pl.BlockSpec((None, 6, d), lambda bi, i: (bi, 0, 0)),
            full(ng), full(mu), full(w0), full(w1), full(w2),
            full(a0), full(a1), full(a2), full(g1), full(g2),
        ],
        out_specs=[row, row, row],
        out_shape=[jax.ShapeDtypeStruct((b, s, d), F32)] * 3,
        compiler_params=_params(("arbitrary", "arbitrary")),
    )(x, x, mod, ng, mu, w0, w1, w2, a0, a1, a2, g1, g2)


def _split2(x):
    hi = x.astype(BF16)
    lo = (x - hi.astype(F32)).astype(BF16)
    return hi, lo


def _split3(x):
    hi = x.astype(BF16)
    r1 = x - hi.astype(F32)
    mid = r1.astype(BF16)
    lo = (r1 - mid.astype(F32)).astype(BF16)
    return hi, mid, lo


def _stack2(x, m0, m1):
    return jnp.concatenate([x * m0, x * m1], axis=0)


def _rw_scan_kernel(r_ref, k_ref, v_ref, lw_ref, a_ref, g_ref, kkw_ref, ka_ref, rk_ref,
                    lng_ref, lnb_ref, o_ref, s_ref, q_ref, m_ref, bt_ref, yi_ref, gc_ref, y_ref,
                    *, nchunk):
    C = RW_CHUNK
    C2 = 2 * C

    @pl.when(pl.program_id(2) == 0)
    def _():
        s_ref[...] = jnp.zeros_like(s_ref)

    lane = lax.broadcasted_iota(jnp.int32, (1, LANES), 1)
    m0 = (lane < RW_HEAD).astype(F32)
    m1 = 1.0 - m0
    ri = lax.broadcasted_iota(jnp.int32, (C2, C2), 0)
    ci = lax.broadcasted_iota(jnp.int32, (C2, C2), 1)
    same_head = (ri >> 6) == (ci >> 6)
    strict = (ri & (C - 1)) > (ci & (C - 1))
    incl = (ri & (C - 1)) >= (ci & (C - 1))
    eye = (ri == ci).astype(F32)
    off_masks = []
    for j in range(C.bit_length() - 1):
        off_masks.append(((ri >> (j + 1)) == (ci >> (j + 1)))
                         & (((ri >> j) & 1) == 1) & (((ci >> j) & 1) == 0))
    block_ones = same_head.astype(BF16)
    tri = (incl[:C, :C]).astype(BF16)

    def head_sum(x):
        hi, lo = _split2(x)
        return (jnp.dot(hi, block_ones, preferred_element_type=F32)
                + jnp.dot(lo, block_ones, preferred_element_type=F32))

    kkw = kkw_ref[...]
    ka = ka_ref[...]

    for c in range(nchunk):
        rows = pl.ds(c * C, C)
        R = r_ref[rows, :]
        K = k_ref[rows, :]
        V = v_ref[rows, :]
        LW = lw_ref[rows, :]
        A = a_ref[rows, :]

        kk0 = K * kkw
        nrm = jnp.maximum(jnp.sqrt(head_sum(kk0 * kk0)), 1e-12)
        kk = kk0 / nrm
        k2 = K * (1.0 + (A - 1.0) * ka)

        hi, mid, lo = _split3(LW)
        cl = (jnp.dot(tri, hi, preferred_element_type=F32)
              + jnp.dot(tri, mid, preferred_element_type=F32)
              + jnp.dot(tri, lo, preferred_element_type=F32))
        cl_end = cl[C - 1:C, :]
        gam = jnp.exp(cl)
        gam_inv = jnp.exp(-cl)
        gam_prev = jnp.exp(cl - LW)
        gam_tail = jnp.exp(cl_end - cl)

        al_s = _stack2(-kk * gam_prev, m0, m1)
        rb_s = _stack2(R * gam, m0, m1)
        be_s = _stack2(kk * A * gam_inv, m0, m1)
        kb_s = _stack2(k2 * gam_inv, m0, m1)
        bt_s = _stack2(kk * A * gam_tail, m0, m1)
        kt_s = _stack2(k2 * gam_tail, m0, m1)
        v_s = _stack2(V, m0, m1)

        G = _dot_nt(jnp.concatenate([al_s, rb_s], axis=0), jnp.concatenate([be_s, kb_s], axis=0))
        Lb = jnp.where(strict, G[:C2, :C2], 0.0)
        Lk = jnp.where(strict, G[:C2, C2:], 0.0)
        Rb = jnp.where(incl, G[C2:, :C2], 0.0)
        Rk = jnp.where(incl, G[C2:, C2:], 0.0)

        lkv = _dot(Lk, v_s)
        T = eye + jnp.where(off_masks[0], Lb, 0.0)
        for off in off_masks[1:]:
            T = T + _dot(_dot(T, jnp.where(off, Lb, 0.0)), T)

        Z = _dot(T, jnp.concatenate([al_s, lkv], axis=1))
        X = jnp.concatenate(
            [Z, jnp.concatenate([jnp.zeros((C2, C2), F32), v_s], axis=1)], axis=0)
        QY = _dot(jnp.concatenate([Rb, Rk], axis=1), X)
        MB = _dot(X.T, jnp.concatenate([bt_s, kt_s], axis=0))

        q_ref[c] = (rb_s + QY[:, :C2]).astype(BF16)
        yi_ref[c] = QY[:, C2:]
        m_ref[c] = MB[:C2, :].astype(BF16)
        bt_ref[c] = MB[C2:, :]
        gc_ref[c] = jnp.broadcast_to(jnp.exp(cl_end), (8, LANES))

    S = s_ref[...]
    for c in range(nchunk):
        Sb = S.astype(BF16)
        ys = _dot_nt(q_ref[c], Sb) + yi_ref[c]
        y_ref[pl.ds(c * C, C), :] = ys[:C, :] + ys[C:, :]
        S = S * gc_ref[c][0:1, :] + jnp.dot(Sb, m_ref[c], preferred_element_type=F32) + bt_ref[c]
    s_ref[...] = S

    y = y_ref[...]
    inv_n = 1.0 / RW_HEAD
    mean = head_sum(y) * inv_n
    yc = y - mean
    var = head_sum(yc * yc) * inv_n
    yn = yc * lax.rsqrt(var + RW_GN_EPS) * lng_ref[...] + lnb_ref[...]
    R = r_ref[...]
    K = k_ref[...]
    k2 = K * (1.0 + (a_ref[...] - 1.0) * ka)
    bonus = head_sum(R * k2 * rk_ref[...]) * v_ref[...]
    o_ref[...] = ((yn + bonus) * g_ref[...]).astype(o_ref.dtype)


def _rw_scan(rkv, lw, a, g, kkw, ka, rk, lng, lnb, tb):
    _, b, s, d = rkv.shape
    nchunk = tb // RW_CHUNK
    npair = d // LANES
    C2 = 2 * RW_CHUNK
    tok = pl.BlockSpec((None, tb, LANES), lambda bi, hp, t: (bi, t, hp))
    par = pl.BlockSpec((1, LANES), lambda bi, hp, t: (0, hp))

    def rkv_spec(which):
        return pl.BlockSpec((None, None, tb, LANES), lambda bi, hp, t: (which, bi, t, hp))

    return pl.pallas_call(
        functools.partial(_rw_scan_kernel, nchunk=nchunk),
        grid=(b, npair, s // tb),
        in_specs=[rkv_spec(0), rkv_spec(1), rkv_spec(2), tok, tok, tok, par, par, par, par, par],
        out_specs=tok,
        out_shape=jax.ShapeDtypeStruct((b, s, d), BF16),
        scratch_shapes=[
            pltpu.VMEM((C2, C2), F32),
            pltpu.VMEM((nchunk, C2, C2), BF16),
            pltpu.VMEM((nchunk, C2, C2), BF16),
            pltpu.VMEM((nchunk, C2, C2), F32),
            pltpu.VMEM((nchunk, C2, C2), F32),
            pltpu.VMEM((nchunk, 8, LANES), F32),
            pltpu.VMEM((tb, LANES), F32),
        ],
        compiler_params=_params(("arbitrary", "arbitrary", "arbitrary")),
    )(rkv, rkv, rkv, lw, a, g, kkw, ka, rk, lng, lnb)


def _oproj_kernel(a_ref, w_ref, x_ref, mod_ref, ng_ref, o_ref):
    y = jnp.dot(a_ref[...], w_ref[...], preferred_element_type=F32)
    o_ref[...] = x_ref[...] + mod_ref[2:3, :] * (_rms(y) * ng_ref[1:2, :])


def _oproj(a, w, x, mod, ng, tm):
    b, s, d = x.shape
    k = a.shape[-1]
    return pl.pallas_call(
        _oproj_kernel,
        grid=(b, s // tm),
        in_specs=[
            pl.BlockSpec((None, tm, k), lambda bi, i: (bi, i, 0)),
            pl.BlockSpec((k, d), lambda bi, i: (0, 0)),
            pl.BlockSpec((None, tm, d), lambda bi, i: (bi, i, 0)),
            pl.BlockSpec((None, 6, d), lambda bi, i: (bi, 0, 0)),
            pl.BlockSpec((4, d), lambda bi, i: (0, 0)),
        ],
        out_specs=pl.BlockSpec((None, tm, d), lambda bi, i: (bi, i, 0)),
        out_shape=jax.ShapeDtypeStruct((b, s, d), F32),
        compiler_params=_params(("arbitrary", "arbitrary")),
    )(a, w, x, mod, ng)


def _mlp_kernel(x_ref, mod_ref, ng_ref, up_ref, dn_ref, o_ref, h_ref, acc_ref):
    j = pl.program_id(2)

    @pl.when(j == 0)
    def _():
        h = _normmod(x_ref[...], ng_ref[2:3, :], mod_ref[4:5, :], mod_ref[3:4, :])
        h_ref[...] = h.astype(BF16)

    u = jnp.maximum(jnp.dot(h_ref[...], up_ref[...], preferred_element_type=F32), 0.0)
    part = jnp.dot((u * u).astype(BF16), dn_ref[...], preferred_element_type=F32)

    @pl.when(j == 0)
    def _():
        acc_ref[...] = part

    @pl.when(j > 0)
    def _():
        acc_ref[...] += part

    @pl.when(j == pl.num_programs(2) - 1)
    def _():
        o_ref[...] = x_ref[...] + mod_ref[5:6, :] * (_rms(acc_ref[...]) * ng_ref[3:4, :])


def _mlp(x, mod, ng, up, dn, tm, tf):
    b, s, d = x.shape
    f = up.shape[1]
    return pl.pallas_call(
        _mlp_kernel,
        grid=(b, s // tm, f // tf),
        in_specs=[
            pl.BlockSpec((None, tm, d), lambda bi, i, j: (bi, i, 0)),
            pl.BlockSpec((None, 6, d), lambda bi, i, j: (bi, 0, 0)),
            pl.BlockSpec((4, d), lambda bi, i, j: (0, 0)),
            pl.BlockSpec((d, tf), lambda bi, i, j: (0, j)),
            pl.BlockSpec((tf, d), lambda bi, i, j: (j, 0)),
        ],
        out_specs=pl.BlockSpec((None, tm, d), lambda bi, i, j: (bi, i, 0)),
        out_shape=jax.ShapeDtypeStruct((b, s, d), F32),
        scratch_shapes=[pltpu.VMEM((tm, d), BF16), pltpu.VMEM((tm, d), F32)],
        compiler_params=_params(("arbitrary", "arbitrary", "arbitrary")),
    )(x, mod, ng, up, dn)


def _mla_proj_kernel(x_ref, mod_ref, ng_ref, cos_ref, sin_ref, kvg_ref, kdc_ref, kdr_ref, kdrr_ref,
                     kvn_ref, uk_ref, uv_ref, dq_ref, qn_ref, uqn_ref, uqr_ref, uqrr_ref,
                     qnope_ref, qrope_ref, knope_ref, krope_ref, v_ref, *, scale):
    x = x_ref[...]
    xn = _rms(x)
    cos = cos_ref[...]
    sin = sin_ref[...]

    hs = (xn * kvg_ref[...]).astype(BF16)
    ckv = _rms(jnp.dot(hs, kdc_ref[...], preferred_element_type=F32)) * kvn_ref[...]
    ckv = ckv.astype(BF16)
    knope_ref[...] = jnp.dot(ckv, uk_ref[...], preferred_element_type=F32).astype(knope_ref.dtype)
    v_ref[...] = jnp.dot(ckv, uv_ref[...], preferred_element_type=F32).astype(v_ref.dtype)
    kr = (jnp.dot(hs, kdr_ref[...], preferred_element_type=F32) * cos
          + jnp.dot(hs, kdrr_ref[...], preferred_element_type=F32) * sin)
    krope_ref[...] = kr.astype(krope_ref.dtype)

    h = (xn * ng_ref[0:1, :] * (1.0 + mod_ref[1:2, :]) + mod_ref[0:1, :]).astype(BF16)
    cq = _rms(jnp.dot(h, dq_ref[...], preferred_element_type=F32)) * qn_ref[...]
    cq = cq.astype(BF16)
    qnope = jnp.dot(cq, uqn_ref[...], preferred_element_type=F32) * scale
    qnope_ref[...] = qnope.astype(qnope_ref.dtype)
    qr = jnp.dot(cq, uqr_ref[...], preferred_element_type=F32)
    qrr = jnp.dot(cq, uqrr_ref[...], preferred_element_type=F32)
    reps = qr.shape[1] // LANES
    cos_w = jnp.concatenate([cos] * reps, axis=1)
    sin_w = jnp.concatenate([sin] * reps, axis=1)
    qrope_ref[...] = ((qr * cos_w + qrr * sin_w) * scale).astype(qrope_ref.dtype)


def _mla_proj(x, mod, ng, cos_t, sin_t, kvg, kdc, kdr, kdrr, kvn, uk, uv, dq, qn, uqn, uqr, uqrr,
              scale, tm):
    b, s, d = x.shape
    dr = uqr.shape[1]
    full = lambda arr: pl.BlockSpec(arr.shape, lambda bi, i: (0,) * arr.ndim)
    row = lambda w: pl.BlockSpec((None, tm, w), lambda bi, i: (bi, i, 0))
    return pl.pallas_call(
        functools.partial(_mla_proj_kernel, scale=scale),
        grid=(b, s // tm),
        in_specs=[row(d), pl.BlockSpec((None, 6, d), lambda bi, i: (bi, 0, 0)), full(ng),
                  row(LANES), row(LANES), full(kvg), full(kdc), full(kdr), full(kdrr), full(kvn),
                  full(uk), full(uv), full(dq), full(qn), full(uqn), full(uqr), full(uqrr)],
        out_specs=[row(d), row(dr), row(d), row(LANES), row(d)],
        out_shape=[jax.ShapeDtypeStruct((b, s, d), BF16), jax.ShapeDtypeStruct((b, s, dr), BF16),
                   jax.ShapeDtypeStruct((b, s, d), BF16), jax.ShapeDtypeStruct((b, s, LANES), BF16),
                   jax.ShapeDtypeStruct((b, s, d), BF16)],
        compiler_params=_params(("arbitrary", "arbitrary")),
    )(x, mod, ng, cos_t, sin_t, kvg, kdc, kdr, kdrr, kvn, uk, uv, dq, qn, uqn, uqr, uqrr)


def _attn_kernel(qn_ref, qr_ref, kn_ref, kr_ref, v_ref, o_ref, *, tq):
    s = qn_ref.shape[0]
    neg = jnp.finfo(F32).min
    lane = lax.broadcasted_iota(jnp.int32, (1, LANES), 1)
    ri = lax.broadcasted_iota(jnp.int32, (tq, tq), 0)
    ci = lax.broadcasted_iota(jnp.int32, (tq, tq), 1)
    causal = ri >= ci
    kr = kr_ref[...]
    for hd in range(2):
        cols = slice(hd * MLA_NOPE, (hd + 1) * MLA_NOPE)
        in_head = (lane >> 6) == hd
        k_cat = jnp.concatenate([kn_ref[:, cols], kr], axis=1)
        for qi in range(s // tq):
            rows = slice(qi * tq, (qi + 1) * tq)
            q_rope = jnp.where(in_head, qr_ref[rows, :], jnp.zeros((), BF16))
            q_cat = jnp.concatenate([qn_ref[rows, cols], q_rope], axis=1)
            lo = qi * tq
            sd = _dot_nt(q_cat, k_cat[lo:lo + tq, :])
            sd = jnp.where(causal, sd, neg)
            m = jnp.max(sd, axis=-1, keepdims=True)
            if qi > 0:
                sl = _dot_nt(q_cat, k_cat[:lo, :])
                m = jnp.maximum(m, jnp.max(sl, axis=-1, keepdims=True))
                pl_ = jnp.exp(sl - m)
            pd = jnp.exp(sd - m)
            den = jnp.sum(pd, axis=-1, keepdims=True)
            acc = _dot(pd, v_ref[lo:lo + tq, cols])
            if qi > 0:
                den = den + jnp.sum(pl_, axis=-1, keepdims=True)
                acc = acc + _dot(pl_, v_ref[:lo, cols])
            o_ref[rows, cols] = (acc / den).astype(o_ref.dtype)


def _attention(qn, qr, kn, kr, v, tq):
    b, s, d = qn.shape
    npair = d // (2 * MLA_NOPE)
    wide = pl.BlockSpec((None, s, 2 * MLA_NOPE), lambda bi, hp: (bi, 0, hp))
    return pl.pallas_call(
        functools.partial(_attn_kernel, tq=tq),
        grid=(b, npair),
        in_specs=[wide, pl.BlockSpec((None, s, LANES), lambda bi, hp: (bi, 0, hp)), wide,
                  pl.BlockSpec((None, s, LANES), lambda bi, hp: (bi, 0, 0)), wide],
        out_specs=wide,
        out_shape=jax.ShapeDtypeStruct((b, s, d), BF16),
        compiler_params=_params(("arbitrary", "arbitrary")),
    )(qn, qr, kn, kr, v)


def _pad_cols(w, n):
    return jnp.pad(w, ((0, 0), (0, n - w.shape[1])))


def _pad_rows(w, n):
    return jnp.pad(w, ((0, n - w.shape[0]), (0, 0)))


def _rot_half_cols(w):
    k, n = w.shape
    half = MLA_ROPE // 2
    w3 = w.reshape(k, n // MLA_ROPE, MLA_ROPE)
    return jnp.concatenate([-w3[..., half:], w3[..., :half]], axis=-1).reshape(k, n)


def kernel(x, c, positions, ada_w, ada_b, norm_g, mlp_up, mlp_down, rw_mu, rw_rkv, rw_w0, rw_w1,
           rw_w2, rw_a0, rw_a1, rw_a2, rw_g1, rw_g2, rw_kk, rw_ka, rw_rk, rw_lnx, rw_o, mla_dq,
           mla_qnorm, mla_uq, mla_o, kv_in_g, kv_down, kv_norm, kv_uk, kv_uv):
    b, s, d = x.shape
    depth = ada_w.shape[0]
    n_rw = rw_mu.shape[0]
    kv_lora = kv_norm.shape[0]
    heads = d // MLA_HEAD_V
    assert d % (2 * MLA_NOPE) == 0 and s % RW_CHUNK == 0

    tm = min(s, 512)
    tm_small = min(s, 256)
    tn = min(d, 1024)
    tf = min(mlp_up.shape[2], 1024)
    tb = min(s, 512)
    tq = min(s, 512)

    mod_all = _ada_mod(c, ada_w, ada_b).reshape(depth, b, 6, d)
    cos_t, sin_t = _rope_tables(positions)
    shared = None

    for l in range(depth):
        mod = mod_all[l]
        ng = norm_g[l]
        if l < n_rw:
            i = l
            lora = max(LANES, -(-rw_w1.shape[2] // LANES) * LANES)
            rkv = _rw_proj(x, mod, ng, rw_mu[i], rw_rkv[i].astype(BF16), tm, tn)
            lw, a, g = _rw_lora(
                x, mod, ng, rw_mu[i], rw_w0[i].reshape(1, d),
                _pad_cols(rw_w1[i], lora).astype(BF16), _pad_rows(rw_w2[i], lora).astype(BF16),
                rw_a0[i].reshape(1, d),
                _pad_cols(rw_a1[i], lora).astype(BF16), _pad_rows(rw_a2[i], lora).astype(BF16),
                rw_g1[i].astype(BF16), rw_g2[i].astype(BF16), tm_small)
            mixed = _rw_scan(rkv, lw, a, g, rw_kk[i].reshape(1, d), rw_ka[i].reshape(1, d),
                             rw_rk[i].reshape(1, d), rw_lnx[i, 0].reshape(1, d),
                             rw_lnx[i, 1].reshape(1, d), tb)
            x = _oproj(mixed, rw_o[i].astype(BF16), x, mod, ng, tm)
        else:
            i = l - n_rw
            uq = mla_uq[i]
            q_lora = uq.shape[0]
            uqn = uq[:, :, :MLA_NOPE].reshape(q_lora, heads * MLA_NOPE)
            uqr = uq[:, :, MLA_NOPE:].reshape(q_lora, heads * MLA_ROPE)
            kdr = kv_down[:, kv_lora:]
            kdr2 = jnp.concatenate([kdr, kdr], axis=1)
            scale = float((MLA_NOPE + MLA_ROPE) ** -0.5)
            qn, qr, kn, kr, v = _mla_proj(
                x, mod, ng, cos_t, sin_t, kv_in_g.reshape(1, d),
                kv_down[:, :kv_lora].astype(BF16), kdr2.astype(BF16),
                _rot_half_cols(kdr2).astype(BF16), kv_norm.reshape(1, kv_lora),
                kv_uk.reshape(kv_lora, -1).astype(BF16), kv_uv.reshape(kv_lora, -1).astype(BF16),
                mla_dq[i].astype(BF16), mla_qnorm[i].reshape(1, q_lora), uqn.astype(BF16),
                uqr.astype(BF16), _rot_half_cols(uqr).astype(BF16), scale, tm_small)
            if shared is None:
                shared = (kn, kr, v)
            kn, kr, v = shared
            att = _attention(qn, qr, kn, kr, v, tq)
            x = _oproj(att, mla_o[i].astype(BF16), x, mod, ng, tm)
        x = _mlp(x, mod, ng, mlp_up[l].astype(BF16), mlp_down[l].astype(BF16), tm, tf)
    return x
```

```python
import functools

import jax
import jax.numpy as jnp
from jax import lax
from jax.experimental import pallas as pl
from jax.experimental.pallas import tpu as pltpu

F32 = jnp.float32
BF16 = jnp.bfloat16

LANES = 128
NORM_EPS = 1e-6
RW_HEAD = 64
RW_GN_EPS = RW_HEAD * 1e-5
RW_CHUNK = 64
MLA_HEAD_V = 128
MLA_NOPE = 128
MLA_ROPE = 64
ROPE_THETA = 10000.0
VMEM_LIMIT = 56 * 1024 * 1024


def _params(sem):
    return pltpu.CompilerParams(dimension_semantics=sem, vmem_limit_bytes=VMEM_LIMIT)


def _dot(a, b):
    return jnp.dot(a.astype(BF16), b.astype(BF16), preferred_element_type=F32)


def _dot_nt(a, b):
    return lax.dot_general(a.astype(BF16), b.astype(BF16), (((1,), (1,)), ((), ())),
                           preferred_element_type=F32)


def _rms(x):
    return x * lax.rsqrt(jnp.mean(x * x, axis=-1, keepdims=True) + NORM_EPS)


def _normmod(x, g, scale, shift):
    return _rms(x) * g * (1.0 + scale) + shift


def _sigmoid(x):
    return 1.0 / (1.0 + jnp.exp(-x))


def _ada_kernel(c_ref, w_ref, b_ref, o_ref):
    c = c_ref[...]
    ca = c * _sigmoid(c)
    o_ref[...] = jnp.dot(ca, w_ref[...], preferred_element_type=F32,
                         precision=lax.Precision.HIGHEST) + b_ref[...]


def _ada_mod(c, ada_w, ada_b):
    depth, d, n = ada_w.shape
    b = c.shape[0]
    tn = 1536 if n % 1536 == 0 else n
    return pl.pallas_call(
        _ada_kernel,
        grid=(depth, n // tn),
        in_specs=[
            pl.BlockSpec((b, d), lambda l, j: (0, 0)),
            pl.BlockSpec((None, d, tn), lambda l, j: (l, 0, j)),
            pl.BlockSpec((None, 1, tn), lambda l, j: (l, 0, j)),
        ],
        out_specs=pl.BlockSpec((None, b, tn), lambda l, j: (l, 0, j)),
        out_shape=jax.ShapeDtypeStruct((depth, b, n), F32),
        compiler_params=_params(("arbitrary", "arbitrary")),
    )(c, ada_w, ada_b.reshape(depth, 1, n))


def _rope_kernel(pos_ref, invf_ref, cos_ref, sin_ref):
    ang = pos_ref[...] * invf_ref[...]
    cos_ref[...] = jnp.cos(ang)
    sin_ref[...] = jnp.sin(ang)


def _rope_tables(positions):
    b, s = positions.shape
    half = MLA_ROPE // 2
    inv_freq = 1.0 / (ROPE_THETA ** (jnp.arange(0, MLA_ROPE, 2, dtype=F32) / MLA_ROPE))
    invf = jnp.tile(inv_freq, LANES // half).reshape(1, LANES)
    pos = jnp.broadcast_to(positions.astype(F32)[..., None], (b, s, LANES))
    tm = min(s, 512)
    spec = pl.BlockSpec((None, tm, LANES), lambda bi, i: (bi, i, 0))
    return pl.pallas_call(
        _rope_kernel,
        grid=(b, s // tm),
        in_specs=[spec, pl.BlockSpec((1, LANES), lambda bi, i: (0, 0))],
        out_specs=[spec, spec],
        out_shape=[jax.ShapeDtypeStruct((b, s, LANES), F32)] * 2,
        compiler_params=_params(("arbitrary", "arbitrary")),
    )(pos, invf)


def _shifted(x_ref, xp_ref, mod_ref, ng_ref, first_tile):
    g = ng_ref[0:1, :]
    shift = mod_ref[0:1, :]
    scale = mod_ref[1:2, :]
    h = _normmod(x_ref[...], g, scale, shift)
    hp = _normmod(xp_ref[7:8, :], g, scale, shift)
    hp = jnp.where(first_tile, 0.0, hp)
    hs = pltpu.roll(h, 1, 0)
    row = lax.broadcasted_iota(jnp.int32, (8, h.shape[1]), 0)
    top = jnp.where(row == 0, hp, hs[0:8, :])
    hs = jnp.concatenate([top, hs[8:, :]], axis=0)
    return h, hs - h


def _rw_proj_kernel(x_ref, xp_ref, mod_ref, ng_ref, mu_ref, w_ref, o_ref, xs_ref, *, nt):
    i = pl.program_id(1)
    j = pl.program_id(2)

    @pl.when(j == 0)
    def _():
        h, xx = _shifted(x_ref, xp_ref, mod_ref, ng_ref, i == 0)
        for s in range(3):
            xs_ref[s] = (h + xx * mu_ref[s:s + 1, :]).astype(BF16)

    o_ref[...] = jnp.dot(xs_ref[j // nt], w_ref[...], preferred_element_type=F32)


def _rw_proj(x, mod, ng, mu, w_rkv, tm, tn):
    b, s, d = x.shape
    nt = d // tn
    return pl.pallas_call(
        functools.partial(_rw_proj_kernel, nt=nt),
        grid=(b, s // tm, 3 * nt),
        in_specs=[
            pl.BlockSpec((None, tm, d), lambda bi, i, j: (bi, i, 0)),
            pl.BlockSpec((None, 8, d), lambda bi, i, j: (bi, jnp.maximum(i * (tm // 8) - 1, 0), 0)),
            pl.BlockSpec((None, 6, d), lambda bi, i, j: (bi, 0, 0)),
            pl.BlockSpec((4, d), lambda bi, i, j: (0, 0)),
            pl.BlockSpec((6, d), lambda bi, i, j: (0, 0)),
            pl.BlockSpec((None, d, tn), lambda bi, i, j: (j // nt, 0, j % nt)),
        ],
        out_specs=pl.BlockSpec((None, None, tm, tn), lambda bi, i, j: (j // nt, bi, i, j % nt)),
        out_shape=jax.ShapeDtypeStruct((3, b, s, d), F32),
        scratch_shapes=[pltpu.VMEM((3, tm, d), BF16)],
        compiler_params=_params(("arbitrary", "arbitrary", "arbitrary")),
    )(x, x, mod, ng, mu, w_rkv)


def _rw_lora_kernel(x_ref, xp_ref, mod_ref, ng_ref, mu_ref, w0_ref, w1_ref, w2_ref,
                    a0_ref, a1_ref, a2_ref, g1_ref, g2_ref, lw_ref, a_ref, g_ref):
    i = pl.program_id(1)
    h, xx = _shifted(x_ref, xp_ref, mod_ref, ng_ref, i == 0)
    xw = h + xx * mu_ref[3:4, :]
    xa = h + xx * mu_ref[4:5, :]
    xg = h + xx * mu_ref[5:6, :]
    wpre = w0_ref[...] + _dot(jnp.tanh(_dot(xw, w1_ref[...])), w2_ref[...])
    z = -wpre
    softplus = jnp.maximum(z, 0.0) + jnp.log(1.0 + jnp.exp(-jnp.abs(z)))
    lw_ref[...] = -jnp.exp(-softplus - 0.5)
    a_ref[...] = _sigmoid(a0_ref[...] + _dot(_dot(xa, a1_ref[...]), a2_ref[...]))
    g_ref[...] = _dot(_sigmoid(_dot(xg, g1_ref[...])), g2_ref[...])


def _rw_lora(x, mod, ng, mu, w0, w1, w2, a0, a1, a2, g1, g2, tm):
    b, s, d = x.shape
    full = lambda arr: pl.BlockSpec(arr.shape, lambda bi, i: (0,) * arr.ndim)
    row = pl.BlockSpec((None, tm, d), lambda bi, i: (bi, i, 0))
    return pl.pallas_call(
        _rw_lora_kernel,
        grid=(b, s // tm),
        in_specs=[
            row,
            pl.BlockSpec((None, 8, d), lambda bi, i: (bi, jnp.maximum(i * (tm // 8) - 1, 0), 0)),
            pl.BlockSpec((None, 6, d), lambda bi, i: (bi, 0, 0)),
            full(ng), full(mu), full(w0), full(w1), full(w2),
            full(a0), full(a1), full(a2), full(g1), full(g2),
        ],
        out_specs=[row, row, row],
        out_shape=[jax.ShapeDtypeStruct((b, s, d), F32)] * 3,
        compiler_params=_params(("arbitrary", "arbitrary")),
    )(x, x, mod, ng, mu, w0, w1, w2, a0, a1, a2, g1, g2)


def _split2(x):
    hi = x.astype(BF16)
    lo = (x - hi.astype(F32)).astype(BF16)
    return hi, lo


def _split3(x):
    hi = x.astype(BF16)
    r1 = x - hi.astype(F32)
    mid = r1.astype(BF16)
    lo = (r1 - mid.astype(F32)).astype(BF16)
    return hi, mid, lo


def _stack2(x, m0, m1):
    return jnp.concatenate([x * m0, x * m1], axis=0)


def _rw_scan_kernel(r_ref, k_ref, v_ref, lw_ref, a_ref, g_ref, kkw_ref, ka_ref, rk_ref,
                    lng_ref, lnb_ref, o_ref, s_ref, q_ref, m_ref, bt_ref, yi_ref, gc_ref, y_ref,
                    *, nchunk, npp):
    C = RW_CHUNK
    C2 = 2 * C

    @pl.when(pl.program_id(2) == 0)
    def _():
        s_ref[...] = jnp.zeros_like(s_ref)

    lane = lax.broadcasted_iota(jnp.int32, (1, LANES), 1)
    m0 = (lane < RW_HEAD).astype(F32)
    m1 = 1.0 - m0
    ri = lax.broadcasted_iota(jnp.int32, (C2, C2), 0)
    ci = lax.broadcasted_iota(jnp.int32, (C2, C2), 1)
    same_head = (ri >> 6) == (ci >> 6)
    strict = (ri & (C - 1)) > (ci & (C - 1))
    incl = (ri & (C - 1)) >= (ci & (C - 1))
    eye = (ri == ci).astype(F32)
    off_masks = []
    for j in range(C.bit_length() - 1):
        off_masks.append(((ri >> (j + 1)) == (ci >> (j + 1)))
                         & (((ri >> j) & 1) == 1) & (((ci >> j) & 1) == 0))
    block_ones = same_head.astype(BF16)
    tri = (incl[:C, :C]).astype(BF16)

    def head_sum(x):
        hi, lo = _split2(x)
        return (jnp.dot(hi, block_ones, preferred_element_type=F32)
                + jnp.dot(lo, block_ones, preferred_element_type=F32))

    def stack(x):
        return _stack2(x, m0, m1).astype(BF16)

    units = []
    for pp in range(npp):
        cols = slice(pp * LANES, (pp + 1) * LANES)
        K = k_ref[:, cols]
        A = a_ref[:, cols]
        LW = lw_ref[:, cols]
        kk0 = K * kkw_ref[:, cols]
        kk = kk0 / jnp.maximum(jnp.sqrt(head_sum(kk0 * kk0)), 1e-12)
        k2 = K * (1.0 + (A - 1.0) * ka_ref[:, cols])
        kka = kk * A
        cl_cat = None
        for part in _split3(LW):
            cat = jnp.concatenate([part[c * C:(c + 1) * C, :] for c in range(nchunk)], axis=1)
            term = jnp.dot(tri, cat, preferred_element_type=F32)
            cl_cat = term if cl_cat is None else cl_cat + term
        for c in range(nchunk):
            rows = slice(c * C, (c + 1) * C)
            cl = cl_cat[:, c * LANES:(c + 1) * LANES]
            cl_end = cl[C - 1:C, :]
            gam_inv = jnp.exp(-cl)
            gam_tail = jnp.exp(cl_end - cl)
            rb_s = _stack2(r_ref[rows, cols] * jnp.exp(cl), m0, m1)
            units.append(dict(
                idx=pp * nchunk + c,
                al_s=stack(-kk[rows] * jnp.exp(cl - LW[rows])),
                rb_s=rb_s,
                be_s=stack(kka[rows] * gam_inv),
                kb_s=stack(k2[rows] * gam_inv),
                bt_s=stack(kka[rows] * gam_tail),
                kt_s=stack(k2[rows] * gam_tail),
                v_s=stack(v_ref[rows, cols]),
                gc=jnp.exp(cl_end)))

    for u in units:
        G = _dot_nt(jnp.concatenate([u["al_s"], u["rb_s"].astype(BF16)], axis=0),
                    jnp.concatenate([u["be_s"], u["kb_s"]], axis=0))
        u["Lb"] = jnp.where(strict, G[:C2, :C2], 0.0)
        u["Lk"] = jnp.where(strict, G[:C2, C2:], 0.0).astype(BF16)
        u["R"] = jnp.concatenate([jnp.where(incl, G[C2:, :C2], 0.0),
                                  jnp.where(incl, G[C2:, C2:], 0.0)], axis=1).astype(BF16)
    for u in units:
        u["lkv"] = _dot(u["Lk"], u["v_s"])
        u["T"] = eye + jnp.where(off_masks[0], u["Lb"], 0.0)
    for off in off_masks[1:]:
        for u in units:
            u["TL"] = _dot(u["T"], jnp.where(off, u["Lb"], 0.0))
        for u in units:
            u["T"] = u["T"] + _dot(u["TL"], u["T"])
    for u in units:
        u["Z"] = _dot(u["T"], jnp.concatenate([u["al_s"], u["lkv"].astype(BF16)], axis=1))
    for u in units:
        X = jnp.concatenate(
            [u["Z"], jnp.concatenate([jnp.zeros((C2, C2), F32), u["v_s"].astype(F32)], axis=1)],
            axis=0)
        QY = _dot(u["R"], X)
        MB = _dot(X.T, jnp.concatenate([u["bt_s"], u["kt_s"]], axis=0))
        i = u["idx"]
        q_ref[i] = (u["rb_s"] + QY[:, :C2]).astype(BF16)
        yi_ref[i] = QY[:, C2:]
        m_ref[i] = MB[:C2, :].astype(BF16)
        bt_ref[i] = MB[C2:, :]
        gc_ref[i] = jnp.broadcast_to(u["gc"], (8, LANES))

    S = [s_ref[pp] for pp in range(npp)]
    for c in range(nchunk):
        for pp in range(npp):
            i = pp * nchunk + c
            Sb = S[pp].astype(BF16)
            ys = _dot_nt(q_ref[i], Sb) + yi_ref[i]
            y_ref[c * C:(c + 1) * C, pp * LANES:(pp + 1) * LANES] = ys[:C, :] + ys[C:, :]
            S[pp] = (S[pp] * gc_ref[i][0:1, :]
                     + jnp.dot(Sb, m_ref[i], preferred_element_type=F32) + bt_ref[i])
    for pp in range(npp):
        s_ref[pp] = S[pp]

    inv_n = 1.0 / RW_HEAD
    for pp in range(npp):
        cols = slice(pp * LANES, (pp + 1) * LANES)
        y = y_ref[:, cols]
        mean = head_sum(y) * inv_n
        yc = y - mean
        var = head_sum(yc * yc) * inv_n
        yn = yc * lax.rsqrt(var + RW_GN_EPS) * lng_ref[:, cols] + lnb_ref[:, cols]
        k2 = k_ref[:, cols] * (1.0 + (a_ref[:, cols] - 1.0) * ka_ref[:, cols])
        bonus = head_sum(r_ref[:, cols] * k2 * rk_ref[:, cols]) * v_ref[:, cols]
        o_ref[:, cols] = ((yn + bonus) * g_ref[:, cols]).astype(o_ref.dtype)


def _rw_scan(rkv, lw, a, g, kkw, ka, rk, lng, lnb, tb, npp):
    _, b, s, d = rkv.shape
    nchunk = tb // RW_CHUNK
    width = npp * LANES
    C2 = 2 * RW_CHUNK
    nunit = npp * nchunk
    tok = pl.BlockSpec((None, tb, width), lambda bi, hp, t: (bi, t, hp))
    par = pl.BlockSpec((1, width), lambda bi, hp, t: (0, hp))

    def rkv_spec(which):
        return pl.BlockSpec((None, None, tb, width), lambda bi, hp, t: (which, bi, t, hp))

    return pl.pallas_call(
        functools.partial(_rw_scan_kernel, nchunk=nchunk, npp=npp),
        grid=(b, d // width, s // tb),
        in_specs=[rkv_spec(0), rkv_spec(1), rkv_spec(2), tok, tok, tok, par, par, par, par, par],
        out_specs=tok,
        out_shape=jax.ShapeDtypeStruct((b, s, d), BF16),
        scratch_shapes=[
            pltpu.VMEM((npp, C2, C2), F32),
            pltpu.VMEM((nunit, C2, C2), BF16),
            pltpu.VMEM((nunit, C2, C2), BF16),
            pltpu.VMEM((nunit, C2, C2), F32),
            pltpu.VMEM((nunit, C2, C2), F32),
            pltpu.VMEM((nunit, 8, LANES), F32),
            pltpu.VMEM((tb, width), F32),
        ],
        compiler_params=_params(("arbitrary", "arbitrary", "arbitrary")),
    )(rkv, rkv, rkv, lw, a, g, kkw, ka, rk, lng, lnb)


def _oproj_kernel(a_ref, w_ref, x_ref, mod_ref, ng_ref, o_ref):
    y = jnp.dot(a_ref[...], w_ref[...], preferred_element_type=F32)
    o_ref[...] = x_ref[...] + mod_ref[2:3, :] * (_rms(y) * ng_ref[1:2, :])


def _oproj(a, w, x, mod, ng, tm):
    b, s, d = x.shape
    k = a.shape[-1]
    return pl.pallas_call(
        _oproj_kernel,
        grid=(b, s // tm),
        in_specs=[
            pl.BlockSpec((None, tm, k), lambda bi, i: (bi, i, 0)),
            pl.BlockSpec((k, d), lambda bi, i: (0, 0)),
            pl.BlockSpec((None, tm, d), lambda bi, i: (bi, i, 0)),
            pl.BlockSpec((None, 6, d), lambda bi, i: (bi, 0, 0)),
            pl.BlockSpec((4, d), lambda bi, i: (0, 0)),
        ],
        out_specs=pl.BlockSpec((None, tm, d), lambda bi, i: (bi, i, 0)),
        out_shape=jax.ShapeDtypeStruct((b, s, d), F32),
        compiler_params=_params(("arbitrary", "arbitrary")),
    )(a, w, x, mod, ng)


def _mlp_kernel(x_ref, mod_ref, ng_ref, up_ref, dn_ref, o_ref, h_ref, acc_ref):
    j = pl.program_id(2)

    @pl.when(j == 0)
    def _():
        h = _normmod(x_ref[...], ng_ref[2:3, :], mod_ref[4:5, :], mod_ref[3:4, :])
        h_ref[...] = h.astype(BF16)

    u = jnp.maximum(jnp.dot(h_ref[...], up_ref[...], preferred_element_type=F32), 0.0)
    part = jnp.dot((u * u).astype(BF16), dn_ref[...], preferred_element_type=F32)

    @pl.when(j == 0)
    def _():
        acc_ref[...] = part

    @pl.when(j > 0)
    def _():
        acc_ref[...] += part

    @pl.when(j == pl.num_programs(2) - 1)
    def _():
        o_ref[...] = x_ref[...] + mod_ref[5:6, :] * (_rms(acc_ref[...]) * ng_ref[3:4, :])


def _mlp(x, mod, ng, up, dn, tm, tf):
    b, s, d = x.shape
    f = up.shape[1]
    return pl.pallas_call(
        _mlp_kernel,
        grid=(b, s // tm, f // tf),
        in_specs=[
            pl.BlockSpec((None, tm, d), lambda bi, i, j: (bi, i, 0)),
            pl.BlockSpec((None, 6, d), lambda bi, i, j: (bi, 0, 0)),
            pl.BlockSpec((4, d), lambda bi, i, j: (0, 0)),
            pl.BlockSpec((d, tf), lambda bi, i, j: (0, j)),
            pl.BlockSpec((tf, d), lambda bi, i, j: (j, 0)),
        ],
        out_specs=pl.BlockSpec((None, tm, d), lambda bi, i, j: (bi, i, 0)),
        out_shape=jax.ShapeDtypeStruct((b, s, d), F32),
        scratch_shapes=[pltpu.VMEM((tm, d), BF16), pltpu.VMEM((tm, d), F32)],
        compiler_params=_params(("arbitrary", "arbitrary", "arbitrary")),
    )(x, mod, ng, up, dn)


def _mla_proj_kernel(x_ref, mod_ref, ng_ref, cos_ref, sin_ref, kvg_ref, kdc_ref, kdr_ref, kdrr_ref,
                     kvn_ref, uk_ref, uv_ref, dq_ref, qn_ref, uqn_ref, uqr_ref, uqrr_ref,
                     qnope_ref, qrope_ref, knope_ref, krope_ref, v_ref, *, scale):
    x = x_ref[...]
    xn = _rms(x)
    cos = cos_ref[...]
    sin = sin_ref[...]

    hs = (xn * kvg_ref[...]).astype(BF16)
    ckv = _rms(jnp.dot(hs, kdc_ref[...], preferred_element_type=F32)) * kvn_ref[...]
    ckv = ckv.astype(BF16)
    knope_ref[...] = jnp.dot(ckv, uk_ref[...], preferred_element_type=F32).astype(knope_ref.dtype)
    v_ref[...] = jnp.dot(ckv, uv_ref[...], preferred_element_type=F32).astype(v_ref.dtype)
    kr = (jnp.dot(hs, kdr_ref[...], preferred_element_type=F32) * cos
          + jnp.dot(hs, kdrr_ref[...], preferred_element_type=F32) * sin)
    krope_ref[...] = kr.astype(krope_ref.dtype)

    h = (xn * ng_ref[0:1, :] * (1.0 + mod_ref[1:2, :]) + mod_ref[0:1, :]).astype(BF16)
    cq = _rms(jnp.dot(h, dq_ref[...], preferred_element_type=F32)) * qn_ref[...]
    cq = cq.astype(BF16)
    qnope = jnp.dot(cq, uqn_ref[...], preferred_element_type=F32) * scale
    qnope_ref[...] = qnope.astype(qnope_ref.dtype)
    qr = jnp.dot(cq, uqr_ref[...], preferred_element_type=F32)
    qrr = jnp.dot(cq, uqrr_ref[...], preferred_element_type=F32)
    reps = qr.shape[1] // LANES
    cos_w = jnp.concatenate([cos] * reps, axis=1)
    sin_w = jnp.concatenate([sin] * reps, axis=1)
    qrope_ref[...] = ((qr * cos_w + qrr * sin_w) * scale).astype(qrope_ref.dtype)


def _mla_proj(x, mod, ng, cos_t, sin_t, kvg, kdc, kdr, kdrr, kvn, uk, uv, dq, qn, uqn, uqr, uqrr,
              scale, tm):
    b, s, d = x.shape
    dr = uqr.shape[1]
    full = lambda arr: pl.BlockSpec(arr.shape, lambda bi, i: (0,) * arr.ndim)
    row = lambda w: pl.BlockSpec((None, tm, w), lambda bi, i: (bi, i, 0))
    return pl.pallas_call(
        functools.partial(_mla_proj_kernel, scale=scale),
        grid=(b, s // tm),
        in_specs=[row(d), pl.BlockSpec((None, 6, d), lambda bi, i: (bi, 0, 0)), full(ng),
                  row(LANES), row(LANES), full(kvg), full(kdc), full(kdr), full(kdrr), full(kvn),
                  full(uk), full(uv), full(dq), full(qn), full(uqn), full(uqr), full(uqrr)],
        out_specs=[row(d), row(dr), row(d), row(LANES), row(d)],
        out_shape=[jax.ShapeDtypeStruct((b, s, d), BF16), jax.ShapeDtypeStruct((b, s, dr), BF16),
                   jax.ShapeDtypeStruct((b, s, d), BF16), jax.ShapeDtypeStruct((b, s, LANES), BF16),
                   jax.ShapeDtypeStruct((b, s, d), BF16)],
        compiler_params=_params(("arbitrary", "arbitrary")),
    )(x, mod, ng, cos_t, sin_t, kvg, kdc, kdr, kdrr, kvn, uk, uv, dq, qn, uqn, uqr, uqrr)


def _attn_kernel(qn_ref, qr_ref, kn_ref, kr_ref, v_ref, o_ref, *, tq):
    s = qn_ref.shape[0]
    neg = jnp.finfo(F32).min
    lane = lax.broadcasted_iota(jnp.int32, (1, LANES), 1)
    ri = lax.broadcasted_iota(jnp.int32, (tq, tq), 0)
    ci = lax.broadcasted_iota(jnp.int32, (tq, tq), 1)
    causal = ri >= ci
    kr = kr_ref[...]
    for hd in range(2):
        cols = slice(hd * MLA_NOPE, (hd + 1) * MLA_NOPE)
        in_head = (lane >> 6) == hd
        k_cat = jnp.concatenate([kn_ref[:, cols], kr], axis=1)
        for qi in range(s // tq):
            rows = slice(qi * tq, (qi + 1) * tq)
            q_rope = jnp.where(in_head, qr_ref[rows, :], jnp.zeros((), BF16))
            q_cat = jnp.concatenate([qn_ref[rows, cols], q_rope], axis=1)
            lo = qi * tq
            sd = _dot_nt(q_cat, k_cat[lo:lo + tq, :])
            sd = jnp.where(causal, sd, neg)
            m = jnp.max(sd, axis=-1, keepdims=True)
            if qi > 0:
                sl = _dot_nt(q_cat, k_cat[:lo, :])
                m = jnp.maximum(m, jnp.max(sl, axis=-1, keepdims=True))
                pl_ = jnp.exp(sl - m)
            pd = jnp.exp(sd - m)
            den = jnp.sum(pd, axis=-1, keepdims=True)
            acc = _dot(pd, v_ref[lo:lo + tq, cols])
            if qi > 0:
                den = den + jnp.sum(pl_, axis=-1, keepdims=True)
                acc = acc + _dot(pl_, v_ref[:lo, cols])
            o_ref[rows, cols] = (acc / den).astype(o_ref.dtype)


def _attention(qn, qr, kn, kr, v, tq):
    b, s, d = qn.shape
    npair = d // (2 * MLA_NOPE)
    wide = pl.BlockSpec((None, s, 2 * MLA_NOPE), lambda bi, hp: (bi, 0, hp))
    return pl.pallas_call(
        functools.partial(_attn_kernel, tq=tq),
        grid=(b, npair),
        in_specs=[wide, pl.BlockSpec((None, s, LANES), lambda bi, hp: (bi, 0, hp)), wide,
                  pl.BlockSpec((None, s, LANES), lambda bi, hp: (bi, 0, 0)), wide],
        out_specs=wide,
        out_shape=jax.ShapeDtypeStruct((b, s, d), BF16),
        compiler_params=_params(("arbitrary", "arbitrary")),
    )(qn, qr, kn, kr, v)


def _pad_cols(w, n):
    return jnp.pad(w, ((0, 0), (0, n - w.shape[1])))


def _pad_rows(w, n):
    return jnp.pad(w, ((0, n - w.shape[0]), (0, 0)))


def _rot_half_cols(w):
    k, n = w.shape
    half = MLA_ROPE // 2
    w3 = w.reshape(k, n // MLA_ROPE, MLA_ROPE)
    return jnp.concatenate([-w3[..., half:], w3[..., :half]], axis=-1).reshape(k, n)


def kernel(x, c, positions, ada_w, ada_b, norm_g, mlp_up, mlp_down, rw_mu, rw_rkv, rw_w0, rw_w1,
           rw_w2, rw_a0, rw_a1, rw_a2, rw_g1, rw_g2, rw_kk, rw_ka, rw_rk, rw_lnx, rw_o, mla_dq,
           mla_qnorm, mla_uq, mla_o, kv_in_g, kv_down, kv_norm, kv_uk, kv_uv):
    b, s, d = x.shape
    depth = ada_w.shape[0]
    n_rw = rw_mu.shape[0]
    kv_lora = kv_norm.shape[0]
    heads = d // MLA_HEAD_V
    assert d % (2 * MLA_NOPE) == 0 and s % RW_CHUNK == 0

    tm = min(s, 512)
    tm_small = min(s, 256)
    tn = min(d, 1024)
    tf = min(mlp_up.shape[2], 1024)
    tb = min(s, 512)
    tq = min(s, 512)
    npp = 2 if d % (2 * LANES) == 0 else 1

    mod_all = _ada_mod(c, ada_w, ada_b).reshape(depth, b, 6, d)
    cos_t, sin_t = _rope_tables(positions)
    shared = None

    for l in range(depth):
        mod = mod_all[l]
        ng = norm_g[l]
        if l < n_rw:
            i = l
            lora = max(LANES, -(-rw_w1.shape[2] // LANES) * LANES)
            rkv = _rw_proj(x, mod, ng, rw_mu[i], rw_rkv[i].astype(BF16), tm, tn)
            lw, a, g = _rw_lora(
                x, mod, ng, rw_mu[i], rw_w0[i].reshape(1, d),
                _pad_cols(rw_w1[i], lora).astype(BF16), _pad_rows(rw_w2[i], lora).astype(BF16),
                rw_a0[i].reshape(1, d),
                _pad_cols(rw_a1[i], lora).astype(BF16), _pad_rows(rw_a2[i], lora).astype(BF16),
                rw_g1[i].astype(BF16), rw_g2[i].astype(BF16), tm_small)
            mixed = _rw_scan(rkv, lw, a, g, rw_kk[i].reshape(1, d), rw_ka[i].reshape(1, d),
                             rw_rk[i].reshape(1, d), rw_lnx[i, 0].reshape(1, d),
                             rw_lnx[i, 1].reshape(1, d), tb, npp)
            x = _oproj(mixed, rw_o[i].astype(BF16), x, mod, ng, tm)
        else:
            i = l - n_rw
            uq = mla_uq[i]
            q_lora = uq.shape[0]
            uqn = uq[:, :, :MLA_NOPE].reshape(q_lora, heads * MLA_NOPE)
            uqr = uq[:, :, MLA_NOPE:].reshape(q_lora, heads * MLA_ROPE)
            kdr = kv_down[:, kv_lora:]
            kdr2 = jnp.concatenate([kdr, kdr], axis=1)
            scale = float((MLA_NOPE + MLA_ROPE) ** -0.5)
            qn, qr, kn, kr, v = _mla_proj(
                x, mod, ng, cos_t, sin_t, kv_in_g.reshape(1, d),
                kv_down[:, :kv_lora].astype(BF16), kdr2.astype(BF16),
                _rot_half_cols(kdr2).astype(BF16), kv_norm.reshape(1, kv_lora),
                kv_uk.reshape(kv_lora, -1).astype(BF16), kv_uv.reshape(kv_lora, -1).astype(BF16),
                mla_dq[i].astype(BF16), mla_qnorm[i].reshape(1, q_lora), uqn.astype(BF16),
                uqr.astype(BF16), _rot_half_cols(uqr).astype(BF16), scale, tm_small)
            if shared is None:
                shared = (kn, kr, v)
            kn, kr, v = shared
            att = _attention(qn, qr, kn, kr, v, tq)
            x = _oproj(att, mla_o[i].astype(BF16), x, mod, ng, tm)
        x = _mlp(x, mod, ng, mlp_up[l].astype(BF16), mlp_down[l].astype(BF16), tm, tf)
    return x
```

```python
import functools

import jax
import jax.numpy as jnp
from jax import lax
from jax.experimental import pallas as pl
from jax.experimental.pallas import tpu as pltpu

F32 = jnp.float32
BF16 = jnp.bfloat16

LANES = 128
NORM_EPS = 1e-6
RW_HEAD = 64
RW_GN_EPS = RW_HEAD * 1e-5
RW_CHUNK = 64
MLA_HEAD_V = 128
MLA_NOPE = 128
MLA_ROPE = 64
ROPE_THETA = 10000.0
VMEM_LIMIT = 56 * 1024 * 1024


def _params(sem):
    return pltpu.CompilerParams(dimension_semantics=sem, vmem_limit_bytes=VMEM_LIMIT)


def _dot(a, b):
    return jnp.dot(a.astype(BF16), b.astype(BF16), preferred_element_type=F32)


def _dot_nt(a, b):
    return lax.dot_general(a.astype(BF16), b.astype(BF16), (((1,), (1,)), ((), ())),
                           preferred_element_type=F32)


def _rms(x):
    return x * lax.rsqrt(jnp.mean(x * x, axis=-1, keepdims=True) + NORM_EPS)


def _normmod(x, g, scale, shift):
    return _rms(x) * g * (1.0 + scale) + shift


def _sigmoid(x):
    return 1.0 / (1.0 + jnp.exp(-x))


def _ada_kernel(c_ref, w_ref, b_ref, o_ref):
    c = c_ref[...]
    ca = c * _sigmoid(c)
    o_ref[...] = jnp.dot(ca, w_ref[...], preferred_element_type=F32,
                         precision=lax.Precision.HIGHEST) + b_ref[...]


def _ada_mod(c, ada_w, ada_b):
    depth, d, n = ada_w.shape
    b = c.shape[0]
    tn = 1536 if n % 1536 == 0 else n
    return pl.pallas_call(
        _ada_kernel,
        grid=(depth, n // tn),
        in_specs=[
            pl.BlockSpec((b, d), lambda l, j: (0, 0)),
            pl.BlockSpec((None, d, tn), lambda l, j: (l, 0, j)),
            pl.BlockSpec((None, 1, tn), lambda l, j: (l, 0, j)),
        ],
        out_specs=pl.BlockSpec((None, b, tn), lambda l, j: (l, 0, j)),
        out_shape=jax.ShapeDtypeStruct((depth, b, n), F32),
        compiler_params=_params(("arbitrary", "arbitrary")),
    )(c, ada_w, ada_b.reshape(depth, 1, n))


def _rope_kernel(pos_ref, invf_ref, cos_ref, sin_ref):
    ang = pos_ref[...] * invf_ref[...]
    cos_ref[...] = jnp.cos(ang)
    sin_ref[...] = jnp.sin(ang)


def _rope_tables(positions):
    b, s = positions.shape
    half = MLA_ROPE // 2
    inv_freq = 1.0 / (ROPE_THETA ** (jnp.arange(0, MLA_ROPE, 2, dtype=F32) / MLA_ROPE))
    invf = jnp.tile(inv_freq, LANES // half).reshape(1, LANES)
    pos = jnp.broadcast_to(positions.astype(F32)[..., None], (b, s, LANES))
    tm = min(s, 512)
    spec = pl.BlockSpec((None, tm, LANES), lambda bi, i: (bi, i, 0))
    return pl.pallas_call(
        _rope_kernel,
        grid=(b, s // tm),
        in_specs=[spec, pl.BlockSpec((1, LANES), lambda bi, i: (0, 0))],
        out_specs=[spec, spec],
        out_shape=[jax.ShapeDtypeStruct((b, s, LANES), F32)] * 2,
        compiler_params=_params(("arbitrary", "arbitrary")),
    )(pos, invf)


def _shifted(x_ref, xp_ref, mod_ref, ng_ref, first_tile):
    g = ng_ref[0:1, :]
    shift = mod_ref[0:1, :]
    scale = mod_ref[1:2, :]
    h = _normmod(x_ref[...], g, scale, shift)
    hp = _normmod(xp_ref[7:8, :], g, scale, shift)
    hp = jnp.where(first_tile, 0.0, hp)
    hs = pltpu.roll(h, 1, 0)
    row = lax.broadcasted_iota(jnp.int32, (8, h.shape[1]), 0)
    top = jnp.where(row == 0, hp, hs[0:8, :])
    hs = jnp.concatenate([top, hs[8:, :]], axis=0)
    return h, hs - h


def _rw_proj_kernel(x_ref, xp_ref, mod_ref, ng_ref, mu_ref, w_ref, r_ref, k_ref, v_ref, *, sub):
    i = pl.program_id(1)
    g = ng_ref[0:1, :]
    shift = mod_ref[0:1, :]
    scale = mod_ref[1:2, :]
    prev = jnp.where(i == 0, 0.0, _normmod(xp_ref[7:8, :], g, scale, shift))
    row = lax.broadcasted_iota(jnp.int32, (8, x_ref.shape[1]), 0)
    for sb in range(x_ref.shape[0] // sub):
        rows = slice(sb * sub, (sb + 1) * sub)
        h = _normmod(x_ref[rows, :], g, scale, shift)
        hs = pltpu.roll(h, 1, 0)
        hs = jnp.concatenate([jnp.where(row == 0, prev, hs[0:8, :]), hs[8:, :]], axis=0)
        prev = h[sub - 1:sub, :]
        xx = hs - h
        for s, o_ref in enumerate((r_ref, k_ref, v_ref)):
            xs = (h + xx * mu_ref[s:s + 1, :]).astype(BF16)
            o_ref[rows, :] = jnp.dot(xs, w_ref[s], preferred_element_type=F32).astype(o_ref.dtype)


def _rw_proj(x, mod, ng, mu, w_rkv, tm):
    b, s, d = x.shape
    row = pl.BlockSpec((None, tm, d), lambda bi, i: (bi, i, 0))
    return pl.pallas_call(
        functools.partial(_rw_proj_kernel, sub=tm // 2),
        grid=(b, s // tm),
        in_specs=[
            row,
            pl.BlockSpec((None, 8, d), lambda bi, i: (bi, jnp.maximum(i * (tm // 8) - 1, 0), 0)),
            pl.BlockSpec((None, 6, d), lambda bi, i: (bi, 0, 0)),
            pl.BlockSpec((4, d), lambda bi, i: (0, 0)),
            pl.BlockSpec((6, d), lambda bi, i: (0, 0)),
            pl.BlockSpec((3, d, d), lambda bi, i: (0, 0, 0), pipeline_mode=pl.Buffered(1)),
        ],
        out_specs=[row, row, row],
        out_shape=[jax.ShapeDtypeStruct((b, s, d), BF16)] * 3,
        compiler_params=_params(("arbitrary", "arbitrary")),
    )(x, x, mod, ng, mu, w_rkv)


def _rw_lora_kernel(x_ref, xp_ref, mod_ref, ng_ref, mu_ref, w0_ref, w1_ref, w2_ref,
                    a0_ref, a1_ref, a2_ref, g1_ref, g2_ref, lw_ref, a_ref, g_ref):
    i = pl.program_id(1)
    h, xx = _shifted(x_ref, xp_ref, mod_ref, ng_ref, i == 0)
    xw = h + xx * mu_ref[3:4, :]
    xa = h + xx * mu_ref[4:5, :]
    xg = h + xx * mu_ref[5:6, :]
    wpre = w0_ref[...] + _dot(jnp.tanh(_dot(xw, w1_ref[...])), w2_ref[...])
    z = -wpre
    softplus = jnp.maximum(z, 0.0) + jnp.log(1.0 + jnp.exp(-jnp.abs(z)))
    lw_ref[...] = -jnp.exp(-softplus - 0.5)
    a_ref[...] = _sigmoid(a0_ref[...] + _dot(_dot(xa, a1_ref[...]), a2_ref[...]))
    g_ref[...] = _dot(_sigmoid(_dot(xg, g1_ref[...])), g2_ref[...])


def _rw_lora(x, mod, ng, mu, w0, w1, w2, a0, a1, a2, g1, g2, tm):
    b, s, d = x.shape
    full = lambda arr: pl.BlockSpec(arr.shape, lambda bi, i: (0,) * arr.ndim)
    row = pl.BlockSpec((None, tm, d), lambda bi, i: (bi, i, 0))
    return pl.pallas_call(
        _rw_lora_kernel,
        grid=(b, s // tm),
        in_specs=[
            row,
            pl.BlockSpec((None, 8, d), lambda bi, i: (bi, jnp.maximum(i * (tm // 8) - 1, 0), 0)),
            pl.BlockSpec((None, 6, d), lambda bi, i: (bi, 0, 0)),
            full(ng), full(mu), full(w0), full(w1), full(w2),
            full(a0), full(a1), full(a2), full(g1), full(g2),
        ],
        out_specs=[row, row, row],
        out_shape=[jax.ShapeDtypeStruct((b, s, d), F32)] * 3,
        compiler_params=_params(("arbitrary", "arbitrary")),
    )(x, x, mod, ng, mu, w0, w1, w2, a0, a1, a2, g1, g2)


def _split2(x):
    hi = x.astype(BF16)
    lo = (x - hi.astype(F32)).astype(BF16)
    return hi, lo


def _split3(x):
    hi = x.astype(BF16)
    r1 = x - hi.astype(F32)
    mid = r1.astype(BF16)
    lo = (r1 - mid.astype(F32)).astype(BF16)
    return hi, mid, lo


def _stack2(x, m0, m1):
    return jnp.concatenate([x * m0, x * m1], axis=0)


def _rw_scan_kernel(r_ref, k_ref, v_ref, lw_ref, a_ref, g_ref, kkw_ref, ka_ref, rk_ref,
                    lng_ref, lnb_ref, o_ref, s_ref, q_ref, m_ref, bt_ref, yi_ref, gc_ref, y_ref,
                    *, nchunk, npp):
    C = RW_CHUNK
    C2 = 2 * C

    @pl.when(pl.program_id(2) == 0)
    def _():
        s_ref[...] = jnp.zeros_like(s_ref)

    lane = lax.broadcasted_iota(jnp.int32, (1, LANES), 1)
    m0 = (lane < RW_HEAD).astype(F32)
    m1 = 1.0 - m0
    ri = lax.broadcasted_iota(jnp.int32, (C2, C2), 0)
    ci = lax.broadcasted_iota(jnp.int32, (C2, C2), 1)
    same_head = (ri >> 6) == (ci >> 6)
    strict = (ri & (C - 1)) > (ci & (C - 1))
    incl = (ri & (C - 1)) >= (ci & (C - 1))
    eye = (ri == ci).astype(F32)
    off_masks = []
    for j in range(C.bit_length() - 1):
        off_masks.append(((ri >> (j + 1)) == (ci >> (j + 1)))
                         & (((ri >> j) & 1) == 1) & (((ci >> j) & 1) == 0))
    block_ones = same_head.astype(BF16)
    tri = (incl[:C, :C]).astype(BF16)

    def head_sum(x):
        hi, lo = _split2(x)
        return (jnp.dot(hi, block_ones, preferred_element_type=F32)
                + jnp.dot(lo, block_ones, preferred_element_type=F32))

    def stack(x):
        return _stack2(x, m0, m1).astype(BF16)

    units = []
    for pp in range(npp):
        cols = slice(pp * LANES, (pp + 1) * LANES)
        K = k_ref[:, cols]
        A = a_ref[:, cols]
        LW = lw_ref[:, cols]
        kk0 = K * kkw_ref[:, cols]
        kk = kk0 / jnp.maximum(jnp.sqrt(head_sum(kk0 * kk0)), 1e-12)
        k2 = K * (1.0 + (A - 1.0) * ka_ref[:, cols])
        kka = kk * A
        cl_cat = None
        for part in _split3(LW):
            cat = jnp.concatenate([part[c * C:(c + 1) * C, :] for c in range(nchunk)], axis=1)
            term = jnp.dot(tri, cat, preferred_element_type=F32)
            cl_cat = term if cl_cat is None else cl_cat + term
        for c in range(nchunk):
            rows = slice(c * C, (c + 1) * C)
            cl = cl_cat[:, c * LANES:(c + 1) * LANES]
            cl_end = cl[C - 1:C, :]
            gam_inv = jnp.exp(-cl)
            gam_tail = jnp.exp(cl_end - cl)
            rb_s = _stack2(r_ref[rows, cols] * jnp.exp(cl), m0, m1)
            units.append(dict(
                idx=pp * nchunk + c,
                al_s=stack(-kk[rows] * jnp.exp(cl - LW[rows])),
                rb_s=rb_s,
                be_s=stack(kka[rows] * gam_inv),
                kb_s=stack(k2[rows] * gam_inv),
                bt_s=stack(kka[rows] * gam_tail),
                kt_s=stack(k2[rows] * gam_tail),
                v_s=stack(v_ref[rows, cols]),
                gc=jnp.exp(cl_end)))

    for u in units:
        G = _dot_nt(jnp.concatenate([u["al_s"], u["rb_s"].astype(BF16)], axis=0),
                    jnp.concatenate([u["be_s"], u["kb_s"]], axis=0))
        u["Lb"] = jnp.where(strict, G[:C2, :C2], 0.0)
        u["Lk"] = jnp.where(strict, G[:C2, C2:], 0.0).astype(BF16)
        u["R"] = jnp.concatenate([jnp.where(incl, G[C2:, :C2], 0.0),
                                  jnp.where(incl, G[C2:, C2:], 0.0)], axis=1).astype(BF16)
    for u in units:
        u["lkv"] = _dot(u["Lk"], u["v_s"])
        u["T"] = eye + jnp.where(off_masks[0], u["Lb"], 0.0)
    for off in off_masks[1:]:
        for u in units:
            u["TL"] = _dot(u["T"], jnp.where(off, u["Lb"], 0.0))
        for u in units:
            u["T"] = u["T"] + _dot(u["TL"], u["T"])
    for u in units:
        u["Z"] = _dot(u["T"], jnp.concatenate([u["al_s"], u["lkv"].astype(BF16)], axis=1))
    for u in units:
        X = jnp.concatenate(
            [u["Z"], jnp.concatenate([jnp.zeros((C2, C2), F32), u["v_s"].astype(F32)], axis=1)],
            axis=0)
        QY = _dot(u["R"], X)
        MB = _dot(X.T, jnp.concatenate([u["bt_s"], u["kt_s"]], axis=0))
        i = u["idx"]
        q_ref[i] = (u["rb_s"] + QY[:, :C2]).astype(BF16)
        yi_ref[i] = QY[:, C2:]
        m_ref[i] = MB[:C2, :].astype(BF16)
        bt_ref[i] = MB[C2:, :]
        gc_ref[i] = jnp.broadcast_to(u["gc"], (8, LANES))

    S = [s_ref[pp] for pp in range(npp)]
    for c in range(nchunk):
        for pp in range(npp):
            i = pp * nchunk + c
            Sb = S[pp].astype(BF16)
            ys = _dot_nt(q_ref[i], Sb) + yi_ref[i]
            y_ref[c * C:(c + 1) * C, pp * LANES:(pp + 1) * LANES] = ys[:C, :] + ys[C:, :]
            S[pp] = (S[pp] * gc_ref[i][0:1, :]
                     + jnp.dot(Sb, m_ref[i], preferred_element_type=F32) + bt_ref[i])
    for pp in range(npp):
        s_ref[pp] = S[pp]

    inv_n = 1.0 / RW_HEAD
    for pp in range(npp):
        cols = slice(pp * LANES, (pp + 1) * LANES)
        y = y_ref[:, cols]
        mean = head_sum(y) * inv_n
        yc = y - mean
        var = head_sum(yc * yc) * inv_n
        yn = yc * lax.rsqrt(var + RW_GN_EPS) * lng_ref[:, cols] + lnb_ref[:, cols]
        k2 = k_ref[:, cols] * (1.0 + (a_ref[:, cols] - 1.0) * ka_ref[:, cols])
        bonus = head_sum(r_ref[:, cols] * k2 * rk_ref[:, cols]) * v_ref[:, cols]
        o_ref[:, cols] = ((yn + bonus) * g_ref[:, cols]).astype(o_ref.dtype)


def _rw_scan(r, k, v, lw, a, g, kkw, ka, rk, lng, lnb, tb, npp):
    b, s, d = r.shape
    nchunk = tb // RW_CHUNK
    width = npp * LANES
    C2 = 2 * RW_CHUNK
    nunit = npp * nchunk
    tok = pl.BlockSpec((None, tb, width), lambda bi, hp, t: (bi, t, hp))
    par = pl.BlockSpec((1, width), lambda bi, hp, t: (0, hp))

    return pl.pallas_call(
        functools.partial(_rw_scan_kernel, nchunk=nchunk, npp=npp),
        grid=(b, d // width, s // tb),
        in_specs=[tok, tok, tok, tok, tok, tok, par, par, par, par, par],
        out_specs=tok,
        out_shape=jax.ShapeDtypeStruct((b, s, d), BF16),
        scratch_shapes=[
            pltpu.VMEM((npp, C2, C2), F32),
            pltpu.VMEM((nunit, C2, C2), BF16),
            pltpu.VMEM((nunit, C2, C2), BF16),
            pltpu.VMEM((nunit, C2, C2), F32),
            pltpu.VMEM((nunit, C2, C2), F32),
            pltpu.VMEM((nunit, 8, LANES), F32),
            pltpu.VMEM((tb, width), F32),
        ],
        compiler_params=_params(("arbitrary", "arbitrary", "arbitrary")),
    )(r, k, v, lw, a, g, kkw, ka, rk, lng, lnb)


def _oproj_kernel(a_ref, w_ref, x_ref, mod_ref, ng_ref, o_ref):
    y = jnp.dot(a_ref[...], w_ref[...], preferred_element_type=F32)
    o_ref[...] = x_ref[...] + mod_ref[2:3, :] * (_rms(y) * ng_ref[1:2, :])


def _oproj(a, w, x, mod, ng, tm):
    b, s, d = x.shape
    k = a.shape[-1]
    return pl.pallas_call(
        _oproj_kernel,
        grid=(b, s // tm),
        in_specs=[
            pl.BlockSpec((None, tm, k), lambda bi, i: (bi, i, 0)),
            pl.BlockSpec((k, d), lambda bi, i: (0, 0)),
            pl.BlockSpec((None, tm, d), lambda bi, i: (bi, i, 0)),
            pl.BlockSpec((None, 6, d), lambda bi, i: (bi, 0, 0)),
            pl.BlockSpec((4, d), lambda bi, i: (0, 0)),
        ],
        out_specs=pl.BlockSpec((None, tm, d), lambda bi, i: (bi, i, 0)),
        out_shape=jax.ShapeDtypeStruct((b, s, d), F32),
        compiler_params=_params(("arbitrary", "arbitrary")),
    )(a, w, x, mod, ng)


def _mlp_kernel(x_ref, mod_ref, ng_ref, up_ref, dn_ref, o_ref, h_ref, acc_ref):
    j = pl.program_id(2)

    @pl.when(j == 0)
    def _():
        h = _normmod(x_ref[...], ng_ref[2:3, :], mod_ref[4:5, :], mod_ref[3:4, :])
        h_ref[...] = h.astype(BF16)
        acc_ref[...] = jnp.zeros_like(acc_ref)

    u = jnp.maximum(jnp.dot(h_ref[...], up_ref[...], preferred_element_type=F32), 0.0)
    acc_ref[...] += jnp.dot((u * u).astype(BF16), dn_ref[...], preferred_element_type=F32)

    @pl.when(j == pl.num_programs(2) - 1)
    def _():
        o_ref[...] = x_ref[...] + mod_ref[5:6, :] * (_rms(acc_ref[...]) * ng_ref[3:4, :])


def _mlp(x, mod, ng, up, dn, tm, tf):
    b, s, d = x.shape
    f = up.shape[1]
    return pl.pallas_call(
        _mlp_kernel,
        grid=(b, s // tm, f // tf),
        in_specs=[
            pl.BlockSpec((None, tm, d), lambda bi, i, j: (bi, i, 0)),
            pl.BlockSpec((None, 6, d), lambda bi, i, j: (bi, 0, 0)),
            pl.BlockSpec((4, d), lambda bi, i, j: (0, 0)),
            pl.BlockSpec((d, tf), lambda bi, i, j: (0, j)),
            pl.BlockSpec((tf, d), lambda bi, i, j: (j, 0)),
        ],
        out_specs=pl.BlockSpec((None, tm, d), lambda bi, i, j: (bi, i, 0)),
        out_shape=jax.ShapeDtypeStruct((b, s, d), F32),
        scratch_shapes=[pltpu.VMEM((tm, d), BF16), pltpu.VMEM((tm, d), F32)],
        compiler_params=_params(("arbitrary", "arbitrary", "arbitrary")),
    )(x, mod, ng, up, dn)


def _mla_proj_kernel(x_ref, mod_ref, ng_ref, cos_ref, sin_ref, kvg_ref, kdc_ref, kdr_ref, kdrr_ref,
                     kvn_ref, uk_ref, uv_ref, dq_ref, qn_ref, uqn_ref, uqr_ref, uqrr_ref,
                     qnope_ref, qrope_ref, knope_ref, krope_ref, v_ref, *, scale):
    x = x_ref[...]
    xn = _rms(x)
    cos = cos_ref[...]
    sin = sin_ref[...]

    hs = (xn * kvg_ref[...]).astype(BF16)
    ckv = _rms(jnp.dot(hs, kdc_ref[...], preferred_element_type=F32)) * kvn_ref[...]
    ckv = ckv.astype(BF16)
    knope_ref[...] = jnp.dot(ckv, uk_ref[...], preferred_element_type=F32).astype(knope_ref.dtype)
    v_ref[...] = jnp.dot(ckv, uv_ref[...], preferred_element_type=F32).astype(v_ref.dtype)
    kr = (jnp.dot(hs, kdr_ref[...], preferred_element_type=F32) * cos
          + jnp.dot(hs, kdrr_ref[...], preferred_element_type=F32) * sin)
    krope_ref[...] = kr.astype(krope_ref.dtype)

    h = (xn * ng_ref[0:1, :] * (1.0 + mod_ref[1:2, :]) + mod_ref[0:1, :]).astype(BF16)
    cq = _rms(jnp.dot(h, dq_ref[...], preferred_element_type=F32)) * qn_ref[...]
    cq = cq.astype(BF16)
    qnope = jnp.dot(cq, uqn_ref[...], preferred_element_type=F32) * scale
    qnope_ref[...] = qnope.astype(qnope_ref.dtype)
    qr = jnp.dot(cq, uqr_ref[...], preferred_element_type=F32)
    qrr = jnp.dot(cq, uqrr_ref[...], preferred_element_type=F32)
    reps = qr.shape[1] // LANES
    cos_w = jnp.concatenate([cos] * reps, axis=1)
    sin_w = jnp.concatenate([sin] * reps, axis=1)
    qrope_ref[...] = ((qr * cos_w + qrr * sin_w) * scale).astype(qrope_ref.dtype)


def _mla_proj(x, mod, ng, cos_t, sin_t, kvg, kdc, kdr, kdrr, kvn, uk, uv, dq, qn, uqn, uqr, uqrr,
              scale, tm):
    b, s, d = x.shape
    dr = uqr.shape[1]
    full = lambda arr: pl.BlockSpec(arr.shape, lambda bi, i: (0,) * arr.ndim)
    row = lambda w: pl.BlockSpec((None, tm, w), lambda bi, i: (bi, i, 0))
    return pl.pallas_call(
        functools.partial(_mla_proj_kernel, scale=scale),
        grid=(b, s // tm),
        in_specs=[row(d), pl.BlockSpec((None, 6, d), lambda bi, i: (bi, 0, 0)), full(ng),
                  row(LANES), row(LANES), full(kvg), full(kdc), full(kdr), full(kdrr), full(kvn),
                  full(uk), full(uv), full(dq), full(qn), full(uqn), full(uqr), full(uqrr)],
        out_specs=[row(d), row(dr), row(d), row(LANES), row(d)],
        out_shape=[jax.ShapeDtypeStruct((b, s, d), BF16), jax.ShapeDtypeStruct((b, s, dr), BF16),
                   jax.ShapeDtypeStruct((b, s, d), BF16), jax.ShapeDtypeStruct((b, s, LANES), BF16),
                   jax.ShapeDtypeStruct((b, s, d), BF16)],
        compiler_params=_params(("arbitrary", "arbitrary")),
    )(x, mod, ng, cos_t, sin_t, kvg, kdc, kdr, kdrr, kvn, uk, uv, dq, qn, uqn, uqr, uqrr)


def _attn_kernel(qn_ref, qr_ref, kn_ref, kr_ref, v_ref, o_ref, *, tq):
    s = qn_ref.shape[0]
    neg = jnp.finfo(F32).min
    lane = lax.broadcasted_iota(jnp.int32, (1, LANES), 1)
    ri = lax.broadcasted_iota(jnp.int32, (tq, tq), 0)
    ci = lax.broadcasted_iota(jnp.int32, (tq, tq), 1)
    causal = ri >= ci
    kr = kr_ref[...]
    for hd in range(2):
        cols = slice(hd * MLA_NOPE, (hd + 1) * MLA_NOPE)
        in_head = (lane >> 6) == hd
        k_cat = jnp.concatenate([kn_ref[:, cols], kr], axis=1)
        for qi in range(s // tq):
            rows = slice(qi * tq, (qi + 1) * tq)
            q_rope = jnp.where(in_head, qr_ref[rows, :], jnp.zeros((), BF16))
            q_cat = jnp.concatenate([qn_ref[rows, cols], q_rope], axis=1)
            lo = qi * tq
            sd = _dot_nt(q_cat, k_cat[lo:lo + tq, :])
            sd = jnp.where(causal, sd, neg)
            m = jnp.max(sd, axis=-1, keepdims=True)
            if qi > 0:
                sl = _dot_nt(q_cat, k_cat[:lo, :])
                m = jnp.maximum(m, jnp.max(sl, axis=-1, keepdims=True))
                pl_ = jnp.exp(sl - m)
            pd = jnp.exp(sd - m)
            den = jnp.sum(pd, axis=-1, keepdims=True)
            acc = _dot(pd, v_ref[lo:lo + tq, cols])
            if qi > 0:
                den = den + jnp.sum(pl_, axis=-1, keepdims=True)
                acc = acc + _dot(pl_, v_ref[:lo, cols])
            o_ref[rows, cols] = (acc / den).astype(o_ref.dtype)


def _attention(qn, qr, kn, kr, v, tq):
    b, s, d = qn.shape
    npair = d // (2 * MLA_NOPE)
    wide = pl.BlockSpec((None, s, 2 * MLA_NOPE), lambda bi, hp: (bi, 0, hp))
    return pl.pallas_call(
        functools.partial(_attn_kernel, tq=tq),
        grid=(b, npair),
        in_specs=[wide, pl.BlockSpec((None, s, LANES), lambda bi, hp: (bi, 0, hp)), wide,
                  pl.BlockSpec((None, s, LANES), lambda bi, hp: (bi, 0, 0)), wide],
        out_specs=wide,
        out_shape=jax.ShapeDtypeStruct((b, s, d), BF16),
        compiler_params=_params(("arbitrary", "arbitrary")),
    )(qn, qr, kn, kr, v)


def _pad_cols(w, n):
    return jnp.pad(w, ((0, 0), (0, n - w.shape[1])))


def _pad_rows(w, n):
    return jnp.pad(w, ((0, n - w.shape[0]), (0, 0)))


def _rot_half_cols(w):
    k, n = w.shape
    half = MLA_ROPE // 2
    w3 = w.reshape(k, n // MLA_ROPE, MLA_ROPE)
    return jnp.concatenate([-w3[..., half:], w3[..., :half]], axis=-1).reshape(k, n)


def kernel(x, c, positions, ada_w, ada_b, norm_g, mlp_up, mlp_down, rw_mu, rw_rkv, rw_w0, rw_w1,
           rw_w2, rw_a0, rw_a1, rw_a2, rw_g1, rw_g2, rw_kk, rw_ka, rw_rk, rw_lnx, rw_o, mla_dq,
           mla_qnorm, mla_uq, mla_o, kv_in_g, kv_down, kv_norm, kv_uk, kv_uv):
    b, s, d = x.shape
    depth = ada_w.shape[0]
    n_rw = rw_mu.shape[0]
    kv_lora = kv_norm.shape[0]
    heads = d // MLA_HEAD_V
    assert d % (2 * MLA_NOPE) == 0 and s % RW_CHUNK == 0

    tm = min(s, 512)
    tm_small = min(s, 256)
    tf = min(mlp_up.shape[2], 1024)
    tb = min(s, 512)
    tq = min(s, 512)
    npp = 2 if d % (2 * LANES) == 0 else 1

    mod_all = _ada_mod(c, ada_w, ada_b).reshape(depth, b, 6, d)
    cos_t, sin_t = _rope_tables(positions)
    shared = None

    for l in range(depth):
        mod = mod_all[l]
        ng = norm_g[l]
        if l < n_rw:
            i = l
            lora = max(LANES, -(-rw_w1.shape[2] // LANES) * LANES)
            r, k, v = _rw_proj(x, mod, ng, rw_mu[i], rw_rkv[i].astype(BF16), tm_small)
            lw, a, g = _rw_lora(
                x, mod, ng, rw_mu[i], rw_w0[i].reshape(1, d),
                _pad_cols(rw_w1[i], lora).astype(BF16), _pad_rows(rw_w2[i], lora).astype(BF16),
                rw_a0[i].reshape(1, d),
                _pad_cols(rw_a1[i], lora).astype(BF16), _pad_rows(rw_a2[i], lora).astype(BF16),
                rw_g1[i].astype(BF16), rw_g2[i].astype(BF16), tm_small)
            mixed = _rw_scan(r, k, v, lw, a, g, rw_kk[i].reshape(1, d), rw_ka[i].reshape(1, d),
                             rw_rk[i].reshape(1, d), rw_lnx[i, 0].reshape(1, d),
                             rw_lnx[i, 1].reshape(1, d), tb, npp)
            x = _oproj(mixed, rw_o[i].astype(BF16), x, mod, ng, tm)
        else:
            i = l - n_rw
            uq = mla_uq[i]
            q_lora = uq.shape[0]
            uqn = uq[:, :, :MLA_NOPE].reshape(q_lora, heads * MLA_NOPE)
            uqr = uq[:, :, MLA_NOPE:].reshape(q_lora, heads * MLA_ROPE)
            kdr = kv_down[:, kv_lora:]
            kdr2 = jnp.concatenate([kdr, kdr], axis=1)
            scale = float((MLA_NOPE + MLA_ROPE) ** -0.5)
            qn, qr, kn, kr, v = _mla_proj(
                x, mod, ng, cos_t, sin_t, kv_in_g.reshape(1, d),
                kv_down[:, :kv_lora].astype(BF16), kdr2.astype(BF16),
                _rot_half_cols(kdr2).astype(BF16), kv_norm.reshape(1, kv_lora),
                kv_uk.reshape(kv_lora, -1).astype(BF16), kv_uv.reshape(kv_lora, -1).astype(BF16),
                mla_dq[i].astype(BF16), mla_qnorm[i].reshape(1, q_lora), uqn.astype(BF16),
                uqr.astype(BF16), _rot_half_cols(uqr).astype(BF16), scale, tm_small)
            if shared is None:
                shared = (kn, kr, v)
            kn, kr, v = shared
            att = _attention(qn, qr, kn, kr, v, tq)
            x = _oproj(att, mla_o[i].astype(BF16), x, mod, ng, tm)
        x = _mlp(x, mod, ng, mlp_up[l].astype(BF16), mlp_down[l].astype(BF16), tm, tf)
    return x
```

```python
import functools
import math

import jax
import jax.numpy as jnp
from jax import lax
from jax.experimental import pallas as pl
from jax.experimental.pallas import tpu as pltpu

F32 = jnp.float32
BF16 = jnp.bfloat16

LANES = 128
NORM_EPS = 1e-6
RW_HEAD = 64
RW_GN_EPS = RW_HEAD * 1e-5
RW_CHUNK = 64
MLA_HEAD_V = 128
MLA_NOPE = 128
MLA_ROPE = 64
ROPE_THETA = 10000.0
VMEM_LIMIT = 56 * 1024 * 1024


def _params(sem):
    return pltpu.CompilerParams(dimension_semantics=sem, vmem_limit_bytes=VMEM_LIMIT)


def _dot(a, b):
    return jnp.dot(a.astype(BF16), b.astype(BF16), preferred_element_type=F32)


def _dot_nt(a, b):
    return lax.dot_general(a.astype(BF16), b.astype(BF16), (((1,), (1,)), ((), ())),
                           preferred_element_type=F32)


def _rms(x):
    return x * lax.rsqrt(jnp.mean(x * x, axis=-1, keepdims=True) + NORM_EPS)


def _normmod(x, g, scale, shift):
    return _rms(x) * g * (1.0 + scale) + shift


def _sigmoid(x):
    return 1.0 / (1.0 + jnp.exp(-x))


def _sigmoid_t(x):
    return 0.5 * jnp.tanh(0.5 * x) + 0.5


def _ada_kernel(c_ref, w_ref, b_ref, o_ref):
    c = c_ref[...]
    ca = c * _sigmoid(c)
    o_ref[...] = jnp.dot(ca, w_ref[...], preferred_element_type=F32,
                         precision=lax.Precision.HIGHEST) + b_ref[...]


def _ada_mod(c, ada_w, ada_b):
    depth, d, n = ada_w.shape
    b = c.shape[0]
    tn = 1536 if n % 1536 == 0 else n
    return pl.pallas_call(
        _ada_kernel,
        grid=(depth, n // tn),
        in_specs=[
            pl.BlockSpec((b, d), lambda l, j: (0, 0)),
            pl.BlockSpec((None, d, tn), lambda l, j: (l, 0, j)),
            pl.BlockSpec((None, 1, tn), lambda l, j: (l, 0, j)),
        ],
        out_specs=pl.BlockSpec((None, b, tn), lambda l, j: (l, 0, j)),
        out_shape=jax.ShapeDtypeStruct((depth, b, n), F32),
        compiler_params=_params(("arbitrary", "arbitrary")),
    )(c, ada_w, ada_b.reshape(depth, 1, n))


def _rope_kernel(pos_ref, invf_ref, cos_ref, sin_ref):
    ang = pos_ref[...] * invf_ref[...]
    cos_ref[...] = jnp.cos(ang)
    sin_ref[...] = jnp.sin(ang)


def _rope_tables(positions):
    b, s = positions.shape
    half = MLA_ROPE // 2
    inv_freq = 1.0 / (ROPE_THETA ** (jnp.arange(0, MLA_ROPE, 2, dtype=F32) / MLA_ROPE))
    invf = jnp.tile(inv_freq, LANES // half).reshape(1, LANES)
    pos = jnp.broadcast_to(positions.astype(F32)[..., None], (b, s, LANES))
    tm = min(s, 512)
    spec = pl.BlockSpec((None, tm, LANES), lambda bi, i: (bi, i, 0))
    return pl.pallas_call(
        _rope_kernel,
        grid=(b, s // tm),
        in_specs=[spec, pl.BlockSpec((1, LANES), lambda bi, i: (0, 0))],
        out_specs=[spec, spec],
        out_shape=[jax.ShapeDtypeStruct((b, s, LANES), F32)] * 2,
        compiler_params=_params(("arbitrary", "arbitrary")),
    )(pos, invf)


def _shifted(x_ref, xp_ref, mod_ref, ng_ref, first_tile):
    g = ng_ref[0:1, :]
    shift = mod_ref[0:1, :]
    scale = mod_ref[1:2, :]
    h = _normmod(x_ref[...], g, scale, shift)
    hp = _normmod(xp_ref[7:8, :], g, scale, shift)
    hp = jnp.where(first_tile, 0.0, hp)
    hs = pltpu.roll(h, 1, 0)
    row = lax.broadcasted_iota(jnp.int32, (8, h.shape[1]), 0)
    top = jnp.where(row == 0, hp, hs[0:8, :])
    hs = jnp.concatenate([top, hs[8:, :]], axis=0)
    return h, hs - h


def _rw_proj_kernel(x_ref, xp_ref, mod_ref, ng_ref, mu_ref, w_ref, r_ref, k_ref, v_ref, *, sub):
    i = pl.program_id(1)
    g = ng_ref[0:1, :]
    shift = mod_ref[0:1, :]
    scale = mod_ref[1:2, :]
    prev = jnp.where(i == 0, 0.0, _normmod(xp_ref[7:8, :], g, scale, shift))
    row = lax.broadcasted_iota(jnp.int32, (8, x_ref.shape[1]), 0)
    for sb in range(x_ref.shape[0] // sub):
        rows = slice(sb * sub, (sb + 1) * sub)
        h = _normmod(x_ref[rows, :], g, scale, shift)
        hs = pltpu.roll(h, 1, 0)
        hs = jnp.concatenate([jnp.where(row == 0, prev, hs[0:8, :]), hs[8:, :]], axis=0)
        prev = h[sub - 1:sub, :]
        xx = hs - h
        for s, o_ref in enumerate((r_ref, k_ref, v_ref)):
            xs = (h + xx * mu_ref[s:s + 1, :]).astype(BF16)
            o_ref[rows, :] = jnp.dot(xs, w_ref[s], preferred_element_type=F32).astype(o_ref.dtype)


def _rw_proj(x, mod, ng, mu, w_rkv, tm):
    b, s, d = x.shape
    row = pl.BlockSpec((None, tm, d), lambda bi, i: (bi, i, 0))
    return pl.pallas_call(
        functools.partial(_rw_proj_kernel, sub=tm // 2),
        grid=(b, s // tm),
        in_specs=[
            row,
            pl.BlockSpec((None, 8, d), lambda bi, i: (bi, jnp.maximum(i * (tm // 8) - 1, 0), 0)),
            pl.BlockSpec((None, 6, d), lambda bi, i: (bi, 0, 0)),
            pl.BlockSpec((4, d), lambda bi, i: (0, 0)),
            pl.BlockSpec((6, d), lambda bi, i: (0, 0)),
            pl.BlockSpec((3, d, d), lambda bi, i: (0, 0, 0), pipeline_mode=pl.Buffered(1)),
        ],
        out_specs=[row, row, row],
        out_shape=[jax.ShapeDtypeStruct((b, s, d), BF16)] * 3,
        compiler_params=_params(("arbitrary", "arbitrary")),
    )(x, x, mod, ng, mu, w_rkv)


def _rw_lora_kernel(x_ref, xp_ref, mod_ref, ng_ref, mu_ref, w0_ref, w1_ref, w2_ref,
                    a0_ref, a1_ref, a2_ref, g1_ref, g2_ref, lw_ref, a_ref, g_ref):
    i = pl.program_id(1)
    h, xx = _shifted(x_ref, xp_ref, mod_ref, ng_ref, i == 0)
    xw = h + xx * mu_ref[3:4, :]
    xa = h + xx * mu_ref[4:5, :]
    xg = h + xx * mu_ref[5:6, :]
    wpre = w0_ref[...] + _dot(jnp.tanh(_dot(xw, w1_ref[...])), w2_ref[...])
    lw_ref[...] = (-math.exp(-0.5)) * _sigmoid_t(wpre)
    a_ref[...] = _sigmoid_t(a0_ref[...] + _dot(_dot(xa, a1_ref[...]), a2_ref[...]))
    g_ref[...] = _dot(_sigmoid_t(_dot(xg, g1_ref[...])), g2_ref[...])


def _rw_lora(x, mod, ng, mu, w0, w1, w2, a0, a1, a2, g1, g2, tm):
    b, s, d = x.shape
    full = lambda arr: pl.BlockSpec(arr.shape, lambda bi, i: (0,) * arr.ndim)
    row = pl.BlockSpec((None, tm, d), lambda bi, i: (bi, i, 0))
    return pl.pallas_call(
        _rw_lora_kernel,
        grid=(b, s // tm),
        in_specs=[
            row,
            pl.BlockSpec((None, 8, d), lambda bi, i: (bi, jnp.maximum(i * (tm // 8) - 1, 0), 0)),
            pl.BlockSpec((None, 6, d), lambda bi, i: (bi, 0, 0)),
            full(ng), full(mu), full(w0), full(w1), full(w2),
            full(a0), full(a1), full(a2), full(g1), full(g2),
        ],
        out_specs=[row, row, row],
        out_shape=[jax.ShapeDtypeStruct((b, s, d), F32)] * 3,
        compiler_params=_params(("arbitrary", "arbitrary")),
    )(x, x, mod, ng, mu, w0, w1, w2, a0, a1, a2, g1, g2)


def _split2(x):
    hi = x.astype(BF16)
    lo = (x - hi.astype(F32)).astype(BF16)
    return hi, lo


def _split3(x):
    hi = x.astype(BF16)
    r1 = x - hi.astype(F32)
    mid = r1.astype(BF16)
    lo = (r1 - mid.astype(F32)).astype(BF16)
    return hi, mid, lo


def _stack2(x, m0, m1):
    return jnp.concatenate([x * m0, x * m1], axis=0)


def _rw_scan_kernel(r_ref, k_ref, v_ref, lw_ref, a_ref, g_ref, kkw_ref, ka_ref, rk_ref,
                    lng_ref, lnb_ref, o_ref, s_ref, q_ref, m_ref, bt_ref, yi_ref, gc_ref, y_ref,
                    *, nchunk, npp):
    C = RW_CHUNK
    C2 = 2 * C

    @pl.when(pl.program_id(2) == 0)
    def _():
        s_ref[...] = jnp.zeros_like(s_ref)

    lane = lax.broadcasted_iota(jnp.int32, (1, LANES), 1)
    m0 = (lane < RW_HEAD).astype(F32)
    m1 = 1.0 - m0
    ri = lax.broadcasted_iota(jnp.int32, (C2, C2), 0)
    ci = lax.broadcasted_iota(jnp.int32, (C2, C2), 1)
    same_head = (ri >> 6) == (ci >> 6)
    strict = (ri & (C - 1)) > (ci & (C - 1))
    incl = (ri & (C - 1)) >= (ci & (C - 1))
    eye = (ri == ci).astype(F32)
    off_masks = []
    for j in range(C.bit_length() - 1):
        off_masks.append(((ri >> (j + 1)) == (ci >> (j + 1)))
                         & (((ri >> j) & 1) == 1) & (((ci >> j) & 1) == 0))
    block_ones = same_head.astype(BF16)
    tri = (incl[:C, :C]).astype(BF16)

    def head_sum(x):
        hi, lo = _split2(x)
        return (jnp.dot(hi, block_ones, preferred_element_type=F32)
                + jnp.dot(lo, block_ones, preferred_element_type=F32))

    def stack(x):
        return _stack2(x, m0, m1).astype(BF16)

    units = []
    for pp in range(npp):
        cols = slice(pp * LANES, (pp + 1) * LANES)
        K = k_ref[:, cols]
        A = a_ref[:, cols]
        LW = lw_ref[:, cols]
        kk0 = K * kkw_ref[:, cols]
        kk = kk0 / jnp.maximum(jnp.sqrt(head_sum(kk0 * kk0)), 1e-12)
        k2 = K * (1.0 + (A - 1.0) * ka_ref[:, cols])
        kka = kk * A
        cl_cat = None
        for part in _split3(LW):
            cat = jnp.concatenate([part[c * C:(c + 1) * C, :] for c in range(nchunk)], axis=1)
            term = jnp.dot(tri, cat, preferred_element_type=F32)
            cl_cat = term if cl_cat is None else cl_cat + term
        for c in range(nchunk):
            rows = slice(c * C, (c + 1) * C)
            cl = cl_cat[:, c * LANES:(c + 1) * LANES]
            cl_end = cl[C - 1:C, :]
            gam_inv = jnp.exp(-cl)
            gam_tail = jnp.exp(cl_end - cl)
            rb_s = _stack2(r_ref[rows, cols] * jnp.exp(cl), m0, m1)
            units.append(dict(
                idx=pp * nchunk + c,
                al_s=stack(-kk[rows] * jnp.exp(cl - LW[rows])),
                rb_s=rb_s,
                be_s=stack(kka[rows] * gam_inv),
                kb_s=stack(k2[rows] * gam_inv),
                bt_s=stack(kka[rows] * gam_tail),
                kt_s=stack(k2[rows] * gam_tail),
                v_s=stack(v_ref[rows, cols]),
                gc=jnp.exp(cl_end)))

    for u in units:
        G = _dot_nt(jnp.concatenate([u["al_s"], u["rb_s"].astype(BF16)], axis=0),
                    jnp.concatenate([u["be_s"], u["kb_s"]], axis=0))
        u["Lb"] = jnp.where(strict, G[:C2, :C2], 0.0)
        u["Lk"] = jnp.where(strict, G[:C2, C2:], 0.0).astype(BF16)
        u["R"] = jnp.concatenate([jnp.where(incl, G[C2:, :C2], 0.0),
                                  jnp.where(incl, G[C2:, C2:], 0.0)], axis=1).astype(BF16)
    for u in units:
        u["lkv"] = _dot(u["Lk"], u["v_s"])
        u["T"] = eye + jnp.where(off_masks[0], u["Lb"], 0.0)
    for off in off_masks[1:]:
        for u in units:
            u["TL"] = _dot(u["T"], jnp.where(off, u["Lb"], 0.0))
        for u in units:
            u["T"] = u["T"] + _dot(u["TL"], u["T"])
    for u in units:
        u["Z"] = _dot(u["T"], jnp.concatenate([u["al_s"], u["lkv"].astype(BF16)], axis=1))
    for u in units:
        X = jnp.concatenate(
            [u["Z"], jnp.concatenate([jnp.zeros((C2, C2), F32), u["v_s"].astype(F32)], axis=1)],
            axis=0)
        QY = _dot(u["R"], X)
        MB = _dot(X.T, jnp.concatenate([u["bt_s"], u["kt_s"]], axis=0))
        i = u["idx"]
        q_ref[i] = (u["rb_s"] + QY[:, :C2]).astype(BF16)
        yi_ref[i] = QY[:, C2:]
        m_ref[i] = MB[:C2, :].astype(BF16)
        bt_ref[i] = MB[C2:, :]
        gc_ref[i] = jnp.broadcast_to(u["gc"], (8, LANES))

    S = [s_ref[pp] for pp in range(npp)]
    for c in range(nchunk):
        for pp in range(npp):
            i = pp * nchunk + c
            Sb = S[pp].astype(BF16)
            ys = _dot_nt(q_ref[i], Sb) + yi_ref[i]
            y_ref[c * C:(c + 1) * C, pp * LANES:(pp + 1) * LANES] = ys[:C, :] + ys[C:, :]
            S[pp] = (S[pp] * gc_ref[i][0:1, :]
                     + jnp.dot(Sb, m_ref[i], preferred_element_type=F32) + bt_ref[i])
    for pp in range(npp):
        s_ref[pp] = S[pp]

    inv_n = 1.0 / RW_HEAD
    for pp in range(npp):
        cols = slice(pp * LANES, (pp + 1) * LANES)
        y = y_ref[:, cols]
        mean = head_sum(y) * inv_n
        yc = y - mean
        var = head_sum(yc * yc) * inv_n
        yn = yc * lax.rsqrt(var + RW_GN_EPS) * lng_ref[:, cols] + lnb_ref[:, cols]
        k2 = k_ref[:, cols] * (1.0 + (a_ref[:, cols] - 1.0) * ka_ref[:, cols])
        bonus = head_sum(r_ref[:, cols] * k2 * rk_ref[:, cols]) * v_ref[:, cols]
        o_ref[:, cols] = ((yn + bonus) * g_ref[:, cols]).astype(o_ref.dtype)


def _rw_scan(r, k, v, lw, a, g, kkw, ka, rk, lng, lnb, tb, npp):
    b, s, d = r.shape
    nchunk = tb // RW_CHUNK
    width = npp * LANES
    C2 = 2 * RW_CHUNK
    nunit = npp * nchunk
    tok = pl.BlockSpec((None, tb, width), lambda bi, hp, t: (bi, t, hp))
    par = pl.BlockSpec((1, width), lambda bi, hp, t: (0, hp))

    return pl.pallas_call(
        functools.partial(_rw_scan_kernel, nchunk=nchunk, npp=npp),
        grid=(b, d // width, s // tb),
        in_specs=[tok, tok, tok, tok, tok, tok, par, par, par, par, par],
        out_specs=tok,
        out_shape=jax.ShapeDtypeStruct((b, s, d), BF16),
        scratch_shapes=[
            pltpu.VMEM((npp, C2, C2), F32),
            pltpu.VMEM((nunit, C2, C2), BF16),
            pltpu.VMEM((nunit, C2, C2), BF16),
            pltpu.VMEM((nunit, C2, C2), F32),
            pltpu.VMEM((nunit, C2, C2), F32),
            pltpu.VMEM((nunit, 8, LANES), F32),
            pltpu.VMEM((tb, width), F32),
        ],
        compiler_params=_params(("arbitrary", "arbitrary", "arbitrary")),
    )(r, k, v, lw, a, g, kkw, ka, rk, lng, lnb)


def _bdiag(a, b):
    zero = jnp.zeros(a.shape, a.dtype)
    return jnp.concatenate([jnp.concatenate([a, zero], axis=1),
                            jnp.concatenate([zero, b], axis=1)], axis=0)


def _rw_scan4_kernel(r_ref, k_ref, v_ref, lw_ref, a_ref, g_ref, kkw_ref, ka_ref, rk_ref,
                     lng_ref, lnb_ref, o_ref, s_ref, q_ref, m_ref, bt_ref, yi_ref, gc_ref, y_ref,
                     *, nchunk):
    C = RW_CHUNK
    C2 = 2 * C
    W = 2 * LANES

    @pl.when(pl.program_id(2) == 0)
    def _():
        s_ref[...] = jnp.zeros_like(s_ref)

    lane = lax.broadcasted_iota(jnp.int32, (1, W), 1)
    m0 = ((lane & (LANES - 1)) < RW_HEAD).astype(F32)
    m1 = 1.0 - m0
    ri = lax.broadcasted_iota(jnp.int32, (C2, W), 0)
    ci = lax.broadcasted_iota(jnp.int32, (C2, W), 1) & (C2 - 1)
    strict = (ri & (C - 1)) > (ci & (C - 1))
    incl = (ri & (C - 1)) >= (ci & (C - 1))
    eye = (ri == ci).astype(F32)
    off_masks = []
    for j in range(C.bit_length() - 1):
        off_masks.append(((ri >> (j + 1)) == (ci >> (j + 1)))
                         & (((ri >> j) & 1) == 1) & (((ci >> j) & 1) == 0))
    rw = lax.broadcasted_iota(jnp.int32, (W, W), 0)
    cw = lax.broadcasted_iota(jnp.int32, (W, W), 1)
    head_ones = ((rw >> 6) == (cw >> 6)).astype(BF16)
    tri = (incl[:C, :C]).astype(BF16)

    def head_sum(x):
        hi, lo = _split2(x)
        return (jnp.dot(hi, head_ones, preferred_element_type=F32)
                + jnp.dot(lo, head_ones, preferred_element_type=F32))

    def stack(x):
        return _stack2(x, m0, m1).astype(BF16)

    def halves(x):
        return x[:, :LANES], x[:, LANES:]

    K = k_ref[...]
    A = a_ref[...]
    LW = lw_ref[...]
    kk0 = K * kkw_ref[...]
    kk = kk0 / jnp.maximum(jnp.sqrt(head_sum(kk0 * kk0)), 1e-12)
    k2 = K * (1.0 + (A - 1.0) * ka_ref[...])
    kka = kk * A
    cl_cat = None
    for part in _split3(LW):
        cat = jnp.concatenate([part[c * C:(c + 1) * C, :] for c in range(nchunk)], axis=1)
        term = jnp.dot(tri, cat, preferred_element_type=F32)
        cl_cat = term if cl_cat is None else cl_cat + term

    units = []
    for c in range(nchunk):
        rows = slice(c * C, (c + 1) * C)
        cl = cl_cat[:, c * W:(c + 1) * W]
        cl_end = cl[C - 1:C, :]
        gam_inv = jnp.exp(-cl)
        gam_tail = jnp.exp(cl_end - cl)
        units.append(dict(
            c=c,
            al_s=stack(-kk[rows] * jnp.exp(cl - LW[rows])),
            rb_s=_stack2(r_ref[rows, :] * jnp.exp(cl), m0, m1),
            be_s=stack(kka[rows] * gam_inv),
            kb_s=stack(k2[rows] * gam_inv),
            bt_s=stack(kka[rows] * gam_tail),
            kt_s=stack(k2[rows] * gam_tail),
            v_s=stack(v_ref[rows, :]),
            gc=jnp.exp(cl_end)))

    for u in units:
        lhs = jnp.concatenate([u["al_s"], u["rb_s"].astype(BF16)], axis=0)
        rhs = jnp.concatenate([u["be_s"], u["kb_s"]], axis=0)
        G = [_dot_nt(lh, rh) for lh, rh in zip(halves(lhs), halves(rhs))]
        u["Lb"] = jnp.where(strict, jnp.concatenate([g[:C2, :C2] for g in G], axis=1), 0.0)
        u["Lk"] = jnp.where(strict, jnp.concatenate([g[:C2, C2:] for g in G], axis=1),
                            0.0).astype(BF16)
        u["R"] = [jnp.concatenate([jnp.where(incl[:, :C2], g[C2:, :C2], 0.0),
                                   jnp.where(incl[:, :C2], g[C2:, C2:], 0.0)],
                                  axis=1).astype(BF16) for g in G]
    for u in units:
        u["lkv"] = _dot(u["Lk"], _bdiag(*halves(u["v_s"])))
        u["T"] = eye + jnp.where(off_masks[0], u["Lb"], 0.0)
    for off in off_masks[1:]:
        for u in units:
            lo = jnp.where(off, u["Lb"], 0.0).astype(BF16)
            u["TL"] = _dot(u["T"], _bdiag(*halves(lo)))
        for u in units:
            u["T"] = u["T"] + _dot(u["TL"], _bdiag(*halves(u["T"].astype(BF16))))
    for u in units:
        tb16 = u["T"].astype(BF16)
        lkv16 = u["lkv"].astype(BF16)
        u["Z"] = [_dot(t, jnp.concatenate([al, lk], axis=1))
                  for t, al, lk in zip(halves(tb16), halves(u["al_s"]), halves(lkv16))]
    for u in units:
        qs, ys, ms, bs = [], [], [], []
        for p in range(2):
            cols = slice(p * LANES, (p + 1) * LANES)
            X = jnp.concatenate(
                [u["Z"][p],
                 jnp.concatenate([jnp.zeros((C2, C2), F32), u["v_s"][:, cols].astype(F32)], axis=1)],
                axis=0)
            QY = _dot(u["R"][p], X)
            MB = _dot(X.T, jnp.concatenate([u["bt_s"][:, cols], u["kt_s"][:, cols]], axis=0))
            qs.append(u["rb_s"][:, cols] + QY[:, :C2])
            ys.append(QY[:, C2:])
            ms.append(MB[:C2, :].astype(BF16))
            bs.append(MB[C2:, :])
        c = u["c"]
        q_ref[c] = jnp.concatenate(qs, axis=1).astype(BF16)
        yi_ref[c] = jnp.concatenate(ys, axis=1)
        m_ref[c] = _bdiag(*ms)
        bt_ref[c] = jnp.concatenate(bs, axis=1)
        gc_ref[c] = jnp.broadcast_to(u["gc"], (8, W))

    S = s_ref[...]
    for c in range(nchunk):
        Sb = S.astype(BF16)
        ys = _dot_nt(q_ref[c], _bdiag(*halves(Sb))) + yi_ref[c]
        y_ref[c * C:(c + 1) * C, :] = ys[:C, :] + ys[C:, :]
        S = S * gc_ref[c][0:1, :] + jnp.dot(Sb, m_ref[c], preferred_element_type=F32) + bt_ref[c]
    s_ref[...] = S

    inv_n = 1.0 / RW_HEAD
    y = y_ref[...]
    mean = head_sum(y) * inv_n
    yc = y - mean
    var = head_sum(yc * yc) * inv_n
    yn = yc * lax.rsqrt(var + RW_GN_EPS) * lng_ref[...] + lnb_ref[...]
    bonus = head_sum(r_ref[...] * k2 * rk_ref[...]) * v_ref[...]
    o_ref[...] = ((yn + bonus) * g_ref[...]).astype(o_ref.dtype)


def _rw_scan4(r, k, v, lw, a, g, kkw, ka, rk, lng, lnb, tb):
    b, s, d = r.shape
    nchunk = tb // RW_CHUNK
    W = 2 * LANES
    C2 = 2 * RW_CHUNK
    tok = pl.BlockSpec((None, tb, W), lambda bi, hp, t: (bi, t, hp))
    par = pl.BlockSpec((1, W), lambda bi, hp, t: (0, hp))
    return pl.pallas_call(
        functools.partial(_rw_scan4_kernel, nchunk=nchunk),
        grid=(b, d // W, s // tb),
        in_specs=[tok, tok, tok, tok, tok, tok, par, par, par, par, par],
        out_specs=tok,
        out_shape=jax.ShapeDtypeStruct((b, s, d), BF16),
        scratch_shapes=[
            pltpu.VMEM((C2, W), F32),
            pltpu.VMEM((nchunk, C2, W), BF16),
            pltpu.VMEM((nchunk, W, W), BF16),
            pltpu.VMEM((nchunk, C2, W), F32),
            pltpu.VMEM((nchunk, C2, W), F32),
            pltpu.VMEM((nchunk, 8, W), F32),
            pltpu.VMEM((tb, W), F32),
        ],
        compiler_params=_params(("arbitrary", "arbitrary", "arbitrary")),
    )(r, k, v, lw, a, g, kkw, ka, rk, lng, lnb)


def _oproj_kernel(a_ref, w_ref, x_ref, mod_ref, ng_ref, o_ref):
    y = jnp.dot(a_ref[...], w_ref[...], preferred_element_type=F32)
    o_ref[...] = x_ref[...] + mod_ref[2:3, :] * (_rms(y) * ng_ref[1:2, :])


def _oproj(a, w, x, mod, ng, tm):
    b, s, d = x.shape
    k = a.shape[-1]
    return pl.pallas_call(
        _oproj_kernel,
        grid=(b, s // tm),
        in_specs=[
            pl.BlockSpec((None, tm, k), lambda bi, i: (bi, i, 0)),
            pl.BlockSpec((k, d), lambda bi, i: (0, 0)),
            pl.BlockSpec((None, tm, d), lambda bi, i: (bi, i, 0)),
            pl.BlockSpec((None, 6, d), lambda bi, i: (bi, 0, 0)),
            pl.BlockSpec((4, d), lambda bi, i: (0, 0)),
        ],
        out_specs=pl.BlockSpec((None, tm, d), lambda bi, i: (bi, i, 0)),
        out_shape=jax.ShapeDtypeStruct((b, s, d), F32),
        compiler_params=_params(("arbitrary", "arbitrary")),
    )(a, w, x, mod, ng)


def _mlp_kernel(x_ref, mod_ref, ng_ref, up_ref, dn_ref, o_ref, h_ref, acc_ref, *, nsub):
    j = pl.program_id(2)
    last = pl.num_programs(2) - 1
    sub = x_ref.shape[0] // nsub

    def ffn(h):
        u = jnp.maximum(jnp.dot(h, up_ref[...], preferred_element_type=F32), 0.0)
        return jnp.dot((u * u).astype(BF16), dn_ref[...], preferred_element_type=F32)

    @pl.when(j == 0)
    def _():
        for sb in range(nsub):
            rows = slice(sb * sub, (sb + 1) * sub)
            h = _normmod(x_ref[rows, :], ng_ref[2:3, :], mod_ref[4:5, :], mod_ref[3:4, :])
            h = h.astype(BF16)
            h_ref[rows, :] = h
            acc_ref[rows, :] = ffn(h)

    @pl.when(jnp.logical_and(j > 0, j < last))
    def _():
        acc_ref[...] += ffn(h_ref[...])

    @pl.when(j == last)
    def _():
        for sb in range(nsub):
            rows = slice(sb * sub, (sb + 1) * sub)
            y = acc_ref[rows, :] + ffn(h_ref[rows, :])
            o_ref[rows, :] = x_ref[rows, :] + mod_ref[5:6, :] * (_rms(y) * ng_ref[3:4, :])


def _mlp(x, mod, ng, up, dn, tm, tf):
    b, s, d = x.shape
    f = up.shape[1]
    assert f // tf >= 2
    return pl.pallas_call(
        functools.partial(_mlp_kernel, nsub=4),
        grid=(b, s // tm, f // tf),
        in_specs=[
            pl.BlockSpec((None, tm, d), lambda bi, i, j: (bi, i, 0)),
            pl.BlockSpec((None, 6, d), lambda bi, i, j: (bi, 0, 0)),
            pl.BlockSpec((4, d), lambda bi, i, j: (0, 0)),
            pl.BlockSpec((d, tf), lambda bi, i, j: (0, j)),
            pl.BlockSpec((tf, d), lambda bi, i, j: (j, 0)),
        ],
        out_specs=pl.BlockSpec((None, tm, d), lambda bi, i, j: (bi, i, 0)),
        out_shape=jax.ShapeDtypeStruct((b, s, d), F32),
        scratch_shapes=[pltpu.VMEM((tm, d), BF16), pltpu.VMEM((tm, d), F32)],
        compiler_params=_params(("arbitrary", "arbitrary", "arbitrary")),
    )(x, mod, ng, up, dn)


def _mla_proj_kernel(x_ref, mod_ref, ng_ref, cos_ref, sin_ref, kvg_ref, kdc_ref, kdr_ref, kdrr_ref,
                     kvn_ref, uk_ref, uv_ref, dq_ref, qn_ref, uqn_ref, uqr_ref, uqrr_ref,
                     qnope_ref, qrope_ref, knope_ref, krope_ref, v_ref, *, scale):
    x = x_ref[...]
    xn = _rms(x)
    cos = cos_ref[...]
    sin = sin_ref[...]

    hs = (xn * kvg_ref[...]).astype(BF16)
    ckv = _rms(jnp.dot(hs, kdc_ref[...], preferred_element_type=F32)) * kvn_ref[...]
    ckv = ckv.astype(BF16)
    knope_ref[...] = jnp.dot(ckv, uk_ref[...], preferred_element_type=F32).astype(knope_ref.dtype)
    v_ref[...] = jnp.dot(ckv, uv_ref[...], preferred_element_type=F32).astype(v_ref.dtype)
    kr = (jnp.dot(hs, kdr_ref[...], preferred_element_type=F32) * cos
          + jnp.dot(hs, kdrr_ref[...], preferred_element_type=F32) * sin)
    krope_ref[...] = kr.astype(krope_ref.dtype)

    h = (xn * ng_ref[0:1, :] * (1.0 + mod_ref[1:2, :]) + mod_ref[0:1, :]).astype(BF16)
    cq = _rms(jnp.dot(h, dq_ref[...], preferred_element_type=F32)) * qn_ref[...]
    cq = cq.astype(BF16)
    qnope = jnp.dot(cq, uqn_ref[...], preferred_element_type=F32) * scale
    qnope_ref[...] = qnope.astype(qnope_ref.dtype)
    qr = jnp.dot(cq, uqr_ref[...], preferred_element_type=F32)
    qrr = jnp.dot(cq, uqrr_ref[...], preferred_element_type=F32)
    reps = qr.shape[1] // LANES
    cos_w = jnp.concatenate([cos] * reps, axis=1)
    sin_w = jnp.concatenate([sin] * reps, axis=1)
    qrope_ref[...] = ((qr * cos_w + qrr * sin_w) * scale).astype(qrope_ref.dtype)


def _mla_proj(x, mod, ng, cos_t, sin_t, kvg, kdc, kdr, kdrr, kvn, uk, uv, dq, qn, uqn, uqr, uqrr,
              scale, tm):
    b, s, d = x.shape
    dr = uqr.shape[1]
    full = lambda arr: pl.BlockSpec(arr.shape, lambda bi, i: (0,) * arr.ndim)
    row = lambda w: pl.BlockSpec((None, tm, w), lambda bi, i: (bi, i, 0))
    return pl.pallas_call(
        functools.partial(_mla_proj_kernel, scale=scale),
        grid=(b, s // tm),
        in_specs=[row(d), pl.BlockSpec((None, 6, d), lambda bi, i: (bi, 0, 0)), full(ng),
                  row(LANES), row(LANES), full(kvg), full(kdc), full(kdr), full(kdrr), full(kvn),
                  full(uk), full(uv), full(dq), full(qn), full(uqn), full(uqr), full(uqrr)],
        out_specs=[row(d), row(dr), row(d), row(LANES), row(d)],
        out_shape=[jax.ShapeDtypeStruct((b, s, d), BF16), jax.ShapeDtypeStruct((b, s, dr), BF16),
                   jax.ShapeDtypeStruct((b, s, d), BF16), jax.ShapeDtypeStruct((b, s, LANES), BF16),
                   jax.ShapeDtypeStruct((b, s, d), BF16)],
        compiler_params=_params(("arbitrary", "arbitrary")),
    )(x, mod, ng, cos_t, sin_t, kvg, kdc, kdr, kdrr, kvn, uk, uv, dq, qn, uqn, uqr, uqrr)


def _attn_kernel(qn_ref, qr_ref, kn_ref, kr_ref, v_ref, o_ref, *, tq):
    s = qn_ref.shape[0]
    neg = jnp.finfo(F32).min
    lane = lax.broadcasted_iota(jnp.int32, (1, LANES), 1)
    ri = lax.broadcasted_iota(jnp.int32, (tq, tq), 0)
    ci = lax.broadcasted_iota(jnp.int32, (tq, tq), 1)
    causal = ri >= ci
    kr = kr_ref[...]
    k_cat = [jnp.concatenate([kn_ref[:, hd * MLA_NOPE:(hd + 1) * MLA_NOPE], kr], axis=1)
             for hd in range(2)]

    def scores(hd, qi):
        rows = slice(qi * tq, (qi + 1) * tq)
        in_head = (lane >> 6) == hd
        q_rope = jnp.where(in_head, qr_ref[rows, :], jnp.zeros((), BF16))
        q_cat = jnp.concatenate([qn_ref[rows, hd * MLA_NOPE:(hd + 1) * MLA_NOPE], q_rope], axis=1)
        lo = qi * tq
        sd = jnp.where(causal, _dot_nt(q_cat, k_cat[hd][lo:lo + tq, :]), neg)
        sl = _dot_nt(q_cat, k_cat[hd][:lo, :]) if qi > 0 else None
        return sd, sl

    def finish(hd, qi, sd, sl):
        rows = slice(qi * tq, (qi + 1) * tq)
        cols = slice(hd * MLA_HEAD_V, (hd + 1) * MLA_HEAD_V)
        lo = qi * tq
        m = jnp.max(sd, axis=-1, keepdims=True)
        if sl is not None:
            m = jnp.maximum(m, jnp.max(sl, axis=-1, keepdims=True))
        pd = jnp.exp2(sd - m)
        den = jnp.sum(pd, axis=-1, keepdims=True)
        acc = _dot(pd, v_ref[lo:lo + tq, cols])
        if sl is not None:
            pl_ = jnp.exp2(sl - m)
            den = den + jnp.sum(pl_, axis=-1, keepdims=True)
            acc = acc + _dot(pl_, v_ref[:lo, cols])
        o_ref[rows, cols] = (acc / den).astype(o_ref.dtype)

    items = [(hd, qi) for qi in range(s // tq) for hd in range(2)]
    nxt = scores(*items[0])
    for idx, item in enumerate(items):
        cur = nxt
        if idx + 1 < len(items):
            nxt = scores(*items[idx + 1])
        finish(*item, *cur)


def _attention(qn, qr, kn, kr, v, tq):
    b, s, d = qn.shape
    npair = d // (2 * MLA_NOPE)
    wide = pl.BlockSpec((None, s, 2 * MLA_NOPE), lambda bi, hp: (bi, 0, hp))
    return pl.pallas_call(
        functools.partial(_attn_kernel, tq=tq),
        grid=(b, npair),
        in_specs=[wide, pl.BlockSpec((None, s, LANES), lambda bi, hp: (bi, 0, hp)), wide,
                  pl.BlockSpec((None, s, LANES), lambda bi, hp: (bi, 0, 0)), wide],
        out_specs=wide,
        out_shape=jax.ShapeDtypeStruct((b, s, d), BF16),
        compiler_params=_params(("arbitrary", "arbitrary")),
    )(qn, qr, kn, kr, v)


def _pad_cols(w, n):
    return jnp.pad(w, ((0, 0), (0, n - w.shape[1])))


def _pad_rows(w, n):
    return jnp.pad(w, ((0, n - w.shape[0]), (0, 0)))


def _rot_half_cols(w):
    k, n = w.shape
    half = MLA_ROPE // 2
    w3 = w.reshape(k, n // MLA_ROPE, MLA_ROPE)
    return jnp.concatenate([-w3[..., half:], w3[..., :half]], axis=-1).reshape(k, n)


def kernel(x, c, positions, ada_w, ada_b, norm_g, mlp_up, mlp_down, rw_mu, rw_rkv, rw_w0, rw_w1,
           rw_w2, rw_a0, rw_a1, rw_a2, rw_g1, rw_g2, rw_kk, rw_ka, rw_rk, rw_lnx, rw_o, mla_dq,
           mla_qnorm, mla_uq, mla_o, kv_in_g, kv_down, kv_norm, kv_uk, kv_uv):
    b, s, d = x.shape
    depth = ada_w.shape[0]
    n_rw = rw_mu.shape[0]
    kv_lora = kv_norm.shape[0]
    heads = d // MLA_HEAD_V
    assert d % (2 * MLA_NOPE) == 0 and s % RW_CHUNK == 0

    tm = min(s, 512)
    tm_small = min(s, 256)
    tf = min(mlp_up.shape[2], 1024)
    tb = min(s, 512)
    tq = min(s, 512)

    mod_all = _ada_mod(c, ada_w, ada_b).reshape(depth, b, 6, d)
    cos_t, sin_t = _rope_tables(positions)
    shared = None

    for l in range(depth):
        mod = mod_all[l]
        ng = norm_g[l]
        if l < n_rw:
            i = l
            lora = max(LANES, -(-rw_w1.shape[2] // LANES) * LANES)
            r, k, v = _rw_proj(x, mod, ng, rw_mu[i], rw_rkv[i].astype(BF16), tm_small)
            lw, a, g = _rw_lora(
                x, mod, ng, rw_mu[i], rw_w0[i].reshape(1, d),
                _pad_cols(rw_w1[i], lora).astype(BF16), _pad_rows(rw_w2[i], lora).astype(BF16),
                rw_a0[i].reshape(1, d),
                _pad_cols(rw_a1[i], lora).astype(BF16), _pad_rows(rw_a2[i], lora).astype(BF16),
                rw_g1[i].astype(BF16), rw_g2[i].astype(BF16), tm_small)
            mixed = _rw_scan(r, k, v, lw, a, g, rw_kk[i].reshape(1, d), rw_ka[i].reshape(1, d),
                             rw_rk[i].reshape(1, d), rw_lnx[i, 0].reshape(1, d),
                             rw_lnx[i, 1].reshape(1, d), tb, 2)
            x = _oproj(mixed, rw_o[i].astype(BF16), x, mod, ng, tm)
        else:
            i = l - n_rw
            uq = mla_uq[i]
            q_lora = uq.shape[0]
            uqn = uq[:, :, :MLA_NOPE].reshape(q_lora, heads * MLA_NOPE)
            uqr = uq[:, :, MLA_NOPE:].reshape(q_lora, heads * MLA_ROPE)
            kdr = kv_down[:, kv_lora:]
            kdr2 = jnp.concatenate([kdr, kdr], axis=1)
            scale = float((MLA_NOPE + MLA_ROPE) ** -0.5) * math.log2(math.e)
            qn, qr, kn, kr, v = _mla_proj(
                x, mod, ng, cos_t, sin_t, kv_in_g.reshape(1, d),
                kv_down[:, :kv_lora].astype(BF16), kdr2.astype(BF16),
                _rot_half_cols(kdr2).astype(BF16), kv_norm.reshape(1, kv_lora),
                kv_uk.reshape(kv_lora, -1).astype(BF16), kv_uv.reshape(kv_lora, -1).astype(BF16),
                mla_dq[i].astype(BF16), mla_qnorm[i].reshape(1, q_lora), uqn.astype(BF16),
                uqr.astype(BF16), _rot_half_cols(uqr).astype(BF16), scale, tm_small)
            if shared is None:
                shared = (kn, kr, v)
            kn, kr, v = shared
            att = _attention(qn, qr, kn, kr, v, tq)
            x = _oproj(att, mla_o[i].astype(BF16), x, mod, ng, tm)
        x = _mlp(x, mod, ng, mlp_up[l].astype(BF16), mlp_down[l].astype(BF16), tm, tf)
    return x
```

```python
import functools
import math

import jax
import jax.numpy as jnp
from jax import lax
from jax.experimental import pallas as pl
from jax.experimental.pallas import tpu as pltpu

F32 = jnp.float32
BF16 = jnp.bfloat16

LANES = 128
NORM_EPS = 1e-6
RW_HEAD = 64
RW_GN_EPS = RW_HEAD * 1e-5
RW_CHUNK = 64
MLA_HEAD_V = 128
MLA_NOPE = 128
MLA_ROPE = 64
ROPE_THETA = 10000.0
VMEM_LIMIT = 56 * 1024 * 1024


def _params(sem):
    return pltpu.CompilerParams(dimension_semantics=sem, vmem_limit_bytes=VMEM_LIMIT)


def _dot(a, b):
    return jnp.dot(a.astype(BF16), b.astype(BF16), preferred_element_type=F32)


def _dot_nt(a, b):
    return lax.dot_general(a.astype(BF16), b.astype(BF16), (((1,), (1,)), ((), ())),
                           preferred_element_type=F32)


def _rms(x):
    return x * lax.rsqrt(jnp.mean(x * x, axis=-1, keepdims=True) + NORM_EPS)


def _normmod(x, g, scale, shift):
    return _rms(x) * g * (1.0 + scale) + shift


def _sigmoid(x):
    return 1.0 / (1.0 + jnp.exp(-x))


def _sigmoid_t(x):
    return 0.5 * jnp.tanh(0.5 * x) + 0.5


CAST_BLOCK_BYTES = 8 * 1024 * 1024


def _cast_kernel(x_ref, o_ref):
    o_ref[...] = x_ref[...].astype(o_ref.dtype)


def _to_bf16(w):
    shape = w.shape
    w2 = w.reshape(-1, shape[-1])
    r, c = w2.shape
    tr = min(r, max(16, CAST_BLOCK_BYTES // (4 * c)))
    if r % tr or tr % 16:
        return w.astype(BF16)
    out = pl.pallas_call(
        _cast_kernel,
        grid=(r // tr,),
        in_specs=[pl.BlockSpec((tr, c), lambda i: (i, 0))],
        out_specs=pl.BlockSpec((tr, c), lambda i: (i, 0)),
        out_shape=jax.ShapeDtypeStruct((r, c), BF16),
        compiler_params=_params(("arbitrary",)),
    )(w2)
    return out.reshape(shape)


def _ada_kernel(c_ref, w_ref, b_ref, o_ref):
    c = c_ref[...]
    c_hi, c_lo = _split2(c * _sigmoid(c))
    w_hi, w_lo = _split2(w_ref[...])
    o_ref[...] = (jnp.dot(c_hi, w_hi, preferred_element_type=F32)
                  + jnp.dot(c_lo, w_hi, preferred_element_type=F32)
                  + jnp.dot(c_hi, w_lo, preferred_element_type=F32)) + b_ref[...]


def _ada_mod(c, ada_w, ada_b):
    depth, d, n = ada_w.shape
    b = c.shape[0]
    tn = 1536 if n % 1536 == 0 else n
    return pl.pallas_call(
        _ada_kernel,
        grid=(depth, n // tn),
        in_specs=[
            pl.BlockSpec((b, d), lambda l, j: (0, 0)),
            pl.BlockSpec((None, d, tn), lambda l, j: (l, 0, j)),
            pl.BlockSpec((None, 1, tn), lambda l, j: (l, 0, j)),
        ],
        out_specs=pl.BlockSpec((None, b, tn), lambda l, j: (l, 0, j)),
        out_shape=jax.ShapeDtypeStruct((depth, b, n), F32),
        compiler_params=_params(("arbitrary", "arbitrary")),
    )(c, ada_w, ada_b.reshape(depth, 1, n))


def _rope_kernel(pos_ref, invf_ref, cos_ref, sin_ref):
    ang = pos_ref[...] * invf_ref[...]
    cos_ref[...] = jnp.cos(ang)
    sin_ref[...] = jnp.sin(ang)


def _rope_tables(positions):
    b, s = positions.shape
    half = MLA_ROPE // 2
    inv_freq = 1.0 / (ROPE_THETA ** (jnp.arange(0, MLA_ROPE, 2, dtype=F32) / MLA_ROPE))
    invf = jnp.tile(inv_freq, LANES // half).reshape(1, LANES)
    pos = jnp.broadcast_to(positions.astype(F32)[..., None], (b, s, LANES))
    tm = min(s, 512)
    spec = pl.BlockSpec((None, tm, LANES), lambda bi, i: (bi, i, 0))
    return pl.pallas_call(
        _rope_kernel,
        grid=(b, s // tm),
        in_specs=[spec, pl.BlockSpec((1, LANES), lambda bi, i: (0, 0))],
        out_specs=[spec, spec],
        out_shape=[jax.ShapeDtypeStruct((b, s, LANES), F32)] * 2,
        compiler_params=_params(("arbitrary", "arbitrary")),
    )(pos, invf)


def _shifted(x_ref, xp_ref, mod_ref, ng_ref, first_tile):
    g = ng_ref[0:1, :]
    shift = mod_ref[0:1, :]
    scale = mod_ref[1:2, :]
    h = _normmod(x_ref[...], g, scale, shift)
    hp = _normmod(xp_ref[7:8, :], g, scale, shift)
    hp = jnp.where(first_tile, 0.0, hp)
    hs = pltpu.roll(h, 1, 0)
    row = lax.broadcasted_iota(jnp.int32, (8, h.shape[1]), 0)
    top = jnp.where(row == 0, hp, hs[0:8, :])
    hs = jnp.concatenate([top, hs[8:, :]], axis=0)
    return h, hs - h


def _rw_proj_kernel(x_ref, xp_ref, mod_ref, ng_ref, mu_ref, w_ref, r_ref, k_ref, v_ref, *, sub):
    i = pl.program_id(1)
    g = ng_ref[0:1, :]
    shift = mod_ref[0:1, :]
    scale = mod_ref[1:2, :]
    prev = jnp.where(i == 0, 0.0, _normmod(xp_ref[7:8, :], g, scale, shift))
    row = lax.broadcasted_iota(jnp.int32, (8, x_ref.shape[1]), 0)
    for sb in range(x_ref.shape[0] // sub):
        rows = slice(sb * sub, (sb + 1) * sub)
        h = _normmod(x_ref[rows, :], g, scale, shift)
        hs = pltpu.roll(h, 1, 0)
        hs = jnp.concatenate([jnp.where(row == 0, prev, hs[0:8, :]), hs[8:, :]], axis=0)
        prev = h[sub - 1:sub, :]
        xx = hs - h
        for s, o_ref in enumerate((r_ref, k_ref, v_ref)):
            xs = (h + xx * mu_ref[s:s + 1, :]).astype(BF16)
            o_ref[rows, :] = jnp.dot(xs, w_ref[s], preferred_element_type=F32).astype(o_ref.dtype)


def _rw_proj(x, mod, ng, mu, w_rkv, tm):
    b, s, d = x.shape
    row = pl.BlockSpec((None, tm, d), lambda bi, i: (bi, i, 0))
    return pl.pallas_call(
        functools.partial(_rw_proj_kernel, sub=tm // 2),
        grid=(b, s // tm),
        in_specs=[
            row,
            pl.BlockSpec((None, 8, d), lambda bi, i: (bi, jnp.maximum(i * (tm // 8) - 1, 0), 0)),
            pl.BlockSpec((None, 6, d), lambda bi, i: (bi, 0, 0)),
            pl.BlockSpec((4, d), lambda bi, i: (0, 0)),
            pl.BlockSpec((6, d), lambda bi, i: (0, 0)),
            pl.BlockSpec((3, d, d), lambda bi, i: (0, 0, 0), pipeline_mode=pl.Buffered(1)),
        ],
        out_specs=[row, row, row],
        out_shape=[jax.ShapeDtypeStruct((b, s, d), BF16)] * 3,
        compiler_params=_params(("arbitrary", "arbitrary")),
    )(x, x, mod, ng, mu, w_rkv)


def _rw_lora_kernel(x_ref, xp_ref, mod_ref, ng_ref, mu_ref, w0_ref, w1_ref, w2_ref,
                    a0_ref, a1_ref, a2_ref, g1_ref, g2_ref, lw_ref, a_ref, g_ref):
    i = pl.program_id(1)
    h, xx = _shifted(x_ref, xp_ref, mod_ref, ng_ref, i == 0)
    xw = h + xx * mu_ref[3:4, :]
    xa = h + xx * mu_ref[4:5, :]
    xg = h + xx * mu_ref[5:6, :]
    wpre = w0_ref[...] + _dot(jnp.tanh(_dot(xw, w1_ref[...])), w2_ref[...])
    lw_ref[...] = (-math.exp(-0.5)) * _sigmoid_t(wpre)
    a_ref[...] = _sigmoid_t(a0_ref[...] + _dot(_dot(xa, a1_ref[...]), a2_ref[...]))
    g_ref[...] = _dot(_sigmoid_t(_dot(xg, g1_ref[...])), g2_ref[...])


def _rw_lora(x, mod, ng, mu, w0, w1, w2, a0, a1, a2, g1, g2, tm):
    b, s, d = x.shape
    full = lambda arr: pl.BlockSpec(arr.shape, lambda bi, i: (0,) * arr.ndim)
    row = pl.BlockSpec((None, tm, d), lambda bi, i: (bi, i, 0))
    return pl.pallas_call(
        _rw_lora_kernel,
        grid=(b, s // tm),
        in_specs=[
            row,
            pl.BlockSpec((None, 8, d), lambda bi, i: (bi, jnp.maximum(i * (tm // 8) - 1, 0), 0)),
            pl.BlockSpec((None, 6, d), lambda bi, i: (bi, 0, 0)),
            full(ng), full(mu), full(w0), full(w1), full(w2),
            full(a0), full(a1), full(a2), full(g1), full(g2),
        ],
        out_specs=[row, row, row],
        out_shape=[jax.ShapeDtypeStruct((b, s, d), F32)] * 3,
        compiler_params=_params(("arbitrary", "arbitrary")),
    )(x, x, mod, ng, mu, w0, w1, w2, a0, a1, a2, g1, g2)


def _split2(x):
    hi = x.astype(BF16)
    lo = (x - hi.astype(F32)).astype(BF16)
    return hi, lo


def _split3(x):
    hi = x.astype(BF16)
    r1 = x - hi.astype(F32)
    mid = r1.astype(BF16)
    lo = (r1 - mid.astype(F32)).astype(BF16)
    return hi, mid, lo


def _stack2(x, m0, m1):
    return jnp.concatenate([x * m0, x * m1], axis=0)


def _rw_scan_kernel(r_ref, k_ref, v_ref, lw_ref, a_ref, g_ref, kkw_ref, ka_ref, rk_ref,
                    lng_ref, lnb_ref, o_ref, s_ref, q_ref, m_ref, bt_ref, yi_ref, gc_ref, y_ref,
                    *, nchunk, npp):
    C = RW_CHUNK
    C2 = 2 * C

    @pl.when(pl.program_id(2) == 0)
    def _():
        s_ref[...] = jnp.zeros_like(s_ref)

    lane = lax.broadcasted_iota(jnp.int32, (1, LANES), 1)
    m0 = (lane < RW_HEAD).astype(F32)
    m1 = 1.0 - m0
    ri = lax.broadcasted_iota(jnp.int32, (C2, C2), 0)
    ci = lax.broadcasted_iota(jnp.int32, (C2, C2), 1)
    same_head = (ri >> 6) == (ci >> 6)
    strict = (ri & (C - 1)) > (ci & (C - 1))
    incl = (ri & (C - 1)) >= (ci & (C - 1))
    eye = (ri == ci).astype(F32)
    off_masks = []
    for j in range(C.bit_length() - 1):
        off_masks.append(((ri >> (j + 1)) == (ci >> (j + 1)))
                         & (((ri >> j) & 1) == 1) & (((ci >> j) & 1) == 0))
    block_ones = same_head.astype(BF16)
    tri = (incl[:C, :C]).astype(BF16)

    def head_sum(x):
        hi, lo = _split2(x)
        return (jnp.dot(hi, block_ones, preferred_element_type=F32)
                + jnp.dot(lo, block_ones, preferred_element_type=F32))

    def stack(x):
        return _stack2(x, m0, m1).astype(BF16)

    units = []
    for pp in range(npp):
        cols = slice(pp * LANES, (pp + 1) * LANES)
        K = k_ref[:, cols]
        A = a_ref[:, cols]
        LW = lw_ref[:, cols]
        kk0 = K * kkw_ref[:, cols]
        kk = kk0 / jnp.maximum(jnp.sqrt(head_sum(kk0 * kk0)), 1e-12)
        k2 = K * (1.0 + (A - 1.0) * ka_ref[:, cols])
        kka = kk * A
        cl_cat = None
        for part in _split3(LW):
            cat = jnp.concatenate([part[c * C:(c + 1) * C, :] for c in range(nchunk)], axis=1)
            term = jnp.dot(tri, cat, preferred_element_type=F32)
            cl_cat = term if cl_cat is None else cl_cat + term
        for c in range(nchunk):
            rows = slice(c * C, (c + 1) * C)
            cl = cl_cat[:, c * LANES:(c + 1) * LANES]
            cl_end = cl[C - 1:C, :]
            gam_inv = jnp.exp(-cl)
            gam_tail = jnp.exp(cl_end - cl)
            rb_s = _stack2(r_ref[rows, cols] * jnp.exp(cl), m0, m1)
            units.append(dict(
                idx=pp * nchunk + c,
                al_s=stack(-kk[rows] * jnp.exp(cl - LW[rows])),
                rb_s=rb_s,
                be_s=stack(kka[rows] * gam_inv),
                kb_s=stack(k2[rows] * gam_inv),
                bt_s=stack(kka[rows] * gam_tail),
                kt_s=stack(k2[rows] * gam_tail),
                v_s=stack(v_ref[rows, cols]),
                gc=jnp.exp(cl_end)))

    for u in units:
        G = _dot_nt(jnp.concatenate([u["al_s"], u["rb_s"].astype(BF16)], axis=0),
                    jnp.concatenate([u["be_s"], u["kb_s"]], axis=0))
        u["Lb"] = jnp.where(strict, G[:C2, :C2], 0.0)
        u["Lk"] = jnp.where(strict, G[:C2, C2:], 0.0).astype(BF16)
        u["R"] = jnp.concatenate([jnp.where(incl, G[C2:, :C2], 0.0),
                                  jnp.where(incl, G[C2:, C2:], 0.0)], axis=1).astype(BF16)
    for u in units:
        u["lkv"] = _dot(u["Lk"], u["v_s"])
        u["T"] = eye + jnp.where(off_masks[0], u["Lb"], 0.0)
    for off in off_masks[1:]:
        for u in units:
            u["TL"] = _dot(u["T"], jnp.where(off, u["Lb"], 0.0))
        for u in units:
            u["T"] = u["T"] + _dot(u["TL"], u["T"])
    for u in units:
        u["Z"] = _dot(u["T"], jnp.concatenate([u["al_s"], u["lkv"].astype(BF16)], axis=1))
    for u in units:
        X = jnp.concatenate(
            [u["Z"], jnp.concatenate([jnp.zeros((C2, C2), F32), u["v_s"].astype(F32)], axis=1)],
            axis=0)
        QY = _dot(u["R"], X)
        MB = _dot(X.T, jnp.concatenate([u["bt_s"], u["kt_s"]], axis=0))
        i = u["idx"]
        q_ref[i] = (u["rb_s"] + QY[:, :C2]).astype(BF16)
        yi_ref[i] = QY[:, C2:]
        m_ref[i] = MB[:C2, :].astype(BF16)
        bt_ref[i] = MB[C2:, :]
        gc_ref[i] = jnp.broadcast_to(u["gc"], (8, LANES))

    S = [s_ref[pp] for pp in range(npp)]
    for c in range(nchunk):
        for pp in range(npp):
            i = pp * nchunk + c
            Sb = S[pp].astype(BF16)
            ys = _dot_nt(q_ref[i], Sb) + yi_ref[i]
            y_ref[c * C:(c + 1) * C, pp * LANES:(pp + 1) * LANES] = ys[:C, :] + ys[C:, :]
            S[pp] = (S[pp] * gc_ref[i][0:1, :]
                     + jnp.dot(Sb, m_ref[i], preferred_element_type=F32) + bt_ref[i])
    for pp in range(npp):
        s_ref[pp] = S[pp]

    inv_n = 1.0 / RW_HEAD
    for pp in range(npp):
        cols = slice(pp * LANES, (pp + 1) * LANES)
        y = y_ref[:, cols]
        mean = head_sum(y) * inv_n
        yc = y - mean
        var = head_sum(yc * yc) * inv_n
        yn = yc * lax.rsqrt(var + RW_GN_EPS) * lng_ref[:, cols] + lnb_ref[:, cols]
        k2 = k_ref[:, cols] * (1.0 + (a_ref[:, cols] - 1.0) * ka_ref[:, cols])
        bonus = head_sum(r_ref[:, cols] * k2 * rk_ref[:, cols]) * v_ref[:, cols]
        o_ref[:, cols] = ((yn + bonus) * g_ref[:, cols]).astype(o_ref.dtype)


def _rw_scan(r, k, v, lw, a, g, kkw, ka, rk, lng, lnb, tb, npp):
    b, s, d = r.shape
    nchunk = tb // RW_CHUNK
    width = npp * LANES
    C2 = 2 * RW_CHUNK
    nunit = npp * nchunk
    tok = pl.BlockSpec((None, tb, width), lambda bi, hp, t: (bi, t, hp))
    par = pl.BlockSpec((1, width), lambda bi, hp, t: (0, hp))

    return pl.pallas_call(
        functools.partial(_rw_scan_kernel, nchunk=nchunk, npp=npp),
        grid=(b, d // width, s // tb),
        in_specs=[tok, tok, tok, tok, tok, tok, par, par, par, par, par],
        out_specs=tok,
        out_shape=jax.ShapeDtypeStruct((b, s, d), BF16),
        scratch_shapes=[
            pltpu.VMEM((npp, C2, C2), F32),
            pltpu.VMEM((nunit, C2, C2), BF16),
            pltpu.VMEM((nunit, C2, C2), BF16),
            pltpu.VMEM((nunit, C2, C2), F32),
            pltpu.VMEM((nunit, C2, C2), F32),
            pltpu.VMEM((nunit, 8, LANES), F32),
            pltpu.VMEM((tb, width), F32),
        ],
        compiler_params=_params(("arbitrary", "arbitrary", "arbitrary")),
    )(r, k, v, lw, a, g, kkw, ka, rk, lng, lnb)


def _bdiag(a, b):
    zero = jnp.zeros(a.shape, a.dtype)
    return jnp.concatenate([jnp.concatenate([a, zero], axis=1),
                            jnp.concatenate([zero, b], axis=1)], axis=0)


def _rw_scan4_kernel(r_ref, k_ref, v_ref, lw_ref, a_ref, g_ref, kkw_ref, ka_ref, rk_ref,
                     lng_ref, lnb_ref, o_ref, s_ref, q_ref, m_ref, bt_ref, yi_ref, gc_ref, y_ref,
                     *, nchunk):
    C = RW_CHUNK
    C2 = 2 * C
    W = 2 * LANES

    @pl.when(pl.program_id(2) == 0)
    def _():
        s_ref[...] = jnp.zeros_like(s_ref)

    lane = lax.broadcasted_iota(jnp.int32, (1, W), 1)
    m0 = ((lane & (LANES - 1)) < RW_HEAD).astype(F32)
    m1 = 1.0 - m0
    ri = lax.broadcasted_iota(jnp.int32, (C2, W), 0)
    ci = lax.broadcasted_iota(jnp.int32, (C2, W), 1) & (C2 - 1)
    strict = (ri & (C - 1)) > (ci & (C - 1))
    incl = (ri & (C - 1)) >= (ci & (C - 1))
    eye = (ri == ci).astype(F32)
    off_masks = []
    for j in range(C.bit_length() - 1):
        off_masks.append(((ri >> (j + 1)) == (ci >> (j + 1)))
                         & (((ri >> j) & 1) == 1) & (((ci >> j) & 1) == 0))
    rw = lax.broadcasted_iota(jnp.int32, (W, W), 0)
    cw = lax.broadcasted_iota(jnp.int32, (W, W), 1)
    head_ones = ((rw >> 6) == (cw >> 6)).astype(BF16)
    tri = (incl[:C, :C]).astype(BF16)

    def head_sum(x):
        hi, lo = _split2(x)
        return (jnp.dot(hi, head_ones, preferred_element_type=F32)
                + jnp.dot(lo, head_ones, preferred_element_type=F32))

    def stack(x):
        return _stack2(x, m0, m1).astype(BF16)

    def halves(x):
        return x[:, :LANES], x[:, LANES:]

    K = k_ref[...]
    A = a_ref[...]
    LW = lw_ref[...]
    kk0 = K * kkw_ref[...]
    kk = kk0 / jnp.maximum(jnp.sqrt(head_sum(kk0 * kk0)), 1e-12)
    k2 = K * (1.0 + (A - 1.0) * ka_ref[...])
    kka = kk * A
    cl_cat = None
    for part in _split3(LW):
        cat = jnp.concatenate([part[c * C:(c + 1) * C, :] for c in range(nchunk)], axis=1)
        term = jnp.dot(tri, cat, preferred_element_type=F32)
        cl_cat = term if cl_cat is None else cl_cat + term

    units = []
    for c in range(nchunk):
        rows = slice(c * C, (c + 1) * C)
        cl = cl_cat[:, c * W:(c + 1) * W]
        cl_end = cl[C - 1:C, :]
        gam_inv = jnp.exp(-cl)
        gam_tail = jnp.exp(cl_end - cl)
        units.append(dict(
            c=c,
            al_s=stack(-kk[rows] * jnp.exp(cl - LW[rows])),
            rb_s=_stack2(r_ref[rows, :] * jnp.exp(cl), m0, m1),
            be_s=stack(kka[rows] * gam_inv),
            kb_s=stack(k2[rows] * gam_inv),
            bt_s=stack(kka[rows] * gam_tail),
            kt_s=stack(k2[rows] * gam_tail),
            v_s=stack(v_ref[rows, :]),
            gc=jnp.exp(cl_end)))

    for u in units:
        lhs = jnp.concatenate([u["al_s"], u["rb_s"].astype(BF16)], axis=0)
        rhs = jnp.concatenate([u["be_s"], u["kb_s"]], axis=0)
        G = [_dot_nt(lh, rh) for lh, rh in zip(halves(lhs), halves(rhs))]
        u["Lb"] = jnp.where(strict, jnp.concatenate([g[:C2, :C2] for g in G], axis=1), 0.0)
        u["Lk"] = jnp.where(strict, jnp.concatenate([g[:C2, C2:] for g in G], axis=1),
                            0.0).astype(BF16)
        u["R"] = [jnp.concatenate([jnp.where(incl[:, :C2], g[C2:, :C2], 0.0),
                                   jnp.where(incl[:, :C2], g[C2:, C2:], 0.0)],
                                  axis=1).astype(BF16) for g in G]
    for u in units:
        u["lkv"] = _dot(u["Lk"], _bdiag(*halves(u["v_s"])))
        u["T"] = eye + jnp.where(off_masks[0], u["Lb"], 0.0)
    for off in off_masks[1:]:
        for u in units:
            lo = jnp.where(off, u["Lb"], 0.0).astype(BF16)
            u["TL"] = _dot(u["T"], _bdiag(*halves(lo)))
        for u in units:
            u["T"] = u["T"] + _dot(u["TL"], _bdiag(*halves(u["T"].astype(BF16))))
    for u in units:
        tb16 = u["T"].astype(BF16)
        lkv16 = u["lkv"].astype(BF16)
        u["Z"] = [_dot(t, jnp.concatenate([al, lk], axis=1))
                  for t, al, lk in zip(halves(tb16), halves(u["al_s"]), halves(lkv16))]
    for u in units:
        qs, ys, ms, bs = [], [], [], []
        for p in range(2):
            cols = slice(p * LANES, (p + 1) * LANES)
            X = jnp.concatenate(
                [u["Z"][p],
                 jnp.concatenate([jnp.zeros((C2, C2), F32), u["v_s"][:, cols].astype(F32)], axis=1)],
                axis=0)
            QY = _dot(u["R"][p], X)
            MB = _dot(X.T, jnp.concatenate([u["bt_s"][:, cols], u["kt_s"][:, cols]], axis=0))
            qs.append(u["rb_s"][:, cols] + QY[:, :C2])
            ys.append(QY[:, C2:])
            ms.append(MB[:C2, :].astype(BF16))
            bs.append(MB[C2:, :])
        c = u["c"]
        q_ref[c] = jnp.concatenate(qs, axis=1).astype(BF16)
        yi_ref[c] = jnp.concatenate(ys, axis=1)
        m_ref[c] = _bdiag(*ms)
        bt_ref[c] = jnp.concatenate(bs, axis=1)
        gc_ref[c] = jnp.broadcast_to(u["gc"], (8, W))

    S = s_ref[...]
    for c in range(nchunk):
        Sb = S.astype(BF16)
        ys = _dot_nt(q_ref[c], _bdiag(*halves(Sb))) + yi_ref[c]
        y_ref[c * C:(c + 1) * C, :] = ys[:C, :] + ys[C:, :]
        S = S * gc_ref[c][0:1, :] + jnp.dot(Sb, m_ref[c], preferred_element_type=F32) + bt_ref[c]
    s_ref[...] = S

    inv_n = 1.0 / RW_HEAD
    y = y_ref[...]
    mean = head_sum(y) * inv_n
    yc = y - mean
    var = head_sum(yc * yc) * inv_n
    yn = yc * lax.rsqrt(var + RW_GN_EPS) * lng_ref[...] + lnb_ref[...]
    bonus = head_sum(r_ref[...] * k2 * rk_ref[...]) * v_ref[...]
    o_ref[...] = ((yn + bonus) * g_ref[...]).astype(o_ref.dtype)


def _rw_scan4(r, k, v, lw, a, g, kkw, ka, rk, lng, lnb, tb):
    b, s, d = r.shape
    nchunk = tb // RW_CHUNK
    W = 2 * LANES
    C2 = 2 * RW_CHUNK
    tok = pl.BlockSpec((None, tb, W), lambda bi, hp, t: (bi, t, hp))
    par = pl.BlockSpec((1, W), lambda bi, hp, t: (0, hp))
    return pl.pallas_call(
        functools.partial(_rw_scan4_kernel, nchunk=nchunk),
        grid=(b, d // W, s // tb),
        in_specs=[tok, tok, tok, tok, tok, tok, par, par, par, par, par],
        out_specs=tok,
        out_shape=jax.ShapeDtypeStruct((b, s, d), BF16),
        scratch_shapes=[
            pltpu.VMEM((C2, W), F32),
            pltpu.VMEM((nchunk, C2, W), BF16),
            pltpu.VMEM((nchunk, W, W), BF16),
            pltpu.VMEM((nchunk, C2, W), F32),
            pltpu.VMEM((nchunk, C2, W), F32),
            pltpu.VMEM((nchunk, 8, W), F32),
            pltpu.VMEM((tb, W), F32),
        ],
        compiler_params=_params(("arbitrary", "arbitrary", "arbitrary")),
    )(r, k, v, lw, a, g, kkw, ka, rk, lng, lnb)


def _oproj_kernel(a_ref, w_ref, x_ref, mod_ref, ng_ref, o_ref):
    nsub = 4
    sub = a_ref.shape[0] // nsub
    for sb in range(nsub):
        rows = slice(sb * sub, (sb + 1) * sub)
        y = jnp.dot(a_ref[rows, :], w_ref[...], preferred_element_type=F32)
        o_ref[rows, :] = x_ref[rows, :] + mod_ref[2:3, :] * (_rms(y) * ng_ref[1:2, :])


def _oproj(a, w, x, mod, ng, tm):
    b, s, d = x.shape
    k = a.shape[-1]
    return pl.pallas_call(
        _oproj_kernel,
        grid=(b, s // tm),
        in_specs=[
            pl.BlockSpec((None, tm, k), lambda bi, i: (bi, i, 0)),
            pl.BlockSpec((k, d), lambda bi, i: (0, 0)),
            pl.BlockSpec((None, tm, d), lambda bi, i: (bi, i, 0)),
            pl.BlockSpec((None, 6, d), lambda bi, i: (bi, 0, 0)),
            pl.BlockSpec((4, d), lambda bi, i: (0, 0)),
        ],
        out_specs=pl.BlockSpec((None, tm, d), lambda bi, i: (bi, i, 0)),
        out_shape=jax.ShapeDtypeStruct((b, s, d), F32),
        compiler_params=_params(("arbitrary", "arbitrary")),
    )(a, w, x, mod, ng)


def _mlp_kernel(x_ref, mod_ref, ng_ref, up_ref, dn_ref, o_ref, h_ref, acc_ref, *, nsub):
    j = pl.program_id(2)
    last = pl.num_programs(2) - 1
    sub = x_ref.shape[0] // nsub

    def ffn(h):
        u = jnp.maximum(jnp.dot(h, up_ref[...], preferred_element_type=F32), 0.0)
        return jnp.dot((u * u).astype(BF16), dn_ref[...], preferred_element_type=F32)

    @pl.when(j == 0)
    def _():
        for sb in range(nsub):
            rows = slice(sb * sub, (sb + 1) * sub)
            h = _normmod(x_ref[rows, :], ng_ref[2:3, :], mod_ref[4:5, :], mod_ref[3:4, :])
            h = h.astype(BF16)
            h_ref[rows, :] = h
            acc_ref[rows, :] = ffn(h)

    @pl.when(jnp.logical_and(j > 0, j < last))
    def _():
        for sb in range(nsub):
            rows = slice(sb * sub, (sb + 1) * sub)
            acc_ref[rows, :] += ffn(h_ref[rows, :])

    @pl.when(j == last)
    def _():
        for sb in range(nsub):
            rows = slice(sb * sub, (sb + 1) * sub)
            y = acc_ref[rows, :] + ffn(h_ref[rows, :])
            o_ref[rows, :] = x_ref[rows, :] + mod_ref[5:6, :] * (_rms(y) * ng_ref[3:4, :])


def _mlp(x, mod, ng, up_all, dn_all, layer, tm, tf):
    b, s, d = x.shape
    f = up_all.shape[2]
    assert f // tf >= 2
    return pl.pallas_call(
        functools.partial(_mlp_kernel, nsub=4),
        grid=(b, s // tm, f // tf),
        in_specs=[
            pl.BlockSpec((None, tm, d), lambda bi, i, j: (bi, i, 0)),
            pl.BlockSpec((None, 6, d), lambda bi, i, j: (bi, 0, 0)),
            pl.BlockSpec((4, d), lambda bi, i, j: (0, 0)),
            pl.BlockSpec((None, d, tf), lambda bi, i, j: (layer, 0, j)),
            pl.BlockSpec((None, tf, d), lambda bi, i, j: (layer, j, 0)),
        ],
        out_specs=pl.BlockSpec((None, tm, d), lambda bi, i, j: (bi, i, 0)),
        out_shape=jax.ShapeDtypeStruct((b, s, d), F32),
        scratch_shapes=[pltpu.VMEM((tm, d), BF16), pltpu.VMEM((tm, d), F32)],
        compiler_params=_params(("arbitrary", "arbitrary", "arbitrary")),
    )(x, mod, ng, up_all, dn_all)


def _mla_proj_kernel(x_ref, mod_ref, ng_ref, cos_ref, sin_ref, kvg_ref, kdc_ref, kdr_ref, kdrr_ref,
                     kvn_ref, uk_ref, uv_ref, dq_ref, qn_ref, uqn_ref, uqr_ref, uqrr_ref,
                     qnope_ref, qrope_ref, knope_ref, krope_ref, v_ref, *, scale):
    x = x_ref[...]
    xn = _rms(x)
    cos = cos_ref[...]
    sin = sin_ref[...]

    hs = (xn * kvg_ref[...]).astype(BF16)
    ckv = _rms(jnp.dot(hs, kdc_ref[...], preferred_element_type=F32)) * kvn_ref[...]
    ckv = ckv.astype(BF16)
    knope_ref[...] = jnp.dot(ckv, uk_ref[...], preferred_element_type=F32).astype(knope_ref.dtype)
    v_ref[...] = jnp.dot(ckv, uv_ref[...], preferred_element_type=F32).astype(v_ref.dtype)
    kr = (jnp.dot(hs, kdr_ref[...], preferred_element_type=F32) * cos
          + jnp.dot(hs, kdrr_ref[...], preferred_element_type=F32) * sin)
    krope_ref[...] = kr.astype(krope_ref.dtype)

    h = (xn * ng_ref[0:1, :] * (1.0 + mod_ref[1:2, :]) + mod_ref[0:1, :]).astype(BF16)
    cq = _rms(jnp.dot(h, dq_ref[...], preferred_element_type=F32)) * qn_ref[...]
    cq = cq.astype(BF16)
    qnope = jnp.dot(cq, uqn_ref[...], preferred_element_type=F32) * scale
    qnope_ref[...] = qnope.astype(qnope_ref.dtype)
    qr = jnp.dot(cq, uqr_ref[...], preferred_element_type=F32)
    qrr = jnp.dot(cq, uqrr_ref[...], preferred_element_type=F32)
    reps = qr.shape[1] // LANES
    cos_w = jnp.concatenate([cos] * reps, axis=1)
    sin_w = jnp.concatenate([sin] * reps, axis=1)
    qrope_ref[...] = ((qr * cos_w + qrr * sin_w) * scale).astype(qrope_ref.dtype)


def _mla_proj(x, mod, ng, cos_t, sin_t, kvg, kdc, kdr, kdrr, kvn, uk, uv, dq, qn, uqn, uqr, uqrr,
              scale, tm):
    b, s, d = x.shape
    dr = uqr.shape[1]
    full = lambda arr: pl.BlockSpec(arr.shape, lambda bi, i: (0,) * arr.ndim)
    row = lambda w: pl.BlockSpec((None, tm, w), lambda bi, i: (bi, i, 0))
    return pl.pallas_call(
        functools.partial(_mla_proj_kernel, scale=scale),
        grid=(b, s // tm),
        in_specs=[row(d), pl.BlockSpec((None, 6, d), lambda bi, i: (bi, 0, 0)), full(ng),
                  row(LANES), row(LANES), full(kvg), full(kdc), full(kdr), full(kdrr), full(kvn),
                  full(uk), full(uv), full(dq), full(qn), full(uqn), full(uqr), full(uqrr)],
        out_specs=[row(d), row(dr), row(d), row(LANES), row(d)],
        out_shape=[jax.ShapeDtypeStruct((b, s, d), BF16), jax.ShapeDtypeStruct((b, s, dr), BF16),
                   jax.ShapeDtypeStruct((b, s, d), BF16), jax.ShapeDtypeStruct((b, s, LANES), BF16),
                   jax.ShapeDtypeStruct((b, s, d), BF16)],
        compiler_params=_params(("arbitrary", "arbitrary")),
    )(x, mod, ng, cos_t, sin_t, kvg, kdc, kdr, kdrr, kvn, uk, uv, dq, qn, uqn, uqr, uqrr)


def _attn_kernel(qn_ref, qr_ref, kn_ref, kr_ref, v_ref, o_ref, *, tq):
    s = qn_ref.shape[0]
    neg = jnp.finfo(F32).min
    lane = lax.broadcasted_iota(jnp.int32, (1, LANES), 1)
    ri = lax.broadcasted_iota(jnp.int32, (tq, tq), 0)
    ci = lax.broadcasted_iota(jnp.int32, (tq, tq), 1)
    causal = ri >= ci
    kr = kr_ref[...]
    k_cat = [jnp.concatenate([kn_ref[:, hd * MLA_NOPE:(hd + 1) * MLA_NOPE], kr], axis=1)
             for hd in range(2)]

    def scores(hd, qi):
        rows = slice(qi * tq, (qi + 1) * tq)
        in_head = (lane >> 6) == hd
        q_rope = jnp.where(in_head, qr_ref[rows, :], jnp.zeros((), BF16))
        q_cat = jnp.concatenate([qn_ref[rows, hd * MLA_NOPE:(hd + 1) * MLA_NOPE], q_rope], axis=1)
        lo = qi * tq
        sd = jnp.where(causal, _dot_nt(q_cat, k_cat[hd][lo:lo + tq, :]), neg)
        sl = _dot_nt(q_cat, k_cat[hd][:lo, :]) if qi > 0 else None
        return sd, sl

    def finish(hd, qi, sd, sl):
        rows = slice(qi * tq, (qi + 1) * tq)
        cols = slice(hd * MLA_HEAD_V, (hd + 1) * MLA_HEAD_V)
        lo = qi * tq
        m = jnp.max(sd, axis=-1, keepdims=True)
        if sl is not None:
            m = jnp.maximum(m, jnp.max(sl, axis=-1, keepdims=True))
        pd = jnp.exp2(sd - m)
        den = jnp.sum(pd, axis=-1, keepdims=True)
        acc = _dot(pd, v_ref[lo:lo + tq, cols])
        if sl is not None:
            pl_ = jnp.exp2(sl - m)
            den = den + jnp.sum(pl_, axis=-1, keepdims=True)
            acc = acc + _dot(pl_, v_ref[:lo, cols])
        o_ref[rows, cols] = (acc / den).astype(o_ref.dtype)

    items = [(hd, qi) for qi in range(s // tq) for hd in range(2)]
    nxt = scores(*items[0])
    for idx, item in enumerate(items):
        cur = nxt
        if idx + 1 < len(items):
            nxt = scores(*items[idx + 1])
        finish(*item, *cur)


def _attention(qn, qr, kn, kr, v, tq):
    b, s, d = qn.shape
    npair = d // (2 * MLA_NOPE)
    wide = pl.BlockSpec((None, s, 2 * MLA_NOPE), lambda bi, hp: (bi, 0, hp))
    return pl.pallas_call(
        functools.partial(_attn_kernel, tq=tq),
        grid=(b, npair),
        in_specs=[wide, pl.BlockSpec((None, s, LANES), lambda bi, hp: (bi, 0, hp)), wide,
                  pl.BlockSpec((None, s, LANES), lambda bi, hp: (bi, 0, 0)), wide],
        out_specs=wide,
        out_shape=jax.ShapeDtypeStruct((b, s, d), BF16),
        compiler_params=_params(("arbitrary", "arbitrary")),
    )(qn, qr, kn, kr, v)


def _pad_cols(w, n):
    return jnp.pad(w, ((0, 0), (0, n - w.shape[1])))


def _pad_rows(w, n):
    return jnp.pad(w, ((0, n - w.shape[0]), (0, 0)))


def _rot_half_cols(w):
    k, n = w.shape
    half = MLA_ROPE // 2
    w3 = w.reshape(k, n // MLA_ROPE, MLA_ROPE)
    return jnp.concatenate([-w3[..., half:], w3[..., :half]], axis=-1).reshape(k, n)


def kernel(x, c, positions, ada_w, ada_b, norm_g, mlp_up, mlp_down, rw_mu, rw_rkv, rw_w0, rw_w1,
           rw_w2, rw_a0, rw_a1, rw_a2, rw_g1, rw_g2, rw_kk, rw_ka, rw_rk, rw_lnx, rw_o, mla_dq,
           mla_qnorm, mla_uq, mla_o, kv_in_g, kv_down, kv_norm, kv_uk, kv_uv):
    b, s, d = x.shape
    depth = ada_w.shape[0]
    n_rw = rw_mu.shape[0]
    kv_lora = kv_norm.shape[0]
    heads = d // MLA_HEAD_V
    assert d % (2 * MLA_NOPE) == 0 and s % RW_CHUNK == 0

    tm = min(s, 512)
    tm_small = min(s, 256)
    tf = min(mlp_up.shape[2], 1024)
    tb = min(s, 512)
    tq = min(s, 512)

    up_all = _to_bf16(mlp_up)
    dn_all = _to_bf16(mlp_down)
    rkv_all = _to_bf16(rw_rkv)
    rwo_all = _to_bf16(rw_o)
    mlao_all = _to_bf16(mla_o)
    mod_all = _ada_mod(c, ada_w, ada_b).reshape(depth, b, 6, d)
    cos_t, sin_t = _rope_tables(positions)
    shared = None

    for l in range(depth):
        mod = mod_all[l]
        ng = norm_g[l]
        if l < n_rw:
            i = l
            lora = max(LANES, -(-rw_w1.shape[2] // LANES) * LANES)
            r, k, v = _rw_proj(x, mod, ng, rw_mu[i], rkv_all[i], tm_small)
            lw, a, g = _rw_lora(
                x, mod, ng, rw_mu[i], rw_w0[i].reshape(1, d),
                _pad_cols(rw_w1[i], lora).astype(BF16), _pad_rows(rw_w2[i], lora).astype(BF16),
                rw_a0[i].reshape(1, d),
                _pad_cols(rw_a1[i], lora).astype(BF16), _pad_rows(rw_a2[i], lora).astype(BF16),
                rw_g1[i].astype(BF16), rw_g2[i].astype(BF16), tm_small)
            mixed = _rw_scan(r, k, v, lw, a, g, rw_kk[i].reshape(1, d), rw_ka[i].reshape(1, d),
                             rw_rk[i].reshape(1, d), rw_lnx[i, 0].reshape(1, d),
                             rw_lnx[i, 1].reshape(1, d), tb, 4)
            x = _oproj(mixed, rwo_all[i], x, mod, ng, tm)
        else:
            i = l - n_rw
            uq = mla_uq[i]
            q_lora = uq.shape[0]
            uqn = uq[:, :, :MLA_NOPE].reshape(q_lora, heads * MLA_NOPE)
            uqr = uq[:, :, MLA_NOPE:].reshape(q_lora, heads * MLA_ROPE)
            kdr = kv_down[:, kv_lora:]
            kdr2 = jnp.concatenate([kdr, kdr], axis=1)
            scale = float((MLA_NOPE + MLA_ROPE) ** -0.5) * math.log2(math.e)
            qn, qr, kn, kr, v = _mla_proj(
                x, mod, ng, cos_t, sin_t, kv_in_g.reshape(1, d),
                kv_down[:, :kv_lora].astype(BF16), kdr2.astype(BF16),
                _rot_half_cols(kdr2).astype(BF16), kv_norm.reshape(1, kv_lora),
                kv_uk.reshape(kv_lora, -1).astype(BF16), kv_uv.reshape(kv_lora, -1).astype(BF16),
                mla_dq[i].astype(BF16), mla_qnorm[i].reshape(1, q_lora), uqn.astype(BF16),
                uqr.astype(BF16), _rot_half_cols(uqr).astype(BF16), scale, tm_small)
            if shared is None:
                shared = (kn, kr, v)
            kn, kr, v = shared
            att = _attention(qn, qr, kn, kr, v, tq)
            x = _oproj(att, mlao_all[i], x, mod, ng, tm)
        x = _mlp(x, mod, ng, up_all, dn_all, l, tm, tf)
    return x
```

```python
import functools
import math

import jax
import jax.numpy as jnp
from jax import lax
from jax.experimental import pallas as pl
from jax.experimental.pallas import tpu as pltpu

F32 = jnp.float32
BF16 = jnp.bfloat16

LANES = 128
NORM_EPS = 1e-6
RW_HEAD = 64
RW_GN_EPS = RW_HEAD * 1e-5
RW_CHUNK = 64
MLA_HEAD_V = 128
MLA_NOPE = 128
MLA_ROPE = 64
ROPE_THETA = 10000.0
VMEM_LIMIT = 56 * 1024 * 1024


def _params(sem):
    return pltpu.CompilerParams(dimension_semantics=sem, vmem_limit_bytes=VMEM_LIMIT)


def _dot(a, b):
    return jnp.dot(a.astype(BF16), b.astype(BF16), preferred_element_type=F32)


def _dot_nt(a, b):
    return lax.dot_general(a.astype(BF16), b.astype(BF16), (((1,), (1,)), ((), ())),
                           preferred_element_type=F32)


def _rms(x):
    return x * lax.rsqrt(jnp.mean(x * x, axis=-1, keepdims=True) + NORM_EPS)


def _normmod(x, g, scale, shift):
    return _rms(x) * g * (1.0 + scale) + shift


def _sigmoid(x):
    return 1.0 / (1.0 + jnp.exp(-x))


def _sigmoid_t(x):
    return 0.5 * jnp.tanh(0.5 * x) + 0.5


def _split2(x):
    hi = x.astype(BF16)
    lo = (x - hi.astype(F32)).astype(BF16)
    return hi, lo


def _split3(x):
    hi = x.astype(BF16)
    r1 = x - hi.astype(F32)
    mid = r1.astype(BF16)
    lo = (r1 - mid.astype(F32)).astype(BF16)
    return hi, mid, lo


CAST_BLOCK_BYTES = 8 * 1024 * 1024


def _cast_kernel(x_ref, o_ref):
    o_ref[...] = x_ref[...].astype(o_ref.dtype)


def _to_bf16(w):
    shape = w.shape
    w2 = w.reshape(-1, shape[-1])
    r, c = w2.shape
    tr = min(r, max(16, CAST_BLOCK_BYTES // (4 * c)))
    if r % tr or tr % 16:
        return w.astype(BF16)
    out = pl.pallas_call(
        _cast_kernel,
        grid=(r // tr,),
        in_specs=[pl.BlockSpec((tr, c), lambda i: (i, 0))],
        out_specs=pl.BlockSpec((tr, c), lambda i: (i, 0)),
        out_shape=jax.ShapeDtypeStruct((r, c), BF16),
        compiler_params=_params(("arbitrary",)),
    )(w2)
    return out.reshape(shape)


def _ada_kernel(c_ref, w_ref, b_ref, o_ref):
    c = c_ref[...]
    c_hi, c_lo = _split2(c * _sigmoid(c))
    w_hi, w_lo = _split2(w_ref[...])
    o_ref[...] = (jnp.dot(c_hi, w_hi, preferred_element_type=F32)
                  + jnp.dot(c_lo, w_hi, preferred_element_type=F32)
                  + jnp.dot(c_hi, w_lo, preferred_element_type=F32)) + b_ref[...]


def _ada_mod(c, ada_w, ada_b):
    depth, d, n = ada_w.shape
    b = c.shape[0]
    tn = 1536 if n % 1536 == 0 else n
    return pl.pallas_call(
        _ada_kernel,
        grid=(depth, n // tn),
        in_specs=[
            pl.BlockSpec((b, d), lambda l, j: (0, 0)),
            pl.BlockSpec((None, d, tn), lambda l, j: (l, 0, j)),
            pl.BlockSpec((None, 1, tn), lambda l, j: (l, 0, j)),
        ],
        out_specs=pl.BlockSpec((None, b, tn), lambda l, j: (l, 0, j)),
        out_shape=jax.ShapeDtypeStruct((depth, b, n), F32),
        compiler_params=_params(("arbitrary", "arbitrary")),
    )(c, ada_w, ada_b.reshape(depth, 1, n))


def _rope_kernel(pos_ref, invf_ref, cos_ref, sin_ref):
    ang = pos_ref[...] * invf_ref[...]
    cos_ref[...] = jnp.cos(ang)
    sin_ref[...] = jnp.sin(ang)


def _rope_tables(positions):
    b, s = positions.shape
    half = MLA_ROPE // 2
    inv_freq = 1.0 / (ROPE_THETA ** (jnp.arange(0, MLA_ROPE, 2, dtype=F32) / MLA_ROPE))
    invf = jnp.tile(inv_freq, LANES // half).reshape(1, LANES)
    pos = jnp.broadcast_to(positions.astype(F32)[..., None], (b, s, LANES))
    tm = min(s, 512)
    spec = pl.BlockSpec((None, tm, LANES), lambda bi, i: (bi, i, 0))
    return pl.pallas_call(
        _rope_kernel,
        grid=(b, s // tm),
        in_specs=[spec, pl.BlockSpec((1, LANES), lambda bi, i: (0, 0))],
        out_specs=[spec, spec],
        out_shape=[jax.ShapeDtypeStruct((b, s, LANES), F32)] * 2,
        compiler_params=_params(("arbitrary", "arbitrary")),
    )(pos, invf)


def _shifted(x_ref, xp_ref, mod_ref, ng_ref, first_tile):
    g = ng_ref[0:1, :]
    shift = mod_ref[0:1, :]
    scale = mod_ref[1:2, :]
    h = _normmod(x_ref[...], g, scale, shift)
    hp = _normmod(xp_ref[7:8, :], g, scale, shift)
    hp = jnp.where(first_tile, 0.0, hp)
    hs = pltpu.roll(h, 1, 0)
    row = lax.broadcasted_iota(jnp.int32, (8, h.shape[1]), 0)
    top = jnp.where(row == 0, hp, hs[0:8, :])
    hs = jnp.concatenate([top, hs[8:, :]], axis=0)
    return h, hs - h


def _rw_proj_kernel(x_ref, xp_ref, mod_ref, ng_ref, mu_ref, w_ref, r_ref, k_ref, v_ref, *, sub):
    i = pl.program_id(1)
    g = ng_ref[0:1, :]
    shift = mod_ref[0:1, :]
    scale = mod_ref[1:2, :]
    prev = jnp.where(i == 0, 0.0, _normmod(xp_ref[7:8, :], g, scale, shift))
    row = lax.broadcasted_iota(jnp.int32, (8, x_ref.shape[1]), 0)
    for sb in range(x_ref.shape[0] // sub):
        rows = slice(sb * sub, (sb + 1) * sub)
        h = _normmod(x_ref[rows, :], g, scale, shift)
        hs = pltpu.roll(h, 1, 0)
        hs = jnp.concatenate([jnp.where(row == 0, prev, hs[0:8, :]), hs[8:, :]], axis=0)
        prev = h[sub - 1:sub, :]
        xx = hs - h
        for s, o_ref in enumerate((r_ref, k_ref, v_ref)):
            xs = (h + xx * mu_ref[s:s + 1, :]).astype(BF16)
            o_ref[rows, :] = jnp.dot(xs, w_ref[s], preferred_element_type=F32).astype(o_ref.dtype)


def _rw_proj(x, mod, ng, mu, w_rkv, tm):
    b, s, d = x.shape
    row = pl.BlockSpec((None, tm, d), lambda bi, i: (bi, i, 0))
    return pl.pallas_call(
        functools.partial(_rw_proj_kernel, sub=tm // 2),
        grid=(b, s // tm),
        in_specs=[
            row,
            pl.BlockSpec((None, 8, d), lambda bi, i: (bi, jnp.maximum(i * (tm // 8) - 1, 0), 0)),
            pl.BlockSpec((None, 6, d), lambda bi, i: (bi, 0, 0)),
            pl.BlockSpec((4, d), lambda bi, i: (0, 0)),
            pl.BlockSpec((6, d), lambda bi, i: (0, 0)),
            pl.BlockSpec((3, d, d), lambda bi, i: (0, 0, 0), pipeline_mode=pl.Buffered(1)),
        ],
        out_specs=[row, row, row],
        out_shape=[jax.ShapeDtypeStruct((b, s, d), BF16)] * 3,
        compiler_params=_params(("arbitrary", "arbitrary")),
    )(x, x, mod, ng, mu, w_rkv)


def _rw_lora_kernel(x_ref, xp_ref, mod_ref, ng_ref, mu_ref, w0_ref, w1_ref, w2_ref,
                    a0_ref, a1_ref, a2_ref, g1_ref, g2_ref, lw_ref, a_ref, g_ref):
    i = pl.program_id(1)
    h, xx = _shifted(x_ref, xp_ref, mod_ref, ng_ref, i == 0)
    xw = h + xx * mu_ref[3:4, :]
    xa = h + xx * mu_ref[4:5, :]
    xg = h + xx * mu_ref[5:6, :]
    wpre = w0_ref[...] + _dot(jnp.tanh(_dot(xw, w1_ref[...])), w2_ref[...])
    lw_ref[...] = (-math.exp(-0.5)) * _sigmoid_t(wpre)
    a_ref[...] = _sigmoid_t(a0_ref[...] + _dot(_dot(xa, a1_ref[...]), a2_ref[...]))
    g_ref[...] = _dot(_sigmoid_t(_dot(xg, g1_ref[...])), g2_ref[...])


def _rw_lora(x, mod, ng, mu, w0, w1, w2, a0, a1, a2, g1, g2, tm):
    b, s, d = x.shape
    full = lambda arr: pl.BlockSpec(arr.shape, lambda bi, i: (0,) * arr.ndim)
    row = pl.BlockSpec((None, tm, d), lambda bi, i: (bi, i, 0))
    return pl.pallas_call(
        _rw_lora_kernel,
        grid=(b, s // tm),
        in_specs=[
            row,
            pl.BlockSpec((None, 8, d), lambda bi, i: (bi, jnp.maximum(i * (tm // 8) - 1, 0), 0)),
            pl.BlockSpec((None, 6, d), lambda bi, i: (bi, 0, 0)),
            full(ng), full(mu), full(w0), full(w1), full(w2),
            full(a0), full(a1), full(a2), full(g1), full(g2),
        ],
        out_specs=[row, row, row],
        out_shape=[jax.ShapeDtypeStruct((b, s, d), F32)] * 3,
        compiler_params=_params(("arbitrary", "arbitrary")),
    )(x, x, mod, ng, mu, w0, w1, w2, a0, a1, a2, g1, g2)


def _stack2(x, m0, m1):
    return jnp.concatenate([x * m0, x * m1], axis=0)


def _rw_scan_kernel(r_ref, k_ref, v_ref, lw_ref, a_ref, g_ref, kkw_ref, ka_ref, rk_ref,
                    lng_ref, lnb_ref, o_ref, s_ref, q_ref, m_ref, bt_ref, yi_ref, gc_ref,
                    bonus_ref, gate_ref, y_ref, *, nchunk, npp, nblk):
    C = RW_CHUNK
    C2 = 2 * C
    nunit = npp * nchunk
    k_step = pl.program_id(0)
    slot_x = k_step % 2
    slot_y = 1 - slot_x
    base_x = slot_x * nunit
    base_y = slot_y * nunit
    first_block = (k_step + (nblk - 1)) % nblk == 0

    @pl.when(k_step == 0)
    def _():
        for ref in (s_ref, q_ref, m_ref, bt_ref, yi_ref, gc_ref, bonus_ref, gate_ref):
            ref[...] = jnp.zeros_like(ref)

    lane = lax.broadcasted_iota(jnp.int32, (1, LANES), 1)
    m0 = (lane < RW_HEAD).astype(F32)
    m1 = 1.0 - m0
    ri = lax.broadcasted_iota(jnp.int32, (C2, C2), 0)
    ci = lax.broadcasted_iota(jnp.int32, (C2, C2), 1)
    same_head = (ri >> 6) == (ci >> 6)
    strict = (ri & (C - 1)) > (ci & (C - 1))
    incl = (ri & (C - 1)) >= (ci & (C - 1))
    eye = (ri == ci).astype(F32)
    off_masks = []
    for j in range(C.bit_length() - 1):
        off_masks.append(((ri >> (j + 1)) == (ci >> (j + 1)))
                         & (((ri >> j) & 1) == 1) & (((ci >> j) & 1) == 0))
    block_ones = same_head.astype(BF16)
    tri = (incl[:C, :C]).astype(BF16)

    def head_sum(x):
        hi, lo = _split2(x)
        return (jnp.dot(hi, block_ones, preferred_element_type=F32)
                + jnp.dot(lo, block_ones, preferred_element_type=F32))

    def stack(x):
        return _stack2(x, m0, m1).astype(BF16)

    units = []

    def prep_inputs(pp):
        cols = slice(pp * LANES, (pp + 1) * LANES)
        K = k_ref[:, cols]
        A = a_ref[:, cols]
        LW = lw_ref[:, cols]
        kk0 = K * kkw_ref[:, cols]
        kk = kk0 / jnp.maximum(jnp.sqrt(head_sum(kk0 * kk0)), 1e-12)
        k2 = K * (1.0 + (A - 1.0) * ka_ref[:, cols])
        kka = kk * A
        bonus_ref[slot_x, :, cols] = (head_sum(r_ref[:, cols] * k2 * rk_ref[:, cols])
                                      * v_ref[:, cols])
        gate_ref[slot_x, :, cols] = g_ref[:, cols]
        cl_cat = None
        for part in _split3(LW):
            cat = jnp.concatenate([part[c * C:(c + 1) * C, :] for c in range(nchunk)], axis=1)
            term = jnp.dot(tri, cat, preferred_element_type=F32)
            cl_cat = term if cl_cat is None else cl_cat + term
        for c in range(nchunk):
            rows = slice(c * C, (c + 1) * C)
            cl = cl_cat[:, c * LANES:(c + 1) * LANES]
            cl_end = cl[C - 1:C, :]
            gam_inv = jnp.exp(-cl)
            gam_tail = jnp.exp(cl_end - cl)
            units.append(dict(
                idx=pp * nchunk + c,
                al_s=stack(-kk[rows] * jnp.exp(cl - LW[rows])),
                rb_s=_stack2(r_ref[rows, cols] * jnp.exp(cl), m0, m1),
                be_s=stack(kka[rows] * gam_inv),
                kb_s=stack(k2[rows] * gam_inv),
                bt_s=stack(kka[rows] * gam_tail),
                kt_s=stack(k2[rows] * gam_tail),
                v_s=stack(v_ref[rows, cols]),
                gc=jnp.exp(cl_end)))

    def stage_gram():
        for u in units:
            G = _dot_nt(jnp.concatenate([u["al_s"], u["rb_s"].astype(BF16)], axis=0),
                        jnp.concatenate([u["be_s"], u["kb_s"]], axis=0))
            u["Lb"] = jnp.where(strict, G[:C2, :C2], 0.0)
            u["Lk"] = jnp.where(strict, G[:C2, C2:], 0.0).astype(BF16)
            u["R"] = jnp.concatenate([jnp.where(incl, G[C2:, :C2], 0.0),
                                      jnp.where(incl, G[C2:, C2:], 0.0)], axis=1).astype(BF16)

    def stage_init():
        for u in units:
            u["lkv"] = _dot(u["Lk"], u["v_s"])
            u["T"] = eye + jnp.where(off_masks[0], u["Lb"], 0.0)

    def stage_left(off):
        for u in units:
            u["TL"] = _dot(u["T"], jnp.where(off, u["Lb"], 0.0))

    def stage_right():
        for u in units:
            u["T"] = u["T"] + _dot(u["TL"], u["T"])

    def stage_solve():
        for u in units:
            u["Z"] = _dot(u["T"], jnp.concatenate([u["al_s"], u["lkv"].astype(BF16)], axis=1))

    def stage_store():
        for u in units:
            X = jnp.concatenate(
                [u["Z"],
                 jnp.concatenate([jnp.zeros((C2, C2), F32), u["v_s"].astype(F32)], axis=1)],
                axis=0)
            QY = _dot(u["R"], X)
            MB = _dot(X.T, jnp.concatenate([u["bt_s"], u["kt_s"]], axis=0))
            i = base_x + u["idx"]
            q_ref[i] = (u["rb_s"] + QY[:, :C2]).astype(BF16)
            yi_ref[i] = QY[:, C2:]
            m_ref[i] = MB[:C2, :].astype(BF16)
            bt_ref[i] = MB[C2:, :]
            gc_ref[i] = jnp.broadcast_to(u["gc"], (8, LANES))

    prepare = [functools.partial(prep_inputs, pp) for pp in range(npp)]
    prepare += [stage_gram, stage_init]
    for off in off_masks[1:]:
        prepare += [functools.partial(stage_left, off), stage_right]
    prepare += [stage_solve, stage_store]

    S = [jnp.where(first_block, 0.0, s_ref[pp]) for pp in range(npp)]

    def state_step(c):
        for pp in range(npp):
            i = base_y + pp * nchunk + c
            Sb = S[pp].astype(BF16)
            ys = _dot_nt(q_ref[i], Sb) + yi_ref[i]
            y_ref[c * C:(c + 1) * C, pp * LANES:(pp + 1) * LANES] = ys[:C, :] + ys[C:, :]
            S[pp] = (S[pp] * gc_ref[i][0:1, :]
                     + jnp.dot(Sb, m_ref[i], preferred_element_type=F32) + bt_ref[i])

    def finish_out():
        inv_n = 1.0 / RW_HEAD
        for pp in range(npp):
            s_ref[pp] = S[pp]
            cols = slice(pp * LANES, (pp + 1) * LANES)
            y = y_ref[:, cols]
            mean = head_sum(y) * inv_n
            yc = y - mean
            var = head_sum(yc * yc) * inv_n
            yn = yc * lax.rsqrt(var + RW_GN_EPS) * lng_ref[:, cols] + lnb_ref[:, cols]
            o_ref[:, cols] = ((yn + bonus_ref[slot_y, :, cols])
                              * gate_ref[slot_y, :, cols]).astype(o_ref.dtype)

    finish = [functools.partial(state_step, c) for c in range(nchunk)] + [finish_out]

    done = 0
    for si, stage in enumerate(prepare):
        want = ((si + 1) * len(finish)) // len(prepare)
        while done < want:
            finish[done]()
            done += 1
        stage()


def _rw_scan(r, k, v, lw, a, g, kkw, ka, rk, lng, lnb, tb, npp):
    b, s, d = r.shape
    nchunk = tb // RW_CHUNK
    width = npp * LANES
    C2 = 2 * RW_CHUNK
    nunit = npp * nchunk
    nblk = s // tb
    ngrp = d // width
    total = b * ngrp * nblk

    def split(kk):
        return kk // (ngrp * nblk), kk % nblk, (kk // nblk) % ngrp

    def tok_in(kq):
        return split(jnp.minimum(kq, total - 1))

    def tok_out(kq):
        return split(jnp.maximum(kq - 1, 0))

    tin = pl.BlockSpec((None, tb, width), tok_in)
    pin = pl.BlockSpec((1, width), lambda kq: (0, tok_in(kq)[2]))
    pout = pl.BlockSpec((1, width), lambda kq: (0, tok_out(kq)[2]))
    return pl.pallas_call(
        functools.partial(_rw_scan_kernel, nchunk=nchunk, npp=npp, nblk=nblk),
        grid=(total + 1,),
        in_specs=[tin, tin, tin, tin, tin, tin, pin, pin, pin, pout, pout],
        out_specs=pl.BlockSpec((None, tb, width), tok_out),
        out_shape=jax.ShapeDtypeStruct((b, s, d), BF16),
        scratch_shapes=[
            pltpu.VMEM((npp, C2, C2), F32),
            pltpu.VMEM((2 * nunit, C2, C2), BF16),
            pltpu.VMEM((2 * nunit, C2, C2), BF16),
            pltpu.VMEM((2 * nunit, C2, C2), F32),
            pltpu.VMEM((2 * nunit, C2, C2), F32),
            pltpu.VMEM((2 * nunit, 8, LANES), F32),
            pltpu.VMEM((2, tb, width), F32),
            pltpu.VMEM((2, tb, width), F32),
            pltpu.VMEM((tb, width), F32),
        ],
        compiler_params=_params(("arbitrary",)),
    )(r, k, v, lw, a, g, kkw, ka, rk, lng, lnb)


def _oproj_kernel(a_ref, w_ref, x_ref, mod_ref, ng_ref, o_ref):
    nsub = 2
    sub = a_ref.shape[0] // nsub
    for sb in range(nsub):
        rows = slice(sb * sub, (sb + 1) * sub)
        y = jnp.dot(a_ref[rows, :], w_ref[...], preferred_element_type=F32)
        o_ref[rows, :] = x_ref[rows, :] + mod_ref[2:3, :] * (_rms(y) * ng_ref[1:2, :])


def _oproj(a, w, x, mod, ng, tm):
    b, s, d = x.shape
    k = a.shape[-1]
    return pl.pallas_call(
        _oproj_kernel,
        grid=(b, s // tm),
        in_specs=[
            pl.BlockSpec((None, tm, k), lambda bi, i: (bi, i, 0)),
            pl.BlockSpec((k, d), lambda bi, i: (0, 0)),
            pl.BlockSpec((None, tm, d), lambda bi, i: (bi, i, 0)),
            pl.BlockSpec((None, 6, d), lambda bi, i: (bi, 0, 0)),
            pl.BlockSpec((4, d), lambda bi, i: (0, 0)),
        ],
        out_specs=pl.BlockSpec((None, tm, d), lambda bi, i: (bi, i, 0)),
        out_shape=jax.ShapeDtypeStruct((b, s, d), F32),
        compiler_params=_params(("arbitrary", "arbitrary")),
    )(a, w, x, mod, ng)


def _mlp_kernel(x_ref, mod_ref, ng_ref, up_ref, dn_ref, o_ref, h_ref, acc_ref, *, nsub):
    j = pl.program_id(2)
    last = pl.num_programs(2) - 1
    sub = x_ref.shape[0] // nsub

    def ffn(h):
        u = jnp.maximum(jnp.dot(h, up_ref[...], preferred_element_type=F32), 0.0)
        return jnp.dot((u * u).astype(BF16), dn_ref[...], preferred_element_type=F32)

    @pl.when(j == 0)
    def _():
        for sb in range(nsub):
            rows = slice(sb * sub, (sb + 1) * sub)
            h = _normmod(x_ref[rows, :], ng_ref[2:3, :], mod_ref[4:5, :], mod_ref[3:4, :])
            h = h.astype(BF16)
            h_ref[rows, :] = h
            acc_ref[rows, :] = ffn(h)

    @pl.when(jnp.logical_and(j > 0, j < last))
    def _():
        acc_ref[...] += ffn(h_ref[...])

    @pl.when(j == last)
    def _():
        for sb in range(nsub):
            rows = slice(sb * sub, (sb + 1) * sub)
            y = acc_ref[rows, :] + ffn(h_ref[rows, :])
            o_ref[rows, :] = x_ref[rows, :] + mod_ref[5:6, :] * (_rms(y) * ng_ref[3:4, :])


def _mlp(x, mod, ng, up_all, dn_all, layer, tm, tf):
    b, s, d = x.shape
    f = up_all.shape[2]
    assert f // tf >= 2
    return pl.pallas_call(
        functools.partial(_mlp_kernel, nsub=2),
        grid=(b, s // tm, f // tf),
        in_specs=[
            pl.BlockSpec((None, tm, d), lambda bi, i, j: (bi, i, 0)),
            pl.BlockSpec((None, 6, d), lambda bi, i, j: (bi, 0, 0)),
            pl.BlockSpec((4, d), lambda bi, i, j: (0, 0)),
            pl.BlockSpec((None, d, tf), lambda bi, i, j: (layer, 0, j)),
            pl.BlockSpec((None, tf, d), lambda bi, i, j: (layer, j, 0)),
        ],
        out_specs=pl.BlockSpec((None, tm, d), lambda bi, i, j: (bi, i, 0)),
        out_shape=jax.ShapeDtypeStruct((b, s, d), F32),
        scratch_shapes=[pltpu.VMEM((tm, d), BF16), pltpu.VMEM((tm, d), F32)],
        compiler_params=_params(("arbitrary", "arbitrary", "arbitrary")),
    )(x, mod, ng, up_all, dn_all)


def _mla_proj_kernel(x_ref, mod_ref, ng_ref, cos_ref, sin_ref, kvg_ref, kdc_ref, kdr_ref, kdrr_ref,
                     kvn_ref, uk_ref, uv_ref, dq_ref, qn_ref, uqn_ref, uqr_ref, uqrr_ref,
                     qnope_ref, qrope_ref, knope_ref, krope_ref, v_ref, *, scale):
    x = x_ref[...]
    xn = _rms(x)
    cos = cos_ref[...]
    sin = sin_ref[...]

    hs = (xn * kvg_ref[...]).astype(BF16)
    ckv = _rms(jnp.dot(hs, kdc_ref[...], preferred_element_type=F32)) * kvn_ref[...]
    ckv = ckv.astype(BF16)
    knope_ref[...] = jnp.dot(ckv, uk_ref[...], preferred_element_type=F32).astype(knope_ref.dtype)
    v_ref[...] = jnp.dot(ckv, uv_ref[...], preferred_element_type=F32).astype(v_ref.dtype)
    kr = (jnp.dot(hs, kdr_ref[...], preferred_element_type=F32) * cos
          + jnp.dot(hs, kdrr_ref[...], preferred_element_type=F32) * sin)
    krope_ref[...] = kr.astype(krope_ref.dtype)

    h = (xn * ng_ref[0:1, :] * (1.0 + mod_ref[1:2, :]) + mod_ref[0:1, :]).astype(BF16)
    cq = _rms(jnp.dot(h, dq_ref[...], preferred_element_type=F32)) * qn_ref[...]
    cq = cq.astype(BF16)
    qnope = jnp.dot(cq, uqn_ref[...], preferred_element_type=F32) * scale
    qnope_ref[...] = qnope.astype(qnope_ref.dtype)
    qr = jnp.dot(cq, uqr_ref[...], preferred_element_type=F32)
    qrr = jnp.dot(cq, uqrr_ref[...], preferred_element_type=F32)
    reps = qr.shape[1] // LANES
    cos_w = jnp.concatenate([cos] * reps, axis=1)
    sin_w = jnp.concatenate([sin] * reps, axis=1)
    qrope_ref[...] = ((qr * cos_w + qrr * sin_w) * scale).astype(qrope_ref.dtype)


def _mla_proj(x, mod, ng, cos_t, sin_t, kvg, kdc, kdr, kdrr, kvn, uk, uv, dq, qn, uqn, uqr, uqrr,
              scale, tm):
    b, s, d = x.shape
    dr = uqr.shape[1]
    full = lambda arr: pl.BlockSpec(arr.shape, lambda bi, i: (0,) * arr.ndim)
    row = lambda w: pl.BlockSpec((None, tm, w), lambda bi, i: (bi, i, 0))
    return pl.pallas_call(
        functools.partial(_mla_proj_kernel, scale=scale),
        grid=(b, s // tm),
        in_specs=[row(d), pl.BlockSpec((None, 6, d), lambda bi, i: (bi, 0, 0)), full(ng),
                  row(LANES), row(LANES), full(kvg), full(kdc), full(kdr), full(kdrr), full(kvn),
                  full(uk), full(uv), full(dq), full(qn), full(uqn), full(uqr), full(uqrr)],
        out_specs=[row(d), row(dr), row(d), row(LANES), row(d)],
        out_shape=[jax.ShapeDtypeStruct((b, s, d), BF16), jax.ShapeDtypeStruct((b, s, dr), BF16),
                   jax.ShapeDtypeStruct((b, s, d), BF16), jax.ShapeDtypeStruct((b, s, LANES), BF16),
                   jax.ShapeDtypeStruct((b, s, d), BF16)],
        compiler_params=_params(("arbitrary", "arbitrary")),
    )(x, mod, ng, cos_t, sin_t, kvg, kdc, kdr, kdrr, kvn, uk, uv, dq, qn, uqn, uqr, uqrr)


def _attn_kernel(qn_ref, qr_ref, kn_ref, kr_ref, v_ref, o_ref, *, tq):
    s = qn_ref.shape[0]
    neg = jnp.finfo(F32).min
    lane = lax.broadcasted_iota(jnp.int32, (1, LANES), 1)
    ri = lax.broadcasted_iota(jnp.int32, (tq, tq), 0)
    ci = lax.broadcasted_iota(jnp.int32, (tq, tq), 1)
    causal = ri >= ci
    kr = kr_ref[...]
    k_cat = [jnp.concatenate([kn_ref[:, hd * MLA_NOPE:(hd + 1) * MLA_NOPE], kr], axis=1)
             for hd in range(2)]

    def scores(hd, qi):
        rows = slice(qi * tq, (qi + 1) * tq)
        in_head = (lane >> 6) == hd
        q_rope = jnp.where(in_head, qr_ref[rows, :], jnp.zeros((), BF16))
        q_cat = jnp.concatenate([qn_ref[rows, hd * MLA_NOPE:(hd + 1) * MLA_NOPE], q_rope], axis=1)
        lo = qi * tq
        sd = jnp.where(causal, _dot_nt(q_cat, k_cat[hd][lo:lo + tq, :]), neg)
        sl = _dot_nt(q_cat, k_cat[hd][:lo, :]) if qi > 0 else None
        return sd, sl

    def finish(hd, qi, sd, sl):
        rows = slice(qi * tq, (qi + 1) * tq)
        cols = slice(hd * MLA_HEAD_V, (hd + 1) * MLA_HEAD_V)
        lo = qi * tq
        m = jnp.max(sd, axis=-1, keepdims=True)
        if sl is not None:
            m = jnp.maximum(m, jnp.max(sl, axis=-1, keepdims=True))
        pd = jnp.exp2(sd - m)
        den = jnp.sum(pd, axis=-1, keepdims=True)
        acc = _dot(pd, v_ref[lo:lo + tq, cols])
        if sl is not None:
            pl_ = jnp.exp2(sl - m)
            den = den + jnp.sum(pl_, axis=-1, keepdims=True)
            acc = acc + _dot(pl_, v_ref[:lo, cols])
        o_ref[rows, cols] = (acc / den).astype(o_ref.dtype)

    items = [(hd, qi) for qi in range(s // tq) for hd in range(2)]
    nxt = scores(*items[0])
    for idx, item in enumerate(items):
        cur = nxt
        if idx + 1 < len(items):
            nxt = scores(*items[idx + 1])
        finish(*item, *cur)


def _attention(qn, qr, kn, kr, v, tq):
    b, s, d = qn.shape
    npair = d // (2 * MLA_NOPE)
    wide = pl.BlockSpec((None, s, 2 * MLA_NOPE), lambda bi, hp: (bi, 0, hp))
    return pl.pallas_call(
        functools.partial(_attn_kernel, tq=tq),
        grid=(b, npair),
        in_specs=[wide, pl.BlockSpec((None, s, LANES), lambda bi, hp: (bi, 0, hp)), wide,
                  pl.BlockSpec((None, s, LANES), lambda bi, hp: (bi, 0, 0)), wide],
        out_specs=wide,
        out_shape=jax.ShapeDtypeStruct((b, s, d), BF16),
        compiler_params=_params(("arbitrary", "arbitrary")),
    )(qn, qr, kn, kr, v)


def _pad_cols(w, n):
    return jnp.pad(w, ((0, 0), (0, n - w.shape[1])))


def _pad_rows(w, n):
    return jnp.pad(w, ((0, n - w.shape[0]), (0, 0)))


def _rot_half_cols(w):
    k, n = w.shape
    half = MLA_ROPE // 2
    w3 = w.reshape(k, n // MLA_ROPE, MLA_ROPE)
    return jnp.concatenate([-w3[..., half:], w3[..., :half]], axis=-1).reshape(k, n)


def kernel(x, c, positions, ada_w, ada_b, norm_g, mlp_up, mlp_down, rw_mu, rw_rkv, rw_w0, rw_w1,
           rw_w2, rw_a0, rw_a1, rw_a2, rw_g1, rw_g2, rw_kk, rw_ka, rw_rk, rw_lnx, rw_o, mla_dq,
           mla_qnorm, mla_uq, mla_o, kv_in_g, kv_down, kv_norm, kv_uk, kv_uv):
    b, s, d = x.shape
    depth = ada_w.shape[0]
    n_rw = rw_mu.shape[0]
    kv_lora = kv_norm.shape[0]
    heads = d // MLA_HEAD_V
    assert d % (4 * LANES) == 0 and s % RW_CHUNK == 0

    tm = min(s, 512)
    tm_small = min(s, 256)
    tf = min(mlp_up.shape[2], 1024)
    tb = min(s, 512)
    tq = min(s, 512)
    scan_pairs = 4

    up_all = _to_bf16(mlp_up)
    dn_all = _to_bf16(mlp_down)
    rkv_all = _to_bf16(rw_rkv)
    rwo_all = _to_bf16(rw_o)
    mlao_all = _to_bf16(mla_o)
    mod_all = _ada_mod(c, ada_w, ada_b).reshape(depth, b, 6, d)
    cos_t, sin_t = _rope_tables(positions)
    shared = None

    for l in range(depth):
        mod = mod_all[l]
        ng = norm_g[l]
        if l < n_rw:
            i = l
            lora = max(LANES, -(-rw_w1.shape[2] // LANES) * LANES)
            r, k, v = _rw_proj(x, mod, ng, rw_mu[i], rkv_all[i], tm_small)
            lw, a, g = _rw_lora(
                x, mod, ng, rw_mu[i], rw_w0[i].reshape(1, d),
                _pad_cols(rw_w1[i], lora).astype(BF16), _pad_rows(rw_w2[i], lora).astype(BF16),
                rw_a0[i].reshape(1, d),
                _pad_cols(rw_a1[i], lora).astype(BF16), _pad_rows(rw_a2[i], lora).astype(BF16),
                rw_g1[i].astype(BF16), rw_g2[i].astype(BF16), tm_small)
            mixed = _rw_scan(r, k, v, lw, a, g, rw_kk[i].reshape(1, d), rw_ka[i].reshape(1, d),
                             rw_rk[i].reshape(1, d), rw_lnx[i, 0].reshape(1, d),
                             rw_lnx[i, 1].reshape(1, d), tb, scan_pairs)
            x = _oproj(mixed, rwo_all[i], x, mod, ng, tm)
        else:
            i = l - n_rw
            uq = mla_uq[i]
            q_lora = uq.shape[0]
            uqn = uq[:, :, :MLA_NOPE].reshape(q_lora, heads * MLA_NOPE)
            uqr = uq[:, :, MLA_NOPE:].reshape(q_lora, heads * MLA_ROPE)
            kdr = kv_down[:, kv_lora:]
            kdr2 = jnp.concatenate([kdr, kdr], axis=1)
            scale = float((MLA_NOPE + MLA_ROPE) ** -0.5) * math.log2(math.e)
            qn, qr, kn, kr, v = _mla_proj(
                x, mod, ng, cos_t, sin_t, kv_in_g.reshape(1, d),
                kv_down[:, :kv_lora].astype(BF16), kdr2.astype(BF16),
                _rot_half_cols(kdr2).astype(BF16), kv_norm.reshape(1, kv_lora),
                kv_uk.reshape(kv_lora, -1).astype(BF16), kv_uv.reshape(kv_lora, -1).astype(BF16),
                mla_dq[i].astype(BF16), mla_qnorm[i].reshape(1, q_lora), uqn.astype(BF16),
                uqr.astype(BF16), _rot_half_cols(uqr).astype(BF16), scale, tm_small)
            if shared is None:
                shared = (kn, kr, v)
            kn, kr, v = shared
            att = _attention(qn, qr, kn, kr, v, tq)
            x = _oproj(att, mlao_all[i], x, mod, ng, tm)
        x = _mlp(x, mod, ng, up_all, dn_all, l, tm, tf)
    return x
```

```python
import functools
import math

import jax
import jax.numpy as jnp
from jax import lax
from jax.experimental import pallas as pl
from jax.experimental.pallas import tpu as pltpu

F32 = jnp.float32
BF16 = jnp.bfloat16

LANES = 128
NORM_EPS = 1e-6
RW_HEAD = 64
RW_GN_EPS = RW_HEAD * 1e-5
RW_CHUNK = 64
MLA_HEAD_V = 128
MLA_NOPE = 128
MLA_ROPE = 64
ROPE_THETA = 10000.0
VMEM_LIMIT = 56 * 1024 * 1024


def _params(sem):
    return pltpu.CompilerParams(dimension_semantics=sem, vmem_limit_bytes=VMEM_LIMIT)


def _dot(a, b):
    return jnp.dot(a.astype(BF16), b.astype(BF16), preferred_element_type=F32)


def _dot_nt(a, b):
    return lax.dot_general(a.astype(BF16), b.astype(BF16), (((1,), (1,)), ((), ())),
                           preferred_element_type=F32)


def _rms(x):
    return x * lax.rsqrt(jnp.mean(x * x, axis=-1, keepdims=True) + NORM_EPS)


def _normmod(x, g, scale, shift):
    return _rms(x) * g * (1.0 + scale) + shift


def _sigmoid(x):
    return 1.0 / (1.0 + jnp.exp(-x))


def _sigmoid_t(x):
    return 0.5 * jnp.tanh(0.5 * x) + 0.5


def _split2(x):
    hi = x.astype(BF16)
    lo = (x - hi.astype(F32)).astype(BF16)
    return hi, lo


def _split3(x):
    hi = x.astype(BF16)
    r1 = x - hi.astype(F32)
    mid = r1.astype(BF16)
    lo = (r1 - mid.astype(F32)).astype(BF16)
    return hi, mid, lo


CAST_BLOCK_BYTES = 8 * 1024 * 1024


def _cast_kernel(x_ref, o_ref):
    o_ref[...] = x_ref[...].astype(o_ref.dtype)


def _to_bf16(w):
    shape = w.shape
    w2 = w.reshape(-1, shape[-1])
    r, c = w2.shape
    tr = min(r, max(16, CAST_BLOCK_BYTES // (4 * c)))
    if r % tr or tr % 16:
        return w.astype(BF16)
    out = pl.pallas_call(
        _cast_kernel,
        grid=(r // tr,),
        in_specs=[pl.BlockSpec((tr, c), lambda i: (i, 0))],
        out_specs=pl.BlockSpec((tr, c), lambda i: (i, 0)),
        out_shape=jax.ShapeDtypeStruct((r, c), BF16),
        compiler_params=_params(("arbitrary",)),
    )(w2)
    return out.reshape(shape)


def _ada_kernel(c_ref, w_ref, b_ref, o_ref):
    c = c_ref[...]
    c_hi, c_lo = _split2(c * _sigmoid(c))
    w_hi, w_lo = _split2(w_ref[...])
    o_ref[...] = (jnp.dot(c_hi, w_hi, preferred_element_type=F32)
                  + jnp.dot(c_lo, w_hi, preferred_element_type=F32)
                  + jnp.dot(c_hi, w_lo, preferred_element_type=F32)) + b_ref[...]


def _ada_mod(c, ada_w, ada_b):
    depth, d, n = ada_w.shape
    b = c.shape[0]
    tn = 1536 if n % 1536 == 0 else n
    return pl.pallas_call(
        _ada_kernel,
        grid=(depth, n // tn),
        in_specs=[
            pl.BlockSpec((b, d), lambda l, j: (0, 0)),
            pl.BlockSpec((None, d, tn), lambda l, j: (l, 0, j)),
            pl.BlockSpec((None, 1, tn), lambda l, j: (l, 0, j)),
        ],
        out_specs=pl.BlockSpec((None, b, tn), lambda l, j: (l, 0, j)),
        out_shape=jax.ShapeDtypeStruct((depth, b, n), F32),
        compiler_params=_params(("arbitrary", "arbitrary")),
    )(c, ada_w, ada_b.reshape(depth, 1, n))


def _rope_kernel(pos_ref, invf_ref, cos_ref, sin_ref):
    ang = pos_ref[...] * invf_ref[...]
    cos_ref[...] = jnp.cos(ang)
    sin_ref[...] = jnp.sin(ang)


def _rope_tables(positions):
    b, s = positions.shape
    half = MLA_ROPE // 2
    inv_freq = 1.0 / (ROPE_THETA ** (jnp.arange(0, MLA_ROPE, 2, dtype=F32) / MLA_ROPE))
    invf = jnp.tile(inv_freq, LANES // half).reshape(1, LANES)
    pos = jnp.broadcast_to(positions.astype(F32)[..., None], (b, s, LANES))
    tm = min(s, 512)
    spec = pl.BlockSpec((None, tm, LANES), lambda bi, i: (bi, i, 0))
    return pl.pallas_call(
        _rope_kernel,
        grid=(b, s // tm),
        in_specs=[spec, pl.BlockSpec((1, LANES), lambda bi, i: (0, 0))],
        out_specs=[spec, spec],
        out_shape=[jax.ShapeDtypeStruct((b, s, LANES), F32)] * 2,
        compiler_params=_params(("arbitrary", "arbitrary")),
    )(pos, invf)


def _rw_in_kernel(x_ref, xp_ref, mod_ref, ng_ref, mu_ref, w_ref, w0_ref, w1_ref, w2_ref,
                  a0_ref, a1_ref, a2_ref, g1_ref, g2_ref,
                  r_ref, k_ref, v_ref, lw_ref, a_ref, g_ref, *, sub):
    i = pl.program_id(1)
    g = ng_ref[0:1, :]
    shift = mod_ref[0:1, :]
    scale = mod_ref[1:2, :]
    prev = jnp.where(i == 0, 0.0, _normmod(xp_ref[7:8, :], g, scale, shift))
    row = lax.broadcasted_iota(jnp.int32, (8, x_ref.shape[1]), 0)
    for sb in range(x_ref.shape[0] // sub):
        rows = slice(sb * sub, (sb + 1) * sub)
        h = _normmod(x_ref[rows, :], g, scale, shift)
        hs = pltpu.roll(h, 1, 0)
        hs = jnp.concatenate([jnp.where(row == 0, prev, hs[0:8, :]), hs[8:, :]], axis=0)
        prev = h[sub - 1:sub, :]
        xx = hs - h

        def mix(s):
            return (h + xx * mu_ref[s:s + 1, :]).astype(BF16)

        for s, o_ref in enumerate((r_ref, k_ref, v_ref)):
            o_ref[rows, :] = jnp.dot(mix(s), w_ref[s],
                                     preferred_element_type=F32).astype(o_ref.dtype)
        wpre = w0_ref[...] + _dot(jnp.tanh(_dot(mix(3), w1_ref[...])), w2_ref[...])
        lw_ref[rows, :] = (-math.exp(-0.5)) * _sigmoid_t(wpre)
        a_ref[rows, :] = _sigmoid_t(
            a0_ref[...] + _dot(_dot(mix(4), a1_ref[...]), a2_ref[...])).astype(a_ref.dtype)
        g_ref[rows, :] = _dot(_sigmoid_t(_dot(mix(5), g1_ref[...])),
                              g2_ref[...]).astype(g_ref.dtype)


def _rw_in(x, mod, ng, mu, w_rkv, w0, w1, w2, a0, a1, a2, g1, g2, tm):
    b, s, d = x.shape
    row = pl.BlockSpec((None, tm, d), lambda bi, i: (bi, i, 0))

    def resident(arr):
        return pl.BlockSpec(arr.shape, lambda bi, i: (0,) * arr.ndim,
                            pipeline_mode=pl.Buffered(1))

    return pl.pallas_call(
        functools.partial(_rw_in_kernel, sub=tm // 2),
        grid=(b, s // tm),
        in_specs=[
            row,
            pl.BlockSpec((None, 8, d), lambda bi, i: (bi, jnp.maximum(i * (tm // 8) - 1, 0), 0)),
            pl.BlockSpec((None, 6, d), lambda bi, i: (bi, 0, 0)),
            pl.BlockSpec((4, d), lambda bi, i: (0, 0)),
            pl.BlockSpec((6, d), lambda bi, i: (0, 0)),
            resident(w_rkv), resident(w0), resident(w1), resident(w2),
            resident(a0), resident(a1), resident(a2), resident(g1), resident(g2),
        ],
        out_specs=[row] * 6,
        out_shape=[jax.ShapeDtypeStruct((b, s, d), BF16)] * 3
        + [jax.ShapeDtypeStruct((b, s, d), F32)] + [jax.ShapeDtypeStruct((b, s, d), BF16)] * 2,
        compiler_params=_params(("arbitrary", "arbitrary")),
    )(x, x, mod, ng, mu, w_rkv, w0, w1, w2, a0, a1, a2, g1, g2)


def _stack2(x, m0, m1):
    return jnp.concatenate([x * m0, x * m1], axis=0)


def _rw_scan_kernel(r_ref, k_ref, v_ref, lw_ref, a_ref, g_ref, kkw_ref, ka_ref, rk_ref,
                    lng_ref, lnb_ref, o_ref, s_ref, q_ref, m_ref, bt_ref, yi_ref, gc_ref,
                    bonus_ref, gate_ref, y_ref, *, nchunk, npp, nblk):
    C = RW_CHUNK
    C2 = 2 * C
    nunit = npp * nchunk
    k_step = pl.program_id(0)
    slot_x = k_step % 2
    slot_y = 1 - slot_x
    base_x = slot_x * nunit
    base_y = slot_y * nunit
    first_block = (k_step + (nblk - 1)) % nblk == 0

    @pl.when(k_step == 0)
    def _():
        for ref in (s_ref, q_ref, m_ref, bt_ref, yi_ref, gc_ref, bonus_ref, gate_ref):
            ref[...] = jnp.zeros_like(ref)

    lane = lax.broadcasted_iota(jnp.int32, (1, LANES), 1)
    m0 = (lane < RW_HEAD).astype(F32)
    m1 = 1.0 - m0
    ri = lax.broadcasted_iota(jnp.int32, (C2, C2), 0)
    ci = lax.broadcasted_iota(jnp.int32, (C2, C2), 1)
    same_head = (ri >> 6) == (ci >> 6)
    strict = (ri & (C - 1)) > (ci & (C - 1))
    incl = (ri & (C - 1)) >= (ci & (C - 1))
    eye = (ri == ci).astype(F32)
    off_masks = []
    for j in range(C.bit_length() - 1):
        off_masks.append(((ri >> (j + 1)) == (ci >> (j + 1)))
                         & (((ri >> j) & 1) == 1) & (((ci >> j) & 1) == 0))
    block_ones = same_head.astype(BF16)
    tri = (incl[:C, :C]).astype(BF16)

    def head_sum(x):
        hi, lo = _split2(x)
        return (jnp.dot(hi, block_ones, preferred_element_type=F32)
                + jnp.dot(lo, block_ones, preferred_element_type=F32))

    def stack(x):
        return _stack2(x, m0, m1).astype(BF16)

    units = []

    def prep_inputs(pp):
        cols = slice(pp * LANES, (pp + 1) * LANES)
        K = k_ref[:, cols].astype(F32)
        A = a_ref[:, cols].astype(F32)
        LW = lw_ref[:, cols]
        kk0 = K * kkw_ref[:, cols]
        kk = kk0 / jnp.maximum(jnp.sqrt(head_sum(kk0 * kk0)), 1e-12)
        k2 = K * (1.0 + (A - 1.0) * ka_ref[:, cols])
        kka = kk * A
        bonus_ref[slot_x, :, cols] = (head_sum(r_ref[:, cols] * k2 * rk_ref[:, cols])
                                      * v_ref[:, cols])
        gate_ref[slot_x, :, cols] = g_ref[:, cols].astype(F32)
        cl_cat = None
        for part in _split3(LW):
            cat = jnp.concatenate([part[c * C:(c + 1) * C, :] for c in range(nchunk)], axis=1)
            term = jnp.dot(tri, cat, preferred_element_type=F32)
            cl_cat = term if cl_cat is None else cl_cat + term
        for c in range(nchunk):
            rows = slice(c * C, (c + 1) * C)
            cl = cl_cat[:, c * LANES:(c + 1) * LANES]
            cl_end = cl[C - 1:C, :]
            gam_inv = jnp.exp(-cl)
            gam_tail = jnp.exp(cl_end - cl)
            units.append(dict(
                idx=pp * nchunk + c,
                al_s=stack(-kk[rows] * jnp.exp(cl - LW[rows])),
                rb_s=_stack2(r_ref[rows, cols] * jnp.exp(cl), m0, m1),
                be_s=stack(kka[rows] * gam_inv),
                kb_s=stack(k2[rows] * gam_inv),
                bt_s=stack(kka[rows] * gam_tail),
                kt_s=stack(k2[rows] * gam_tail),
                v_s=stack(v_ref[rows, cols]),
                gc=jnp.exp(cl_end)))

    def stage_gram():
        for u in units:
            G = _dot_nt(jnp.concatenate([u["al_s"], u["rb_s"].astype(BF16)], axis=0),
                        jnp.concatenate([u["be_s"], u["kb_s"]], axis=0))
            u["Lb"] = jnp.where(strict, G[:C2, :C2], 0.0)
            u["Lk"] = jnp.where(strict, G[:C2, C2:], 0.0).astype(BF16)
            u["R"] = jnp.concatenate([jnp.where(incl, G[C2:, :C2], 0.0),
                                      jnp.where(incl, G[C2:, C2:], 0.0)], axis=1).astype(BF16)

    def stage_init():
        for u in units:
            u["lkv"] = _dot(u["Lk"], u["v_s"])
            u["T"] = eye + jnp.where(off_masks[0], u["Lb"], 0.0)

    def lower_rows(x, s):
        return jnp.concatenate([x[r:r + s, :] for r in range(s, C2, 2 * s)], axis=0)

    def stage_left(j):
        s = 1 << j
        for u in units:
            lo = jnp.where(off_masks[j], u["Lb"], 0.0)
            u["TL"] = _dot(lower_rows(u["T"], s) if s >= 8 else u["T"], lo)

    def stage_right(j):
        s = 1 << j
        for u in units:
            upd = _dot(u["TL"], u["T"])
            if s >= 8:
                pieces = []
                for m in range(C2 // (2 * s)):
                    pieces.append(u["T"][2 * m * s:(2 * m + 1) * s, :])
                    pieces.append(u["T"][(2 * m + 1) * s:(2 * m + 2) * s, :] + upd[m * s:(m + 1) * s, :])
                u["T"] = jnp.concatenate(pieces, axis=0)
            else:
                u["T"] = u["T"] + upd

    def stage_solve():
        for u in units:
            u["Z"] = _dot(u["T"], jnp.concatenate([u["al_s"], u["lkv"].astype(BF16)], axis=1))

    def stage_store():
        for u in units:
            X = jnp.concatenate(
                [u["Z"],
                 jnp.concatenate([jnp.zeros((C2, C2), F32), u["v_s"].astype(F32)], axis=1)],
                axis=0)
            QY = _dot(u["R"], X)
            MB = _dot(X.T, jnp.concatenate([u["bt_s"], u["kt_s"]], axis=0))
            i = base_x + u["idx"]
            q_ref[i] = (u["rb_s"] + QY[:, :C2]).astype(BF16)
            yi_ref[i] = QY[:, C2:]
            m_ref[i] = MB[:C2, :].astype(BF16)
            bt_ref[i] = MB[C2:, :]
            gc_ref[i] = jnp.broadcast_to(u["gc"], (8, LANES))

    prepare = [functools.partial(prep_inputs, pp) for pp in range(npp)]
    prepare += [stage_gram, stage_init]
    for j in range(1, len(off_masks)):
        prepare += [functools.partial(stage_left, j), functools.partial(stage_right, j)]
    prepare += [stage_solve, stage_store]

    S = [jnp.where(first_block, 0.0, s_ref[pp]) for pp in range(npp)]

    def state_step(c):
        for pp in range(npp):
            i = base_y + pp * nchunk + c
            Sb = S[pp].astype(BF16)
            ys = _dot_nt(q_ref[i], Sb) + yi_ref[i]
            y_ref[c * C:(c + 1) * C, pp * LANES:(pp + 1) * LANES] = ys[:C, :] + ys[C:, :]
            S[pp] = (S[pp] * gc_ref[i][0:1, :]
                     + jnp.dot(Sb, m_ref[i], preferred_element_type=F32) + bt_ref[i])

    def finish_out():
        inv_n = 1.0 / RW_HEAD
        for pp in range(npp):
            s_ref[pp] = S[pp]
            cols = slice(pp * LANES, (pp + 1) * LANES)
            y = y_ref[:, cols]
            mean = head_sum(y) * inv_n
            yc = y - mean
            var = head_sum(yc * yc) * inv_n
            yn = yc * lax.rsqrt(var + RW_GN_EPS) * lng_ref[:, cols] + lnb_ref[:, cols]
            o_ref[:, cols] = ((yn + bonus_ref[slot_y, :, cols])
                              * gate_ref[slot_y, :, cols]).astype(o_ref.dtype)

    finish = [functools.partial(state_step, c) for c in range(nchunk)] + [finish_out]

    done = 0
    for si, stage in enumerate(prepare):
        want = ((si + 1) * len(finish)) // len(prepare)
        while done < want:
            finish[done]()
            done += 1
        stage()


def _rw_scan(r, k, v, lw, a, g, kkw, ka, rk, lng, lnb, tb, npp):
    b, s, d = r.shape
    nchunk = tb // RW_CHUNK
    width = npp * LANES
    C2 = 2 * RW_CHUNK
    nunit = npp * nchunk
    nblk = s // tb
    ngrp = d // width
    total = b * ngrp * nblk

    def split(kk):
        return kk // (ngrp * nblk), kk % nblk, (kk // nblk) % ngrp

    def tok_in(kq):
        return split(jnp.minimum(kq, total - 1))

    def tok_out(kq):
        return split(jnp.maximum(kq - 1, 0))

    tin = pl.BlockSpec((None, tb, width), tok_in)
    pin = pl.BlockSpec((1, width), lambda kq: (0, tok_in(kq)[2]))
    pout = pl.BlockSpec((1, width), lambda kq: (0, tok_out(kq)[2]))
    return pl.pallas_call(
        functools.partial(_rw_scan_kernel, nchunk=nchunk, npp=npp, nblk=nblk),
        grid=(total + 1,),
        in_specs=[tin, tin, tin, tin, tin, tin, pin, pin, pin, pout, pout],
        out_specs=pl.BlockSpec((None, tb, width), tok_out),
        out_shape=jax.ShapeDtypeStruct((b, s, d), BF16),
        scratch_shapes=[
            pltpu.VMEM((npp, C2, C2), F32),
            pltpu.VMEM((2 * nunit, C2, C2), BF16),
            pltpu.VMEM((2 * nunit, C2, C2), BF16),
            pltpu.VMEM((2 * nunit, C2, C2), F32),
            pltpu.VMEM((2 * nunit, C2, C2), F32),
            pltpu.VMEM((2 * nunit, 8, LANES), F32),
            pltpu.VMEM((2, tb, width), F32),
            pltpu.VMEM((2, tb, width), F32),
            pltpu.VMEM((tb, width), F32),
        ],
        compiler_params=_params(("arbitrary",)),
    )(r, k, v, lw, a, g, kkw, ka, rk, lng, lnb)


def _oproj_kernel(a_ref, w_ref, x_ref, mod_ref, ng_ref, o_ref):
    nsub = 2
    sub = a_ref.shape[0] // nsub
    for sb in range(nsub):
        rows = slice(sb * sub, (sb + 1) * sub)
        y = jnp.dot(a_ref[rows, :], w_ref[...], preferred_element_type=F32)
        o_ref[rows, :] = x_ref[rows, :] + mod_ref[2:3, :] * (_rms(y) * ng_ref[1:2, :])


def _oproj(a, w, x, mod, ng, tm):
    b, s, d = x.shape
    k = a.shape[-1]
    return pl.pallas_call(
        _oproj_kernel,
        grid=(b, s // tm),
        in_specs=[
            pl.BlockSpec((None, tm, k), lambda bi, i: (bi, i, 0)),
            pl.BlockSpec((k, d), lambda bi, i: (0, 0)),
            pl.BlockSpec((None, tm, d), lambda bi, i: (bi, i, 0)),
            pl.BlockSpec((None, 6, d), lambda bi, i: (bi, 0, 0)),
            pl.BlockSpec((4, d), lambda bi, i: (0, 0)),
        ],
        out_specs=pl.BlockSpec((None, tm, d), lambda bi, i: (bi, i, 0)),
        out_shape=jax.ShapeDtypeStruct((b, s, d), F32),
        compiler_params=_params(("arbitrary", "arbitrary")),
    )(a, w, x, mod, ng)


def _mlp_kernel(x_ref, mod_ref, ng_ref, up_ref, dn_ref, o_ref, h_ref, acc_ref, *, nsub):
    j = pl.program_id(2)
    last = pl.num_programs(2) - 1
    sub = x_ref.shape[0] // nsub

    def ffn(h):
        u = jnp.maximum(jnp.dot(h, up_ref[...], preferred_element_type=F32), 0.0)
        return jnp.dot((u * u).astype(BF16), dn_ref[...], preferred_element_type=F32)

    @pl.when(j == 0)
    def _():
        for sb in range(nsub):
            rows = slice(sb * sub, (sb + 1) * sub)
            h = _normmod(x_ref[rows, :], ng_ref[2:3, :], mod_ref[4:5, :], mod_ref[3:4, :])
            h = h.astype(BF16)
            h_ref[rows, :] = h
            acc_ref[rows, :] = ffn(h)

    @pl.when(jnp.logical_and(j > 0, j < last))
    def _():
        acc_ref[...] += ffn(h_ref[...])

    @pl.when(j == last)
    def _():
        for sb in range(nsub):
            rows = slice(sb * sub, (sb + 1) * sub)
            y = acc_ref[rows, :] + ffn(h_ref[rows, :])
            o_ref[rows, :] = x_ref[rows, :] + mod_ref[5:6, :] * (_rms(y) * ng_ref[3:4, :])


def _mlp(x, mod, ng, up_all, dn_all, layer, tm, tf):
    b, s, d = x.shape
    f = up_all.shape[2]
    assert f // tf >= 2
    return pl.pallas_call(
        functools.partial(_mlp_kernel, nsub=2),
        grid=(b, s // tm, f // tf),
        in_specs=[
            pl.BlockSpec((None, tm, d), lambda bi, i, j: (bi, i, 0)),
            pl.BlockSpec((None, 6, d), lambda bi, i, j: (bi, 0, 0)),
            pl.BlockSpec((4, d), lambda bi, i, j: (0, 0)),
            pl.BlockSpec((None, d, tf), lambda bi, i, j: (layer, 0, j)),
            pl.BlockSpec((None, tf, d), lambda bi, i, j: (layer, j, 0)),
        ],
        out_specs=pl.BlockSpec((None, tm, d), lambda bi, i, j: (bi, i, 0)),
        out_shape=jax.ShapeDtypeStruct((b, s, d), F32),
        scratch_shapes=[pltpu.VMEM((tm, d), BF16), pltpu.VMEM((tm, d), F32)],
        compiler_params=_params(("arbitrary", "arbitrary", "arbitrary")),
    )(x, mod, ng, up_all, dn_all)


def _mla_proj_kernel(x_ref, mod_ref, ng_ref, cos_ref, sin_ref, kvg_ref, kdc_ref, kdr_ref, kdrr_ref,
                     kvn_ref, uk_ref, uv_ref, dq_ref, qn_ref, uqn_ref, uqr_ref, uqrr_ref,
                     qnope_ref, qrope_ref, knope_ref, krope_ref, v_ref, *, scale):
    x = x_ref[...]
    xn = _rms(x)
    cos = cos_ref[...]
    sin = sin_ref[...]

    hs = (xn * kvg_ref[...]).astype(BF16)
    ckv = _rms(jnp.dot(hs, kdc_ref[...], preferred_element_type=F32)) * kvn_ref[...]
    ckv = ckv.astype(BF16)
    knope_ref[...] = jnp.dot(ckv, uk_ref[...], preferred_element_type=F32).astype(knope_ref.dtype)
    v_ref[...] = jnp.dot(ckv, uv_ref[...], preferred_element_type=F32).astype(v_ref.dtype)
    kr = (jnp.dot(hs, kdr_ref[...], preferred_element_type=F32) * cos
          + jnp.dot(hs, kdrr_ref[...], preferred_element_type=F32) * sin)
    krope_ref[...] = kr.astype(krope_ref.dtype)

    h = (xn * ng_ref[0:1, :] * (1.0 + mod_ref[1:2, :]) + mod_ref[0:1, :]).astype(BF16)
    cq = _rms(jnp.dot(h, dq_ref[...], preferred_element_type=F32)) * qn_ref[...]
    cq = cq.astype(BF16)
    qnope = jnp.dot(cq, uqn_ref[...], preferred_element_type=F32) * scale
    qnope_ref[...] = qnope.astype(qnope_ref.dtype)
    qr = jnp.dot(cq, uqr_ref[...], preferred_element_type=F32)
    qrr = jnp.dot(cq, uqrr_ref[...], preferred_element_type=F32)
    reps = qr.shape[1] // LANES
    cos_w = jnp.concatenate([cos] * reps, axis=1)
    sin_w = jnp.concatenate([sin] * reps, axis=1)
    qrope_ref[...] = ((qr * cos_w + qrr * sin_w) * scale).astype(qrope_ref.dtype)


def _mla_proj(x, mod, ng, cos_t, sin_t, kvg, kdc, kdr, kdrr, kvn, uk, uv, dq, qn, uqn, uqr, uqrr,
              scale, tm):
    b, s, d = x.shape
    dr = uqr.shape[1]
    full = lambda arr: pl.BlockSpec(arr.shape, lambda bi, i: (0,) * arr.ndim)
    row = lambda w: pl.BlockSpec((None, tm, w), lambda bi, i: (bi, i, 0))
    return pl.pallas_call(
        functools.partial(_mla_proj_kernel, scale=scale),
        grid=(b, s // tm),
        in_specs=[row(d), pl.BlockSpec((None, 6, d), lambda bi, i: (bi, 0, 0)), full(ng),
                  row(LANES), row(LANES), full(kvg), full(kdc), full(kdr), full(kdrr), full(kvn),
                  full(uk), full(uv), full(dq), full(qn), full(uqn), full(uqr), full(uqrr)],
        out_specs=[row(d), row(dr), row(d), row(LANES), row(d)],
        out_shape=[jax.ShapeDtypeStruct((b, s, d), BF16), jax.ShapeDtypeStruct((b, s, dr), BF16),
                   jax.ShapeDtypeStruct((b, s, d), BF16), jax.ShapeDtypeStruct((b, s, LANES), BF16),
                   jax.ShapeDtypeStruct((b, s, d), BF16)],
        compiler_params=_params(("arbitrary", "arbitrary")),
    )(x, mod, ng, cos_t, sin_t, kvg, kdc, kdr, kdrr, kvn, uk, uv, dq, qn, uqn, uqr, uqrr)


def _attn_kernel(qn_ref, qr_ref, kn_ref, kr_ref, v_ref, o_ref, *, tq):
    s = qn_ref.shape[0]
    neg = jnp.finfo(F32).min
    lane = lax.broadcasted_iota(jnp.int32, (1, LANES), 1)
    ri = lax.broadcasted_iota(jnp.int32, (tq, tq), 0)
    ci = lax.broadcasted_iota(jnp.int32, (tq, tq), 1)
    causal = ri >= ci
    kr = kr_ref[...]
    k_cat = [jnp.concatenate([kn_ref[:, hd * MLA_NOPE:(hd + 1) * MLA_NOPE], kr], axis=1)
             for hd in range(2)]

    def scores(hd, qi):
        rows = slice(qi * tq, (qi + 1) * tq)
        in_head = (lane >> 6) == hd
        q_rope = jnp.where(in_head, qr_ref[rows, :], jnp.zeros((), BF16))
        q_cat = jnp.concatenate([qn_ref[rows, hd * MLA_NOPE:(hd + 1) * MLA_NOPE], q_rope], axis=1)
        lo = qi * tq
        sd = jnp.where(causal, _dot_nt(q_cat, k_cat[hd][lo:lo + tq, :]), neg)
        sl = _dot_nt(q_cat, k_cat[hd][:lo, :]) if qi > 0 else None
        return sd, sl

    def finish(hd, qi, sd, sl):
        rows = slice(qi * tq, (qi + 1) * tq)
        cols = slice(hd * MLA_HEAD_V, (hd + 1) * MLA_HEAD_V)
        lo = qi * tq
        m = jnp.max(sd, axis=-1, keepdims=True)
        if sl is not None:
            m = jnp.maximum(m, jnp.max(sl, axis=-1, keepdims=True))
        pd = jnp.exp2(sd - m)
        den = jnp.sum(pd, axis=-1, keepdims=True)
        acc = _dot(pd, v_ref[lo:lo + tq, cols])
        if sl is not None:
            pl_ = jnp.exp2(sl - m)
            den = den + jnp.sum(pl_, axis=-1, keepdims=True)
            acc = acc + _dot(pl_, v_ref[:lo, cols])
        o_ref[rows, cols] = (acc / den).astype(o_ref.dtype)

    items = [(hd, qi) for qi in range(s // tq) for hd in range(2)]
    nxt = scores(*items[0])
    for idx, item in enumerate(items):
        cur = nxt
        if idx + 1 < len(items):
            nxt = scores(*items[idx + 1])
        finish(*item, *cur)


def _attention(qn, qr, kn, kr, v, tq):
    b, s, d = qn.shape
    npair = d // (2 * MLA_NOPE)
    wide = pl.BlockSpec((None, s, 2 * MLA_NOPE), lambda bi, hp: (bi, 0, hp))
    return pl.pallas_call(
        functools.partial(_attn_kernel, tq=tq),
        grid=(b, npair),
        in_specs=[wide, pl.BlockSpec((None, s, LANES), lambda bi, hp: (bi, 0, hp)), wide,
                  pl.BlockSpec((None, s, LANES), lambda bi, hp: (bi, 0, 0)), wide],
        out_specs=wide,
        out_shape=jax.ShapeDtypeStruct((b, s, d), BF16),
        compiler_params=_params(("arbitrary", "arbitrary")),
    )(qn, qr, kn, kr, v)


def _pad_cols(w, n):
    return jnp.pad(w, ((0, 0), (0, n - w.shape[1])))


def _pad_rows(w, n):
    return jnp.pad(w, ((0, n - w.shape[0]), (0, 0)))


def _rot_half_cols(w):
    k, n = w.shape
    half = MLA_ROPE // 2
    w3 = w.reshape(k, n // MLA_ROPE, MLA_ROPE)
    return jnp.concatenate([-w3[..., half:], w3[..., :half]], axis=-1).reshape(k, n)


def kernel(x, c, positions, ada_w, ada_b, norm_g, mlp_up, mlp_down, rw_mu, rw_rkv, rw_w0, rw_w1,
           rw_w2, rw_a0, rw_a1, rw_a2, rw_g1, rw_g2, rw_kk, rw_ka, rw_rk, rw_lnx, rw_o, mla_dq,
           mla_qnorm, mla_uq, mla_o, kv_in_g, kv_down, kv_norm, kv_uk, kv_uv):
    b, s, d = x.shape
    depth = ada_w.shape[0]
    n_rw = rw_mu.shape[0]
    kv_lora = kv_norm.shape[0]
    heads = d // MLA_HEAD_V
    assert d % (4 * LANES) == 0 and s % RW_CHUNK == 0

    tm = min(s, 512)
    tm_small = min(s, 256)
    tf = min(mlp_up.shape[2], 1024)
    tb = min(s, 512)
    tq = min(s, 512)
    scan_pairs = 4

    up_all = _to_bf16(mlp_up)
    dn_all = _to_bf16(mlp_down)
    rkv_all = _to_bf16(rw_rkv)
    rwo_all = _to_bf16(rw_o)
    mlao_all = _to_bf16(mla_o)
    mod_all = _ada_mod(c, ada_w, ada_b).reshape(depth, b, 6, d)
    cos_t, sin_t = _rope_tables(positions)
    shared = None

    for l in range(depth):
        mod = mod_all[l]
        ng = norm_g[l]
        if l < n_rw:
            i = l
            lora = max(LANES, -(-rw_w1.shape[2] // LANES) * LANES)
            r, k, v, lw, a, g = _rw_in(
                x, mod, ng, rw_mu[i], rkv_all[i], rw_w0[i].reshape(1, d),
                _pad_cols(rw_w1[i], lora).astype(BF16), _pad_rows(rw_w2[i], lora).astype(BF16),
                rw_a0[i].reshape(1, d),
                _pad_cols(rw_a1[i], lora).astype(BF16), _pad_rows(rw_a2[i], lora).astype(BF16),
                rw_g1[i].astype(BF16), rw_g2[i].astype(BF16), tm_small)
            mixed = _rw_scan(r, k, v, lw, a, g, rw_kk[i].reshape(1, d), rw_ka[i].reshape(1, d),
                             rw_rk[i].reshape(1, d), rw_lnx[i, 0].reshape(1, d),
                             rw_lnx[i, 1].reshape(1, d), tb, scan_pairs)
            x = _oproj(mixed, rwo_all[i], x, mod, ng, tm)
        else:
            i = l - n_rw
            uq = mla_uq[i]
            q_lora = uq.shape[0]
            uqn = uq[:, :, :MLA_NOPE].reshape(q_lora, heads * MLA_NOPE)
            uqr = uq[:, :, MLA_NOPE:].reshape(q_lora, heads * MLA_ROPE)
            kdr = kv_down[:, kv_lora:]
            kdr2 = jnp.concatenate([kdr, kdr], axis=1)
            scale = float((MLA_NOPE + MLA_ROPE) ** -0.5) * math.log2(math.e)
            qn, qr, kn, kr, v = _mla_proj(
                x, mod, ng, cos_t, sin_t, kv_in_g.reshape(1, d),
                kv_down[:, :kv_lora].astype(BF16), kdr2.astype(BF16),
                _rot_half_cols(kdr2).astype(BF16), kv_norm.reshape(1, kv_lora),
                kv_uk.reshape(kv_lora, -1).astype(BF16), kv_uv.reshape(kv_lora, -1).astype(BF16),
                mla_dq[i].astype(BF16), mla_qnorm[i].reshape(1, q_lora), uqn.astype(BF16),
                uqr.astype(BF16), _rot_half_cols(uqr).astype(BF16), scale, tm_small)
            if shared is None:
                shared = (kn, kr, v)
            kn, kr, v = shared
            att = _attention(qn, qr, kn, kr, v, tq)
            x = _oproj(att, mlao_all[i], x, mod, ng, tm)
        x = _mlp(x, mod, ng, up_all, dn_all, l, tm, tf)
    return x
```

```python
import functools
import math

import jax
import jax.numpy as jnp
from jax import lax
from jax.experimental import pallas as pl
from jax.experimental.pallas import tpu as pltpu

F32 = jnp.float32
BF16 = jnp.bfloat16

LANES = 128
NORM_EPS = 1e-6
RW_HEAD = 64
RW_GN_EPS = RW_HEAD * 1e-5
RW_CHUNK = 64
MLA_HEAD_V = 128
MLA_NOPE = 128
MLA_ROPE = 64
ROPE_THETA = 10000.0
VMEM_LIMIT = 56 * 1024 * 1024


def _params(sem):
    return pltpu.CompilerParams(dimension_semantics=sem, vmem_limit_bytes=VMEM_LIMIT)


def _dot(a, b):
    return jnp.dot(a.astype(BF16), b.astype(BF16), preferred_element_type=F32)


def _dot_nt(a, b):
    return lax.dot_general(a.astype(BF16), b.astype(BF16), (((1,), (1,)), ((), ())),
                           preferred_element_type=F32)


def _rms(x):
    return x * lax.rsqrt(jnp.mean(x * x, axis=-1, keepdims=True) + NORM_EPS)


def _normmod(x, g, scale, shift):
    return _rms(x) * g * (1.0 + scale) + shift


def _sigmoid(x):
    return 1.0 / (1.0 + jnp.exp(-x))


def _sigmoid_t(x):
    return 0.5 * jnp.tanh(0.5 * x) + 0.5


def _split2(x):
    hi = x.astype(BF16)
    lo = (x - hi.astype(F32)).astype(BF16)
    return hi, lo


def _split3(x):
    hi = x.astype(BF16)
    r1 = x - hi.astype(F32)
    mid = r1.astype(BF16)
    lo = (r1 - mid.astype(F32)).astype(BF16)
    return hi, mid, lo


CAST_BLOCK_BYTES = 8 * 1024 * 1024


def _cast_kernel(x_ref, o_ref):
    o_ref[...] = x_ref[...].astype(o_ref.dtype)


def _to_bf16(w):
    shape = w.shape
    w2 = w.reshape(-1, shape[-1])
    r, c = w2.shape
    tr = min(r, max(16, CAST_BLOCK_BYTES // (4 * c)))
    if r % tr or tr % 16:
        return w.astype(BF16)
    out = pl.pallas_call(
        _cast_kernel,
        grid=(r // tr,),
        in_specs=[pl.BlockSpec((tr, c), lambda i: (i, 0))],
        out_specs=pl.BlockSpec((tr, c), lambda i: (i, 0)),
        out_shape=jax.ShapeDtypeStruct((r, c), BF16),
        compiler_params=_params(("arbitrary",)),
    )(w2)
    return out.reshape(shape)


def _ada_kernel(c_ref, w_ref, b_ref, o_ref):
    c = c_ref[...]
    c_hi, c_lo = _split2(c * _sigmoid(c))
    w_hi, w_lo = _split2(w_ref[...])
    o_ref[...] = (jnp.dot(c_hi, w_hi, preferred_element_type=F32)
                  + jnp.dot(c_lo, w_hi, preferred_element_type=F32)
                  + jnp.dot(c_hi, w_lo, preferred_element_type=F32)) + b_ref[...]


def _ada_mod(c, ada_w, ada_b):
    depth, d, n = ada_w.shape
    b = c.shape[0]
    tn = 1536 if n % 1536 == 0 else n
    return pl.pallas_call(
        _ada_kernel,
        grid=(depth, n // tn),
        in_specs=[
            pl.BlockSpec((b, d), lambda l, j: (0, 0)),
            pl.BlockSpec((None, d, tn), lambda l, j: (l, 0, j)),
            pl.BlockSpec((None, 1, tn), lambda l, j: (l, 0, j)),
        ],
        out_specs=pl.BlockSpec((None, b, tn), lambda l, j: (l, 0, j)),
        out_shape=jax.ShapeDtypeStruct((depth, b, n), F32),
        compiler_params=_params(("arbitrary", "arbitrary")),
    )(c, ada_w, ada_b.reshape(depth, 1, n))


def _rope_kernel(pos_ref, invf_ref, cos_ref, sin_ref):
    ang = pos_ref[...] * invf_ref[...]
    cos_ref[...] = jnp.cos(ang)
    sin_ref[...] = jnp.sin(ang)


def _rope_tables(positions):
    b, s = positions.shape
    half = MLA_ROPE // 2
    inv_freq = 1.0 / (ROPE_THETA ** (jnp.arange(0, MLA_ROPE, 2, dtype=F32) / MLA_ROPE))
    invf = jnp.tile(inv_freq, LANES // half).reshape(1, LANES)
    pos = jnp.broadcast_to(positions.astype(F32)[..., None], (b, s, LANES))
    tm = min(s, 512)
    spec = pl.BlockSpec((None, tm, LANES), lambda bi, i: (bi, i, 0))
    return pl.pallas_call(
        _rope_kernel,
        grid=(b, s // tm),
        in_specs=[spec, pl.BlockSpec((1, LANES), lambda bi, i: (0, 0))],
        out_specs=[spec, spec],
        out_shape=[jax.ShapeDtypeStruct((b, s, LANES), F32)] * 2,
        compiler_params=_params(("arbitrary", "arbitrary")),
    )(pos, invf)


def _rw_in_kernel(x_ref, xp_ref, mod_ref, ng_ref, mu_ref, w_ref, w0_ref, w1_ref, w2_ref,
                  a0_ref, a1_ref, a2_ref, g1_ref, g2_ref,
                  r_ref, k_ref, v_ref, lw_ref, a_ref, g_ref, *, sub):
    i = pl.program_id(1)
    g = ng_ref[0:1, :]
    shift = mod_ref[0:1, :]
    scale = mod_ref[1:2, :]
    prev = jnp.where(i == 0, 0.0, _normmod(xp_ref[7:8, :], g, scale, shift))
    row = lax.broadcasted_iota(jnp.int32, (8, x_ref.shape[1]), 0)
    for sb in range(x_ref.shape[0] // sub):
        rows = slice(sb * sub, (sb + 1) * sub)
        h = _normmod(x_ref[rows, :], g, scale, shift)
        hs = pltpu.roll(h, 1, 0)
        hs = jnp.concatenate([jnp.where(row == 0, prev, hs[0:8, :]), hs[8:, :]], axis=0)
        prev = h[sub - 1:sub, :]
        xx = hs - h

        def mix(s):
            return (h + xx * mu_ref[s:s + 1, :]).astype(BF16)

        for s, o_ref in enumerate((r_ref, k_ref, v_ref)):
            o_ref[rows, :] = jnp.dot(mix(s), w_ref[s],
                                     preferred_element_type=F32).astype(o_ref.dtype)
        wpre = w0_ref[...] + _dot(jnp.tanh(_dot(mix(3), w1_ref[...])), w2_ref[...])
        lw_ref[rows, :] = (-math.exp(-0.5)) * _sigmoid_t(wpre)
        a_ref[rows, :] = _sigmoid_t(
            a0_ref[...] + _dot(_dot(mix(4), a1_ref[...]), a2_ref[...])).astype(a_ref.dtype)
        g_ref[rows, :] = _dot(_sigmoid_t(_dot(mix(5), g1_ref[...])),
                              g2_ref[...]).astype(g_ref.dtype)


def _rw_in(x, mod, ng, mu, w_rkv, w0, w1, w2, a0, a1, a2, g1, g2, tm):
    b, s, d = x.shape
    row = pl.BlockSpec((None, tm, d), lambda bi, i: (bi, i, 0))

    def resident(arr):
        return pl.BlockSpec(arr.shape, lambda bi, i: (0,) * arr.ndim,
                            pipeline_mode=pl.Buffered(1))

    return pl.pallas_call(
        functools.partial(_rw_in_kernel, sub=tm // 2),
        grid=(b, s // tm),
        in_specs=[
            row,
            pl.BlockSpec((None, 8, d), lambda bi, i: (bi, jnp.maximum(i * (tm // 8) - 1, 0), 0)),
            pl.BlockSpec((None, 6, d), lambda bi, i: (bi, 0, 0)),
            pl.BlockSpec((4, d), lambda bi, i: (0, 0)),
            pl.BlockSpec((6, d), lambda bi, i: (0, 0)),
            resident(w_rkv), resident(w0), resident(w1), resident(w2),
            resident(a0), resident(a1), resident(a2), resident(g1), resident(g2),
        ],
        out_specs=[row] * 6,
        out_shape=[jax.ShapeDtypeStruct((b, s, d), BF16)] * 3
        + [jax.ShapeDtypeStruct((b, s, d), F32)] + [jax.ShapeDtypeStruct((b, s, d), BF16)] * 2,
        compiler_params=_params(("arbitrary", "arbitrary")),
    )(x, x, mod, ng, mu, w_rkv, w0, w1, w2, a0, a1, a2, g1, g2)


def _stack2(x, m0, m1):
    return jnp.concatenate([x * m0, x * m1], axis=0)


def _rw_scan_kernel(r_ref, k_ref, v_ref, lw_ref, a_ref, g_ref, kkw_ref, ka_ref, rk_ref,
                    lng_ref, lnb_ref, o_ref, s_ref, q_ref, m_ref, bt_ref, yi_ref, gc_ref,
                    bonus_ref, gate_ref, y_ref, *, nchunk, npp, nblk):
    C = RW_CHUNK
    C2 = 2 * C
    nunit = npp * nchunk
    k_step = pl.program_id(0)
    slot_x = k_step % 2
    slot_y = 1 - slot_x
    base_x = slot_x * nunit
    base_y = slot_y * nunit
    first_block = (k_step + (nblk - 1)) % nblk == 0

    @pl.when(k_step == 0)
    def _():
        for ref in (s_ref, q_ref, m_ref, bt_ref, yi_ref, gc_ref, bonus_ref, gate_ref):
            ref[...] = jnp.zeros_like(ref)

    lane = lax.broadcasted_iota(jnp.int32, (1, LANES), 1)
    m0 = (lane < RW_HEAD).astype(F32)
    m1 = 1.0 - m0
    ri = lax.broadcasted_iota(jnp.int32, (C2, C2), 0)
    ci = lax.broadcasted_iota(jnp.int32, (C2, C2), 1)
    same_head = (ri >> 6) == (ci >> 6)
    strict = (ri & (C - 1)) > (ci & (C - 1))
    incl = (ri & (C - 1)) >= (ci & (C - 1))
    eye = (ri == ci).astype(F32)
    off_masks = []
    for j in range(C.bit_length() - 1):
        off_masks.append(((ri >> (j + 1)) == (ci >> (j + 1)))
                         & (((ri >> j) & 1) == 1) & (((ci >> j) & 1) == 0))
    block_ones = same_head.astype(BF16)
    tri = (incl[:C, :C]).astype(BF16)

    def head_sum(x):
        hi, lo = _split2(x)
        return (jnp.dot(hi, block_ones, preferred_element_type=F32)
                + jnp.dot(lo, block_ones, preferred_element_type=F32))

    def stack(x):
        return _stack2(x, m0, m1).astype(BF16)

    units = []

    def prep_inputs(pp):
        cols = slice(pp * LANES, (pp + 1) * LANES)
        K = k_ref[:, cols].astype(F32)
        A = a_ref[:, cols].astype(F32)
        LW = lw_ref[:, cols]
        kk0 = K * kkw_ref[:, cols]
        kk = kk0 / jnp.maximum(jnp.sqrt(head_sum(kk0 * kk0)), 1e-12)
        k2 = K * (1.0 + (A - 1.0) * ka_ref[:, cols])
        kka = kk * A
        bonus_ref[slot_x, :, cols] = (head_sum(r_ref[:, cols] * k2 * rk_ref[:, cols])
                                      * v_ref[:, cols])
        gate_ref[slot_x, :, cols] = g_ref[:, cols].astype(F32)
        cl_cat = None
        for part in _split3(LW):
            cat = jnp.concatenate([part[c * C:(c + 1) * C, :] for c in range(nchunk)], axis=1)
            term = jnp.dot(tri, cat, preferred_element_type=F32)
            cl_cat = term if cl_cat is None else cl_cat + term
        for c in range(nchunk):
            rows = slice(c * C, (c + 1) * C)
            cl = cl_cat[:, c * LANES:(c + 1) * LANES]
            cl_end = cl[C - 1:C, :]
            gam_inv = jnp.exp(-cl)
            gam_tail = jnp.exp(cl_end - cl)
            units.append(dict(
                idx=pp * nchunk + c,
                al_s=stack(-kk[rows] * jnp.exp(cl - LW[rows])),
                rb_s=_stack2(r_ref[rows, cols] * jnp.exp(cl), m0, m1),
                be_s=stack(kka[rows] * gam_inv),
                kb_s=stack(k2[rows] * gam_inv),
                bt_s=stack(kka[rows] * gam_tail),
                kt_s=stack(k2[rows] * gam_tail),
                v_s=stack(v_ref[rows, cols]),
                gc=jnp.exp(cl_end)))

    def stage_gram():
        for u in units:
            G = _dot_nt(jnp.concatenate([u["al_s"], u["rb_s"].astype(BF16)], axis=0),
                        jnp.concatenate([u["be_s"], u["kb_s"]], axis=0))
            u["Lb"] = jnp.where(strict, G[:C2, :C2], 0.0)
            u["Lk"] = jnp.where(strict, G[:C2, C2:], 0.0).astype(BF16)
            u["R"] = jnp.concatenate([jnp.where(incl, G[C2:, :C2], 0.0),
                                      jnp.where(incl, G[C2:, C2:], 0.0)], axis=1).astype(BF16)

    def stage_init():
        for u in units:
            u["lkv"] = _dot(u["Lk"], u["v_s"])
            u["T"] = eye + jnp.where(off_masks[0], u["Lb"], 0.0)

    def lower_rows(x, s):
        return jnp.concatenate([x[r:r + s, :] for r in range(s, C2, 2 * s)], axis=0)

    def stage_left(j):
        s = 1 << j
        for u in units:
            lo = jnp.where(off_masks[j], u["Lb"], 0.0)
            u["TL"] = _dot(lower_rows(u["T"], s) if s >= 8 else u["T"], lo)

    def stage_right(j):
        s = 1 << j
        for u in units:
            upd = _dot(u["TL"], u["T"])
            if s >= 8:
                pieces = []
                for m in range(C2 // (2 * s)):
                    pieces.append(u["T"][2 * m * s:(2 * m + 1) * s, :])
                    pieces.append(u["T"][(2 * m + 1) * s:(2 * m + 2) * s, :] + upd[m * s:(m + 1) * s, :])
                u["T"] = jnp.concatenate(pieces, axis=0)
            else:
                u["T"] = u["T"] + upd

    def stage_solve():
        for u in units:
            u["Z"] = _dot(u["T"], jnp.concatenate([u["al_s"], u["lkv"].astype(BF16)], axis=1))

    def stage_store():
        for u in units:
            X = jnp.concatenate(
                [u["Z"],
                 jnp.concatenate([jnp.zeros((C2, C2), F32), u["v_s"].astype(F32)], axis=1)],
                axis=0)
            QY = _dot(u["R"], X)
            MB = _dot(X.T, jnp.concatenate([u["bt_s"], u["kt_s"]], axis=0))
            i = base_x + u["idx"]
            q_ref[i] = (u["rb_s"] + QY[:, :C2]).astype(BF16)
            yi_ref[i] = QY[:, C2:]
            m_ref[i] = MB[:C2, :].astype(BF16)
            bt_ref[i] = MB[C2:, :]
            gc_ref[i] = jnp.broadcast_to(u["gc"], (8, LANES))

    prepare = [functools.partial(prep_inputs, pp) for pp in range(npp)]
    prepare += [stage_gram, stage_init]
    for j in range(1, len(off_masks)):
        prepare += [functools.partial(stage_left, j), functools.partial(stage_right, j)]
    prepare += [stage_solve, stage_store]

    S = [jnp.where(first_block, 0.0, s_ref[pp]) for pp in range(npp)]

    def state_step(c):
        for pp in range(npp):
            i = base_y + pp * nchunk + c
            Sb = S[pp].astype(BF16)
            ys = _dot_nt(q_ref[i], Sb) + yi_ref[i]
            y_ref[c * C:(c + 1) * C, pp * LANES:(pp + 1) * LANES] = ys[:C, :] + ys[C:, :]
            S[pp] = (S[pp] * gc_ref[i][0:1, :]
                     + jnp.dot(Sb, m_ref[i], preferred_element_type=F32) + bt_ref[i])

    def finish_out():
        inv_n = 1.0 / RW_HEAD
        for pp in range(npp):
            s_ref[pp] = S[pp]
            cols = slice(pp * LANES, (pp + 1) * LANES)
            y = y_ref[:, cols]
            mean = head_sum(y) * inv_n
            yc = y - mean
            var = head_sum(yc * yc) * inv_n
            yn = yc * lax.rsqrt(var + RW_GN_EPS) * lng_ref[:, cols] + lnb_ref[:, cols]
            o_ref[:, cols] = ((yn + bonus_ref[slot_y, :, cols])
                              * gate_ref[slot_y, :, cols]).astype(o_ref.dtype)

    finish = [functools.partial(state_step, c) for c in range(nchunk)] + [finish_out]

    done = 0
    for si, stage in enumerate(prepare):
        want = ((si + 1) * len(finish)) // len(prepare)
        while done < want:
            finish[done]()
            done += 1
        stage()


def _rw_scan(r, k, v, lw, a, g, kkw, ka, rk, lng, lnb, tb, npp):
    b, s, d = r.shape
    nchunk = tb // RW_CHUNK
    width = npp * LANES
    C2 = 2 * RW_CHUNK
    nunit = npp * nchunk
    nblk = s // tb
    ngrp = d // width
    total = b * ngrp * nblk

    def split(kk):
        return kk // (ngrp * nblk), kk % nblk, (kk // nblk) % ngrp

    def tok_in(kq):
        return split(jnp.minimum(kq, total - 1))

    def tok_out(kq):
        return split(jnp.maximum(kq - 1, 0))

    tin = pl.BlockSpec((None, tb, width), tok_in)
    pin = pl.BlockSpec((1, width), lambda kq: (0, tok_in(kq)[2]))
    pout = pl.BlockSpec((1, width), lambda kq: (0, tok_out(kq)[2]))
    return pl.pallas_call(
        functools.partial(_rw_scan_kernel, nchunk=nchunk, npp=npp, nblk=nblk),
        grid=(total + 1,),
        in_specs=[tin, tin, tin, tin, tin, tin, pin, pin, pin, pout, pout],
        out_specs=pl.BlockSpec((None, tb, width), tok_out),
        out_shape=jax.ShapeDtypeStruct((b, s, d), BF16),
        scratch_shapes=[
            pltpu.VMEM((npp, C2, C2), F32),
            pltpu.VMEM((2 * nunit, C2, C2), BF16),
            pltpu.VMEM((2 * nunit, C2, C2), BF16),
            pltpu.VMEM((2 * nunit, C2, C2), F32),
            pltpu.VMEM((2 * nunit, C2, C2), F32),
            pltpu.VMEM((2 * nunit, 8, LANES), F32),
            pltpu.VMEM((2, tb, width), F32),
            pltpu.VMEM((2, tb, width), F32),
            pltpu.VMEM((tb, width), F32),
        ],
        compiler_params=_params(("arbitrary",)),
    )(r, k, v, lw, a, g, kkw, ka, rk, lng, lnb)


def _oproj_kernel(a_ref, w_ref, x_ref, mod_ref, ng_ref, o_ref):
    nsub = 2
    sub = a_ref.shape[0] // nsub
    for sb in range(nsub):
        rows = slice(sb * sub, (sb + 1) * sub)
        y = jnp.dot(a_ref[rows, :], w_ref[...], preferred_element_type=F32)
        o_ref[rows, :] = x_ref[rows, :] + mod_ref[2:3, :] * (_rms(y) * ng_ref[1:2, :])


def _oproj(a, w, x, mod, ng, tm):
    b, s, d = x.shape
    k = a.shape[-1]
    return pl.pallas_call(
        _oproj_kernel,
        grid=(b, s // tm),
        in_specs=[
            pl.BlockSpec((None, tm, k), lambda bi, i: (bi, i, 0)),
            pl.BlockSpec((k, d), lambda bi, i: (0, 0)),
            pl.BlockSpec((None, tm, d), lambda bi, i: (bi, i, 0)),
            pl.BlockSpec((None, 6, d), lambda bi, i: (bi, 0, 0)),
            pl.BlockSpec((4, d), lambda bi, i: (0, 0)),
        ],
        out_specs=pl.BlockSpec((None, tm, d), lambda bi, i: (bi, i, 0)),
        out_shape=jax.ShapeDtypeStruct((b, s, d), F32),
        compiler_params=_params(("arbitrary", "arbitrary")),
    )(a, w, x, mod, ng)


def _mlp_kernel(x_ref, mod_ref, ng_ref, up_hbm, dn_hbm, o_ref, h_ref, acc_ref, upb, dnb, sem,
                *, layer, nsub, tf, nf):
    step = pl.program_id(0) * pl.num_programs(1) + pl.program_id(1)
    last_step = pl.num_programs(0) * pl.num_programs(1) - 1
    sub = x_ref.shape[0] // nsub

    def copies(j, slot):
        return (pltpu.make_async_copy(up_hbm.at[layer, :, pl.ds(j * tf, tf)], upb.at[slot],
                                      sem.at[0, slot]),
                pltpu.make_async_copy(dn_hbm.at[layer, pl.ds(j * tf, tf), :], dnb.at[slot],
                                      sem.at[1, slot]))

    def start(j, slot):
        for cp in copies(j, slot):
            cp.start()

    def wait(j, slot):
        for cp in copies(j, slot):
            cp.wait()

    def ffn(h, slot):
        u = jnp.maximum(jnp.dot(h, upb[slot], preferred_element_type=F32), 0.0)
        return jnp.dot((u * u).astype(BF16), dnb[slot], preferred_element_type=F32)

    @pl.when(step == 0)
    def _():
        start(0, 0)

    wait(0, 0)
    start(1, 1)
    for sb in range(nsub):
        rows = slice(sb * sub, (sb + 1) * sub)
        h = _normmod(x_ref[rows, :], ng_ref[2:3, :], mod_ref[4:5, :], mod_ref[3:4, :])
        h = h.astype(BF16)
        h_ref[rows, :] = h
        acc_ref[rows, :] = ffn(h, 0)

    def pair(p, carry):
        j = 2 * p + 1
        wait(j, 1)
        start(j + 1, 0)
        acc_ref[...] += ffn(h_ref[...], 1)
        wait(j + 1, 0)
        start(j + 2, 1)
        acc_ref[...] += ffn(h_ref[...], 0)
        return carry

    lax.fori_loop(0, (nf - 2) // 2, pair, 0)

    wait(nf - 1, 1)

    @pl.when(step != last_step)
    def _():
        start(0, 0)

    for sb in range(nsub):
        rows = slice(sb * sub, (sb + 1) * sub)
        y = acc_ref[rows, :] + ffn(h_ref[rows, :], 1)
        o_ref[rows, :] = x_ref[rows, :] + mod_ref[5:6, :] * (_rms(y) * ng_ref[3:4, :])


def _mlp(x, mod, ng, up_all, dn_all, layer, tm, tf):
    b, s, d = x.shape
    f = up_all.shape[2]
    nf = f // tf
    assert nf >= 4 and nf % 2 == 0
    row = pl.BlockSpec((None, tm, d), lambda bi, i: (bi, i, 0))
    return pl.pallas_call(
        functools.partial(_mlp_kernel, layer=layer, nsub=2, tf=tf, nf=nf),
        grid=(b, s // tm),
        in_specs=[
            row,
            pl.BlockSpec((None, 6, d), lambda bi, i: (bi, 0, 0)),
            pl.BlockSpec((4, d), lambda bi, i: (0, 0)),
            pl.BlockSpec(memory_space=pl.ANY),
            pl.BlockSpec(memory_space=pl.ANY),
        ],
        out_specs=row,
        out_shape=jax.ShapeDtypeStruct((b, s, d), F32),
        scratch_shapes=[
            pltpu.VMEM((tm, d), BF16),
            pltpu.VMEM((tm, d), F32),
            pltpu.VMEM((2, d, tf), BF16),
            pltpu.VMEM((2, tf, d), BF16),
            pltpu.SemaphoreType.DMA((2, 2)),
        ],
        compiler_params=_params(("arbitrary", "arbitrary")),
    )(x, mod, ng, up_all, dn_all)


def _mla_proj_kernel(x_ref, mod_ref, ng_ref, cos_ref, sin_ref, kvg_ref, kdc_ref, kdr_ref, kdrr_ref,
                     kvn_ref, uk_ref, uv_ref, dq_ref, qn_ref, uqn_ref, uqr_ref, uqrr_ref,
                     qnope_ref, qrope_ref, knope_ref, krope_ref, v_ref, *, scale):
    x = x_ref[...]
    xn = _rms(x)
    cos = cos_ref[...]
    sin = sin_ref[...]

    hs = (xn * kvg_ref[...]).astype(BF16)
    ckv = _rms(jnp.dot(hs, kdc_ref[...], preferred_element_type=F32)) * kvn_ref[...]
    ckv = ckv.astype(BF16)
    knope_ref[...] = jnp.dot(ckv, uk_ref[...], preferred_element_type=F32).astype(knope_ref.dtype)
    v_ref[...] = jnp.dot(ckv, uv_ref[...], preferred_element_type=F32).astype(v_ref.dtype)
    kr = (jnp.dot(hs, kdr_ref[...], preferred_element_type=F32) * cos
          + jnp.dot(hs, kdrr_ref[...], preferred_element_type=F32) * sin)
    krope_ref[...] = kr.astype(krope_ref.dtype)

    h = (xn * ng_ref[0:1, :] * (1.0 + mod_ref[1:2, :]) + mod_ref[0:1, :]).astype(BF16)
    cq = _rms(jnp.dot(h, dq_ref[...], preferred_element_type=F32)) * qn_ref[...]
    cq = cq.astype(BF16)
    qnope = jnp.dot(cq, uqn_ref[...], preferred_element_type=F32) * scale
    qnope_ref[...] = qnope.astype(qnope_ref.dtype)
    qr = jnp.dot(cq, uqr_ref[...], preferred_element_type=F32)
    qrr = jnp.dot(cq, uqrr_ref[...], preferred_element_type=F32)
    reps = qr.shape[1] // LANES
    cos_w = jnp.concatenate([cos] * reps, axis=1)
    sin_w = jnp.concatenate([sin] * reps, axis=1)
    qrope_ref[...] = ((qr * cos_w + qrr * sin_w) * scale).astype(qrope_ref.dtype)


def _mla_proj(x, mod, ng, cos_t, sin_t, kvg, kdc, kdr, kdrr, kvn, uk, uv, dq, qn, uqn, uqr, uqrr,
              scale, tm):
    b, s, d = x.shape
    dr = uqr.shape[1]
    full = lambda arr: pl.BlockSpec(arr.shape, lambda bi, i: (0,) * arr.ndim)
    row = lambda w: pl.BlockSpec((None, tm, w), lambda bi, i: (bi, i, 0))
    return pl.pallas_call(
        functools.partial(_mla_proj_kernel, scale=scale),
        grid=(b, s // tm),
        in_specs=[row(d), pl.BlockSpec((None, 6, d), lambda bi, i: (bi, 0, 0)), full(ng),
                  row(LANES), row(LANES), full(kvg), full(kdc), full(kdr), full(kdrr), full(kvn),
                  full(uk), full(uv), full(dq), full(qn), full(uqn), full(uqr), full(uqrr)],
        out_specs=[row(d), row(dr), row(d), row(LANES), row(d)],
        out_shape=[jax.ShapeDtypeStruct((b, s, d), BF16), jax.ShapeDtypeStruct((b, s, dr), BF16),
                   jax.ShapeDtypeStruct((b, s, d), BF16), jax.ShapeDtypeStruct((b, s, LANES), BF16),
                   jax.ShapeDtypeStruct((b, s, d), BF16)],
        compiler_params=_params(("arbitrary", "arbitrary")),
    )(x, mod, ng, cos_t, sin_t, kvg, kdc, kdr, kdrr, kvn, uk, uv, dq, qn, uqn, uqr, uqrr)


def _attn_kernel(qn_ref, qr_ref, kn_ref, kr_ref, v_ref, o_ref, *, tq):
    s = qn_ref.shape[0]
    neg = jnp.finfo(F32).min
    lane = lax.broadcasted_iota(jnp.int32, (1, LANES), 1)
    ri = lax.broadcasted_iota(jnp.int32, (tq, tq), 0)
    ci = lax.broadcasted_iota(jnp.int32, (tq, tq), 1)
    causal = ri >= ci
    kr = kr_ref[...]
    k_cat = [jnp.concatenate([kn_ref[:, hd * MLA_NOPE:(hd + 1) * MLA_NOPE], kr], axis=1)
             for hd in range(2)]

    def scores(hd, qi):
        rows = slice(qi * tq, (qi + 1) * tq)
        in_head = (lane >> 6) == hd
        q_rope = jnp.where(in_head, qr_ref[rows, :], jnp.zeros((), BF16))
        q_cat = jnp.concatenate([qn_ref[rows, hd * MLA_NOPE:(hd + 1) * MLA_NOPE], q_rope], axis=1)
        lo = qi * tq
        sd = jnp.where(causal, _dot_nt(q_cat, k_cat[hd][lo:lo + tq, :]), neg)
        sl = _dot_nt(q_cat, k_cat[hd][:lo, :]) if qi > 0 else None
        return sd, sl

    def finish(hd, qi, sd, sl):
        rows = slice(qi * tq, (qi + 1) * tq)
        cols = slice(hd * MLA_HEAD_V, (hd + 1) * MLA_HEAD_V)
        lo = qi * tq
        m = jnp.max(sd, axis=-1, keepdims=True)
        if sl is not None:
            m = jnp.maximum(m, jnp.max(sl, axis=-1, keepdims=True))
        pd = jnp.exp2(sd - m)
        den = jnp.sum(pd, axis=-1, keepdims=True)
        acc = _dot(pd, v_ref[lo:lo + tq, cols])
        if sl is not None:
            pl_ = jnp.exp2(sl - m)
            den = den + jnp.sum(pl_, axis=-1, keepdims=True)
            acc = acc + _dot(pl_, v_ref[:lo, cols])
        o_ref[rows, cols] = (acc / den).astype(o_ref.dtype)

    items = [(hd, qi) for qi in range(s // tq) for hd in range(2)]
    nxt = scores(*items[0])
    for idx, item in enumerate(items):
        cur = nxt
        if idx + 1 < len(items):
            nxt = scores(*items[idx + 1])
        finish(*item, *cur)


def _attention(qn, qr, kn, kr, v, tq):
    b, s, d = qn.shape
    npair = d // (2 * MLA_NOPE)
    wide = pl.BlockSpec((None, s, 2 * MLA_NOPE), lambda bi, hp: (bi, 0, hp))
    return pl.pallas_call(
        functools.partial(_attn_kernel, tq=tq),
        grid=(b, npair),
        in_specs=[wide, pl.BlockSpec((None, s, LANES), lambda bi, hp: (bi, 0, hp)), wide,
                  pl.BlockSpec((None, s, LANES), lambda bi, hp: (bi, 0, 0)), wide],
        out_specs=wide,
        out_shape=jax.ShapeDtypeStruct((b, s, d), BF16),
        compiler_params=_params(("arbitrary", "arbitrary")),
    )(qn, qr, kn, kr, v)


def _pad_cols(w, n):
    return jnp.pad(w, ((0, 0), (0, n - w.shape[1])))


def _pad_rows(w, n):
    return jnp.pad(w, ((0, n - w.shape[0]), (0, 0)))


def _rot_half_cols(w):
    k, n = w.shape
    half = MLA_ROPE // 2
    w3 = w.reshape(k, n // MLA_ROPE, MLA_ROPE)
    return jnp.concatenate([-w3[..., half:], w3[..., :half]], axis=-1).reshape(k, n)


def kernel(x, c, positions, ada_w, ada_b, norm_g, mlp_up, mlp_down, rw_mu, rw_rkv, rw_w0, rw_w1,
           rw_w2, rw_a0, rw_a1, rw_a2, rw_g1, rw_g2, rw_kk, rw_ka, rw_rk, rw_lnx, rw_o, mla_dq,
           mla_qnorm, mla_uq, mla_o, kv_in_g, kv_down, kv_norm, kv_uk, kv_uv):
    b, s, d = x.shape
    depth = ada_w.shape[0]
    n_rw = rw_mu.shape[0]
    kv_lora = kv_norm.shape[0]
    heads = d // MLA_HEAD_V
    assert d % (4 * LANES) == 0 and s % RW_CHUNK == 0

    tm = min(s, 512)
    tm_small = min(s, 256)
    tf = min(1024, mlp_up.shape[2] // 4)
    tb = min(s, 512)
    tq = min(s, 512)
    scan_pairs = 4

    up_all = _to_bf16(mlp_up)
    dn_all = _to_bf16(mlp_down)
    rkv_all = _to_bf16(rw_rkv)
    rwo_all = _to_bf16(rw_o)
    mlao_all = _to_bf16(mla_o)
    mod_all = _ada_mod(c, ada_w, ada_b).reshape(depth, b, 6, d)
    cos_t, sin_t = _rope_tables(positions)
    shared = None

    for l in range(depth):
        mod = mod_all[l]
        ng = norm_g[l]
        if l < n_rw:
            i = l
            lora = max(LANES, -(-rw_w1.shape[2] // LANES) * LANES)
            r, k, v, lw, a, g = _rw_in(
                x, mod, ng, rw_mu[i], rkv_all[i], rw_w0[i].reshape(1, d),
                _pad_cols(rw_w1[i], lora).astype(BF16), _pad_rows(rw_w2[i], lora).astype(BF16),
                rw_a0[i].reshape(1, d),
                _pad_cols(rw_a1[i], lora).astype(BF16), _pad_rows(rw_a2[i], lora).astype(BF16),
                rw_g1[i].astype(BF16), rw_g2[i].astype(BF16), tm_small)
            mixed = _rw_scan(r, k, v, lw, a, g, rw_kk[i].reshape(1, d), rw_ka[i].reshape(1, d),
                             rw_rk[i].reshape(1, d), rw_lnx[i, 0].reshape(1, d),
                             rw_lnx[i, 1].reshape(1, d), tb, scan_pairs)
            x = _oproj(mixed, rwo_all[i], x, mod, ng, tm)
        else:
            i = l - n_rw
            uq = mla_uq[i]
            q_lora = uq.shape[0]
            uqn = uq[:, :, :MLA_NOPE].reshape(q_lora, heads * MLA_NOPE)
            uqr = uq[:, :, MLA_NOPE:].reshape(q_lora, heads * MLA_ROPE)
            kdr = kv_down[:, kv_lora:]
            kdr2 = jnp.concatenate([kdr, kdr], axis=1)
            scale = float((MLA_NOPE + MLA_ROPE) ** -0.5) * math.log2(math.e)
            qn, qr, kn, kr, v = _mla_proj(
                x, mod, ng, cos_t, sin_t, kv_in_g.reshape(1, d),
                kv_down[:, :kv_lora].astype(BF16), kdr2.astype(BF16),
                _rot_half_cols(kdr2).astype(BF16), kv_norm.reshape(1, kv_lora),
                kv_uk.reshape(kv_lora, -1).astype(BF16), kv_uv.reshape(kv_lora, -1).astype(BF16),
                mla_dq[i].astype(BF16), mla_qnorm[i].reshape(1, q_lora), uqn.astype(BF16),
                uqr.astype(BF16), _rot_half_cols(uqr).astype(BF16), scale, tm_small)
            if shared is None:
                shared = (kn, kr, v)
            kn, kr, v = shared
            att = _attention(qn, qr, kn, kr, v, tq)
            x = _oproj(att, mlao_all[i], x, mod, ng, tm)
        x = _mlp(x, mod, ng, up_all, dn_all, l, tm, tf)
    return x
```

```python
import functools
import math

import jax
import jax.numpy as jnp
from jax import lax
from jax.experimental import pallas as pl
from jax.experimental.pallas import tpu as pltpu

F32 = jnp.float32
BF16 = jnp.bfloat16

LANES = 128
NORM_EPS = 1e-6
RW_HEAD = 64
RW_GN_EPS = RW_HEAD * 1e-5
RW_CHUNK = 64
MLA_HEAD_V = 128
MLA_NOPE = 128
MLA_ROPE = 64
ROPE_THETA = 10000.0
VMEM_LIMIT = 56 * 1024 * 1024


def _params(sem):
    return pltpu.CompilerParams(dimension_semantics=sem, vmem_limit_bytes=VMEM_LIMIT)


def _dot(a, b):
    return jnp.dot(a.astype(BF16), b.astype(BF16), preferred_element_type=F32)


def _dot_nt(a, b):
    return lax.dot_general(a.astype(BF16), b.astype(BF16), (((1,), (1,)), ((), ())),
                           preferred_element_type=F32)


def _rms(x):
    return x * lax.rsqrt(jnp.mean(x * x, axis=-1, keepdims=True) + NORM_EPS)


def _normmod(x, g, scale, shift):
    return _rms(x) * g * (1.0 + scale) + shift


def _sigmoid(x):
    return 1.0 / (1.0 + jnp.exp(-x))


def _sigmoid_t(x):
    return 0.5 * jnp.tanh(0.5 * x) + 0.5


def _split2(x):
    hi = x.astype(BF16)
    lo = (x - hi.astype(F32)).astype(BF16)
    return hi, lo


def _split3(x):
    hi = x.astype(BF16)
    r1 = x - hi.astype(F32)
    mid = r1.astype(BF16)
    lo = (r1 - mid.astype(F32)).astype(BF16)
    return hi, mid, lo


CAST_BLOCK_BYTES = 8 * 1024 * 1024


def _cast_kernel(x_ref, o_ref):
    o_ref[...] = x_ref[...].astype(o_ref.dtype)


def _to_bf16(w):
    shape = w.shape
    w2 = w.reshape(-1, shape[-1])
    r, c = w2.shape
    tr = min(r, max(16, CAST_BLOCK_BYTES // (4 * c)))
    if r % tr or tr % 16:
        return w.astype(BF16)
    out = pl.pallas_call(
        _cast_kernel,
        grid=(r // tr,),
        in_specs=[pl.BlockSpec((tr, c), lambda i: (i, 0))],
        out_specs=pl.BlockSpec((tr, c), lambda i: (i, 0)),
        out_shape=jax.ShapeDtypeStruct((r, c), BF16),
        compiler_params=_params(("arbitrary",)),
    )(w2)
    return out.reshape(shape)


def _cast_cols_kernel(x_ref, o_ref):
    tf = o_ref.shape[-1]
    for j in range(o_ref.shape[0]):
        o_ref[j] = x_ref[:, j * tf:(j + 1) * tf].astype(o_ref.dtype)


def _to_bf16_col_tiles(w, tf):
    nl, d, f = w.shape
    tr = min(d, max(16, CAST_BLOCK_BYTES // (4 * f)))
    assert d % tr == 0 and f % tf == 0
    return pl.pallas_call(
        _cast_cols_kernel,
        grid=(nl, d // tr),
        in_specs=[pl.BlockSpec((None, tr, f), lambda l, i: (l, i, 0))],
        out_specs=pl.BlockSpec((None, f // tf, tr, tf), lambda l, i: (l, 0, i, 0)),
        out_shape=jax.ShapeDtypeStruct((nl, f // tf, d, tf), BF16),
        compiler_params=_params(("arbitrary", "arbitrary")),
    )(w)


def _ada_kernel(c_ref, w_ref, b_ref, o_ref):
    c = c_ref[...]
    c_hi, c_lo = _split2(c * _sigmoid(c))
    w_hi, w_lo = _split2(w_ref[...])
    o_ref[...] = (jnp.dot(c_hi, w_hi, preferred_element_type=F32)
                  + jnp.dot(c_lo, w_hi, preferred_element_type=F32)
                  + jnp.dot(c_hi, w_lo, preferred_element_type=F32)) + b_ref[...]


def _ada_mod(c, ada_w, ada_b):
    depth, d, n = ada_w.shape
    b = c.shape[0]
    tn = 1536 if n % 1536 == 0 else n
    return pl.pallas_call(
        _ada_kernel,
        grid=(depth, n // tn),
        in_specs=[
            pl.BlockSpec((b, d), lambda l, j: (0, 0)),
            pl.BlockSpec((None, d, tn), lambda l, j: (l, 0, j)),
            pl.BlockSpec((None, 1, tn), lambda l, j: (l, 0, j)),
        ],
        out_specs=pl.BlockSpec((None, b, tn), lambda l, j: (l, 0, j)),
        out_shape=jax.ShapeDtypeStruct((depth, b, n), F32),
        compiler_params=_params(("arbitrary", "arbitrary")),
    )(c, ada_w, ada_b.reshape(depth, 1, n))


def _rope_kernel(pos_ref, invf_ref, cos_ref, sin_ref):
    ang = pos_ref[...] * invf_ref[...]
    cos_ref[...] = jnp.cos(ang)
    sin_ref[...] = jnp.sin(ang)


def _rope_tables(positions):
    b, s = positions.shape
    half = MLA_ROPE // 2
    per_row = LANES // half
    inv_freq = 1.0 / (ROPE_THETA ** (jnp.arange(0, MLA_ROPE, 2, dtype=F32) / MLA_ROPE))
    invf = jnp.tile(inv_freq, per_row).reshape(1, LANES)
    rows = s // per_row
    pos = jnp.repeat(positions.astype(F32).reshape(b, rows, per_row), half, axis=-1)
    tm = min(rows, 512)
    spec = pl.BlockSpec((None, tm, LANES), lambda bi, i: (bi, i, 0))
    cos, sin = pl.pallas_call(
        _rope_kernel,
        grid=(b, rows // tm),
        in_specs=[spec, pl.BlockSpec((1, LANES), lambda bi, i: (0, 0))],
        out_specs=[spec, spec],
        out_shape=[jax.ShapeDtypeStruct((b, rows, LANES), F32)] * 2,
        compiler_params=_params(("arbitrary", "arbitrary")),
    )(pos, invf)
    widen = lambda t: jnp.tile(t.reshape(b, s, half), (1, 1, per_row))
    return widen(cos), widen(sin)


def _rw_in_kernel(x_ref, xp_ref, mod_ref, ng_ref, mu_ref, w_ref, w0_ref, w1_ref, w2_ref,
                  a0_ref, a1_ref, a2_ref, g1_ref, g2_ref,
                  r_ref, k_ref, v_ref, lw_ref, a_ref, g_ref, *, sub):
    i = pl.program_id(1)
    g = ng_ref[0:1, :]
    shift = mod_ref[0:1, :]
    scale = mod_ref[1:2, :]
    prev = jnp.where(i == 0, 0.0, _normmod(xp_ref[7:8, :], g, scale, shift))
    row = lax.broadcasted_iota(jnp.int32, (8, x_ref.shape[1]), 0)
    for sb in range(x_ref.shape[0] // sub):
        rows = slice(sb * sub, (sb + 1) * sub)
        h = _normmod(x_ref[rows, :], g, scale, shift)
        hs = pltpu.roll(h, 1, 0)
        hs = jnp.concatenate([jnp.where(row == 0, prev, hs[0:8, :]), hs[8:, :]], axis=0)
        prev = h[sub - 1:sub, :]
        xx = hs - h

        def mix(s):
            return (h + xx * mu_ref[s:s + 1, :]).astype(BF16)

        for s, o_ref in enumerate((r_ref, k_ref, v_ref)):
            o_ref[rows, :] = jnp.dot(mix(s), w_ref[s],
                                     preferred_element_type=F32).astype(o_ref.dtype)
        wpre = w0_ref[...] + _dot(jnp.tanh(_dot(mix(3), w1_ref[...])), w2_ref[...])
        lw_ref[rows, :] = (-math.exp(-0.5)) * _sigmoid_t(wpre)
        a_ref[rows, :] = _sigmoid_t(
            a0_ref[...] + _dot(_dot(mix(4), a1_ref[...]), a2_ref[...])).astype(a_ref.dtype)
        g_ref[rows, :] = _dot(_sigmoid_t(_dot(mix(5), g1_ref[...])),
                              g2_ref[...]).astype(g_ref.dtype)


def _rw_in(x, mod, ng, mu, w_rkv, w0, w1, w2, a0, a1, a2, g1, g2, tm):
    b, s, d = x.shape
    row = pl.BlockSpec((None, tm, d), lambda bi, i: (bi, i, 0))

    def resident(arr):
        return pl.BlockSpec(arr.shape, lambda bi, i: (0,) * arr.ndim,
                            pipeline_mode=pl.Buffered(1))

    return pl.pallas_call(
        functools.partial(_rw_in_kernel, sub=tm // 2),
        grid=(b, s // tm),
        in_specs=[
            row,
            pl.BlockSpec((None, 8, d), lambda bi, i: (bi, jnp.maximum(i * (tm // 8) - 1, 0), 0)),
            pl.BlockSpec((None, 6, d), lambda bi, i: (bi, 0, 0)),
            pl.BlockSpec((4, d), lambda bi, i: (0, 0)),
            pl.BlockSpec((6, d), lambda bi, i: (0, 0)),
            resident(w_rkv), resident(w0), resident(w1), resident(w2),
            resident(a0), resident(a1), resident(a2), resident(g1), resident(g2),
        ],
        out_specs=[row] * 6,
        out_shape=[jax.ShapeDtypeStruct((b, s, d), BF16)] * 3
        + [jax.ShapeDtypeStruct((b, s, d), F32)] + [jax.ShapeDtypeStruct((b, s, d), BF16)] * 2,
        compiler_params=_params(("arbitrary", "arbitrary")),
    )(x, x, mod, ng, mu, w_rkv, w0, w1, w2, a0, a1, a2, g1, g2)


def _stack2(x, m0, m1):
    return jnp.concatenate([x * m0, x * m1], axis=0)


def _rw_scan_kernel(r_ref, k_ref, v_ref, lw_ref, a_ref, g_ref, kkw_ref, ka_ref, rk_ref,
                    lng_ref, lnb_ref, o_ref, s_ref, q_ref, m_ref, bt_ref, yi_ref, gc_ref,
                    bonus_ref, gate_ref, y_ref, *, nchunk, npp, nblk):
    C = RW_CHUNK
    C2 = 2 * C
    nunit = npp * nchunk
    k_step = pl.program_id(0)
    slot_x = k_step % 2
    slot_y = 1 - slot_x
    base_x = slot_x * nunit
    base_y = slot_y * nunit
    first_block = (k_step + (nblk - 1)) % nblk == 0

    @pl.when(k_step == 0)
    def _():
        for ref in (s_ref, q_ref, m_ref, bt_ref, yi_ref, gc_ref, bonus_ref, gate_ref):
            ref[...] = jnp.zeros_like(ref)

    lane = lax.broadcasted_iota(jnp.int32, (1, LANES), 1)
    m0 = (lane < RW_HEAD).astype(F32)
    m1 = 1.0 - m0
    ri = lax.broadcasted_iota(jnp.int32, (C2, C2), 0)
    ci = lax.broadcasted_iota(jnp.int32, (C2, C2), 1)
    same_head = (ri >> 6) == (ci >> 6)
    strict = (ri & (C - 1)) > (ci & (C - 1))
    incl = (ri & (C - 1)) >= (ci & (C - 1))
    eye = (ri == ci).astype(F32)
    off_masks = []
    for j in range(C.bit_length() - 1):
        off_masks.append(((ri >> (j + 1)) == (ci >> (j + 1)))
                         & (((ri >> j) & 1) == 1) & (((ci >> j) & 1) == 0))
    block_ones = same_head.astype(BF16)
    tri = (incl[:C, :C]).astype(BF16)

    def head_sum(x):
        hi, lo = _split2(x)
        return (jnp.dot(hi, block_ones, preferred_element_type=F32)
                + jnp.dot(lo, block_ones, preferred_element_type=F32))

    def stack(x):
        return _stack2(x, m0, m1).astype(BF16)

    units = []

    def prep_inputs(pp):
        cols = slice(pp * LANES, (pp + 1) * LANES)
        K = k_ref[:, cols].astype(F32)
        A = a_ref[:, cols].astype(F32)
        LW = lw_ref[:, cols]
        kk0 = K * kkw_ref[:, cols]
        kk = kk0 / jnp.maximum(jnp.sqrt(head_sum(kk0 * kk0)), 1e-12)
        k2 = K * (1.0 + (A - 1.0) * ka_ref[:, cols])
        kka = kk * A
        bonus_ref[slot_x, :, cols] = (head_sum(r_ref[:, cols] * k2 * rk_ref[:, cols])
                                      * v_ref[:, cols])
        gate_ref[slot_x, :, cols] = g_ref[:, cols].astype(F32)
        cl_cat = None
        for part in _split3(LW):
            cat = jnp.concatenate([part[c * C:(c + 1) * C, :] for c in range(nchunk)], axis=1)
            term = jnp.dot(tri, cat, preferred_element_type=F32)
            cl_cat = term if cl_cat is None else cl_cat + term
        for c in range(nchunk):
            rows = slice(c * C, (c + 1) * C)
            cl = cl_cat[:, c * LANES:(c + 1) * LANES]
            cl_end = cl[C - 1:C, :]
            gam_inv = jnp.exp(-cl)
            gam_tail = jnp.exp(cl_end - cl)
            units.append(dict(
                idx=pp * nchunk + c,
                al_s=stack(-kk[rows] * jnp.exp(cl - LW[rows])),
                rb_s=_stack2(r_ref[rows, cols] * jnp.exp(cl), m0, m1),
                be_s=stack(kka[rows] * gam_inv),
                kb_s=stack(k2[rows] * gam_inv),
                bt_s=stack(kka[rows] * gam_tail),
                kt_s=stack(k2[rows] * gam_tail),
                v_s=stack(v_ref[rows, cols]),
                gc=jnp.exp(cl_end)))

    def stage_gram():
        for u in units:
            G = _dot_nt(jnp.concatenate([u["al_s"], u["rb_s"].astype(BF16)], axis=0),
                        jnp.concatenate([u["be_s"], u["kb_s"]], axis=0))
            u["Lb"] = jnp.where(strict, G[:C2, :C2], 0.0)
            u["Lk"] = jnp.where(strict, G[:C2, C2:], 0.0).astype(BF16)
            u["R"] = jnp.concatenate([jnp.where(incl, G[C2:, :C2], 0.0),
                                      jnp.where(incl, G[C2:, C2:], 0.0)], axis=1).astype(BF16)

    def stage_init():
        for u in units:
            u["lkv"] = _dot(u["Lk"], u["v_s"])
            u["T"] = eye + jnp.where(off_masks[0], u["Lb"], 0.0)

    def lower_rows(x, s):
        return jnp.concatenate([x[r:r + s, :] for r in range(s, C2, 2 * s)], axis=0)

    def stage_left(j):
        s = 1 << j
        for u in units:
            lo = jnp.where(off_masks[j], u["Lb"], 0.0)
            u["TL"] = _dot(lower_rows(u["T"], s) if s >= 8 else u["T"], lo)

    def stage_right(j):
        s = 1 << j
        for u in units:
            upd = _dot(u["TL"], u["T"])
            if s >= 8:
                pieces = []
                for m in range(C2 // (2 * s)):
                    pieces.append(u["T"][2 * m * s:(2 * m + 1) * s, :])
                    pieces.append(u["T"][(2 * m + 1) * s:(2 * m + 2) * s, :] + upd[m * s:(m + 1) * s, :])
                u["T"] = jnp.concatenate(pieces, axis=0)
            else:
                u["T"] = u["T"] + upd

    def stage_solve():
        for u in units:
            u["Z"] = _dot(u["T"], jnp.concatenate([u["al_s"], u["lkv"].astype(BF16)], axis=1))

    def stage_store():
        for u in units:
            X = jnp.concatenate(
                [u["Z"],
                 jnp.concatenate([jnp.zeros((C2, C2), F32), u["v_s"].astype(F32)], axis=1)],
                axis=0)
            QY = _dot(u["R"], X)
            MB = _dot(X.T, jnp.concatenate([u["bt_s"], u["kt_s"]], axis=0))
            i = base_x + u["idx"]
            q_ref[i] = (u["rb_s"] + QY[:, :C2]).astype(BF16)
            yi_ref[i] = QY[:, C2:]
            m_ref[i] = MB[:C2, :].astype(BF16)
            bt_ref[i] = MB[C2:, :]
            gc_ref[i] = jnp.broadcast_to(u["gc"], (8, LANES))

    prepare = [functools.partial(prep_inputs, pp) for pp in range(npp)]
    prepare += [stage_gram, stage_init]
    for j in range(1, len(off_masks)):
        prepare += [functools.partial(stage_left, j), functools.partial(stage_right, j)]
    prepare += [stage_solve, stage_store]

    S = [jnp.where(first_block, 0.0, s_ref[pp]) for pp in range(npp)]

    def state_step(c):
        for pp in range(npp):
            i = base_y + pp * nchunk + c
            Sb = S[pp].astype(BF16)
            ys = _dot_nt(q_ref[i], Sb) + yi_ref[i]
            y_ref[c * C:(c + 1) * C, pp * LANES:(pp + 1) * LANES] = ys[:C, :] + ys[C:, :]
            S[pp] = (S[pp] * gc_ref[i][0:1, :]
                     + jnp.dot(Sb, m_ref[i], preferred_element_type=F32) + bt_ref[i])

    def finish_out():
        inv_n = 1.0 / RW_HEAD
        for pp in range(npp):
            s_ref[pp] = S[pp]
            cols = slice(pp * LANES, (pp + 1) * LANES)
            y = y_ref[:, cols]
            mean = head_sum(y) * inv_n
            yc = y - mean
            var = head_sum(yc * yc) * inv_n
            yn = yc * lax.rsqrt(var + RW_GN_EPS) * lng_ref[:, cols] + lnb_ref[:, cols]
            o_ref[:, cols] = ((yn + bonus_ref[slot_y, :, cols])
                              * gate_ref[slot_y, :, cols]).astype(o_ref.dtype)

    finish = [functools.partial(state_step, c) for c in range(nchunk)] + [finish_out]

    done = 0
    for si, stage in enumerate(prepare):
        want = ((si + 1) * len(finish)) // len(prepare)
        while done < want:
            finish[done]()
            done += 1
        stage()


def _rw_scan(r, k, v, lw, a, g, kkw, ka, rk, lng, lnb, tb, npp):
    b, s, d = r.shape
    nchunk = tb // RW_CHUNK
    width = npp * LANES
    C2 = 2 * RW_CHUNK
    nunit = npp * nchunk
    nblk = s // tb
    ngrp = d // width
    total = b * ngrp * nblk

    def split(kk):
        return kk // (ngrp * nblk), kk % nblk, (kk // nblk) % ngrp

    def tok_in(kq):
        return split(jnp.minimum(kq, total - 1))

    def tok_out(kq):
        return split(jnp.maximum(kq - 1, 0))

    tin = pl.BlockSpec((None, tb, width), tok_in)
    pin = pl.BlockSpec((1, width), lambda kq: (0, tok_in(kq)[2]))
    pout = pl.BlockSpec((1, width), lambda kq: (0, tok_out(kq)[2]))
    return pl.pallas_call(
        functools.partial(_rw_scan_kernel, nchunk=nchunk, npp=npp, nblk=nblk),
        grid=(total + 1,),
        in_specs=[tin, tin, tin, tin, tin, tin, pin, pin, pin, pout, pout],
        out_specs=pl.BlockSpec((None, tb, width), tok_out),
        out_shape=jax.ShapeDtypeStruct((b, s, d), BF16),
        scratch_shapes=[
            pltpu.VMEM((npp, C2, C2), F32),
            pltpu.VMEM((2 * nunit, C2, C2), BF16),
            pltpu.VMEM((2 * nunit, C2, C2), BF16),
            pltpu.VMEM((2 * nunit, C2, C2), F32),
            pltpu.VMEM((2 * nunit, C2, C2), F32),
            pltpu.VMEM((2 * nunit, 8, LANES), F32),
            pltpu.VMEM((2, tb, width), F32),
            pltpu.VMEM((2, tb, width), F32),
            pltpu.VMEM((tb, width), F32),
        ],
        compiler_params=_params(("arbitrary",)),
    )(r, k, v, lw, a, g, kkw, ka, rk, lng, lnb)


def _oproj_kernel(a_ref, w_ref, x_ref, mod_ref, ng_ref, o_ref):
    nsub = 2
    sub = a_ref.shape[0] // nsub
    for sb in range(nsub):
        rows = slice(sb * sub, (sb + 1) * sub)
        y = jnp.dot(a_ref[rows, :], w_ref[...], preferred_element_type=F32)
        o_ref[rows, :] = x_ref[rows, :] + mod_ref[2:3, :] * (_rms(y) * ng_ref[1:2, :])


def _oproj(a, w, x, mod, ng, tm):
    b, s, d = x.shape
    k = a.shape[-1]
    return pl.pallas_call(
        _oproj_kernel,
        grid=(b, s // tm),
        in_specs=[
            pl.BlockSpec((None, tm, k), lambda bi, i: (bi, i, 0)),
            pl.BlockSpec((k, d), lambda bi, i: (0, 0)),
            pl.BlockSpec((None, tm, d), lambda bi, i: (bi, i, 0)),
            pl.BlockSpec((None, 6, d), lambda bi, i: (bi, 0, 0)),
            pl.BlockSpec((4, d), lambda bi, i: (0, 0)),
        ],
        out_specs=pl.BlockSpec((None, tm, d), lambda bi, i: (bi, i, 0)),
        out_shape=jax.ShapeDtypeStruct((b, s, d), F32),
        compiler_params=_params(("arbitrary", "arbitrary")),
    )(a, w, x, mod, ng)


def _mlp_kernel(x_ref, mod_ref, ng_ref, up_ref, dn_ref, o_ref, h_ref, acc_ref, *, nsub):
    j = pl.program_id(2)
    last = pl.num_programs(2) - 1
    sub = x_ref.shape[0] // nsub

    def ffn(h):
        u = jnp.maximum(jnp.dot(h, up_ref[...], preferred_element_type=F32), 0.0)
        return jnp.dot((u * u).astype(BF16), dn_ref[...], preferred_element_type=F32)

    @pl.when(j == 0)
    def _():
        for sb in range(nsub):
            rows = slice(sb * sub, (sb + 1) * sub)
            h = _normmod(x_ref[rows, :], ng_ref[2:3, :], mod_ref[4:5, :], mod_ref[3:4, :])
            h = h.astype(BF16)
            h_ref[rows, :] = h
            acc_ref[rows, :] = ffn(h)

    @pl.when(jnp.logical_and(j > 0, j < last))
    def _():
        acc_ref[...] += ffn(h_ref[...])

    @pl.when(j == last)
    def _():
        for sb in range(nsub):
            rows = slice(sb * sub, (sb + 1) * sub)
            y = acc_ref[rows, :] + ffn(h_ref[rows, :])
            o_ref[rows, :] = x_ref[rows, :] + mod_ref[5:6, :] * (_rms(y) * ng_ref[3:4, :])


def _mlp(x, mod, ng, up_all, dn_all, layer, tm):
    b, s, d = x.shape
    nf, _, tf = up_all.shape[1:]
    assert nf >= 2 and dn_all.shape[1] == nf * tf
    return pl.pallas_call(
        functools.partial(_mlp_kernel, nsub=2),
        grid=(b, s // tm, nf),
        in_specs=[
            pl.BlockSpec((None, tm, d), lambda bi, i, j: (bi, i, 0)),
            pl.BlockSpec((None, 6, d), lambda bi, i, j: (bi, 0, 0)),
            pl.BlockSpec((4, d), lambda bi, i, j: (0, 0)),
            pl.BlockSpec((None, None, d, tf), lambda bi, i, j: (layer, j, 0, 0)),
            pl.BlockSpec((None, tf, d), lambda bi, i, j: (layer, j, 0)),
        ],
        out_specs=pl.BlockSpec((None, tm, d), lambda bi, i, j: (bi, i, 0)),
        out_shape=jax.ShapeDtypeStruct((b, s, d), F32),
        scratch_shapes=[pltpu.VMEM((tm, d), BF16), pltpu.VMEM((tm, d), F32)],
        compiler_params=_params(("arbitrary", "arbitrary", "arbitrary")),
    )(x, mod, ng, up_all, dn_all)


def _mla_proj_kernel(x_ref, mod_ref, ng_ref, cos_ref, sin_ref, kvg_ref, kdc_ref, kdr_ref, kdrr_ref,
                     kvn_ref, uk_ref, uv_ref, dq_ref, qn_ref, uqn_ref, uqr_ref, uqrr_ref,
                     qnope_ref, qrope_ref, knope_ref, krope_ref, v_ref, *, scale):
    x = x_ref[...]
    xn = _rms(x)
    cos = cos_ref[...]
    sin = sin_ref[...]

    hs = (xn * kvg_ref[...]).astype(BF16)
    ckv = _rms(jnp.dot(hs, kdc_ref[...], preferred_element_type=F32)) * kvn_ref[...]
    ckv = ckv.astype(BF16)
    knope_ref[...] = jnp.dot(ckv, uk_ref[...], preferred_element_type=F32).astype(knope_ref.dtype)
    v_ref[...] = jnp.dot(ckv, uv_ref[...], preferred_element_type=F32).astype(v_ref.dtype)
    kr = (jnp.dot(hs, kdr_ref[...], preferred_element_type=F32) * cos
          + jnp.dot(hs, kdrr_ref[...], preferred_element_type=F32) * sin)
    krope_ref[...] = kr.astype(krope_ref.dtype)

    h = (xn * ng_ref[0:1, :] * (1.0 + mod_ref[1:2, :]) + mod_ref[0:1, :]).astype(BF16)
    cq = _rms(jnp.dot(h, dq_ref[...], preferred_element_type=F32)) * qn_ref[...]
    cq = cq.astype(BF16)
    qnope = jnp.dot(cq, uqn_ref[...], preferred_element_type=F32) * scale
    qnope_ref[...] = qnope.astype(qnope_ref.dtype)
    qr = jnp.dot(cq, uqr_ref[...], preferred_element_type=F32)
    qrr = jnp.dot(cq, uqrr_ref[...], preferred_element_type=F32)
    reps = qr.shape[1] // LANES
    cos_w = jnp.concatenate([cos] * reps, axis=1)
    sin_w = jnp.concatenate([sin] * reps, axis=1)
    qrope_ref[...] = ((qr * cos_w + qrr * sin_w) * scale).astype(qrope_ref.dtype)


def _mla_proj(x, mod, ng, cos_t, sin_t, kvg, kdc, kdr, kdrr, kvn, uk, uv, dq, qn, uqn, uqr, uqrr,
              scale, tm):
    b, s, d = x.shape
    dr = uqr.shape[1]
    full = lambda arr: pl.BlockSpec(arr.shape, lambda bi, i: (0,) * arr.ndim)
    row = lambda w: pl.BlockSpec((None, tm, w), lambda bi, i: (bi, i, 0))
    return pl.pallas_call(
        functools.partial(_mla_proj_kernel, scale=scale),
        grid=(b, s // tm),
        in_specs=[row(d), pl.BlockSpec((None, 6, d), lambda bi, i: (bi, 0, 0)), full(ng),
                  row(LANES), row(LANES), full(kvg), full(kdc), full(kdr), full(kdrr), full(kvn),
                  full(uk), full(uv), full(dq), full(qn), full(uqn), full(uqr), full(uqrr)],
        out_specs=[row(d), row(dr), row(d), row(LANES), row(d)],
        out_shape=[jax.ShapeDtypeStruct((b, s, d), BF16), jax.ShapeDtypeStruct((b, s, dr), BF16),
                   jax.ShapeDtypeStruct((b, s, d), BF16), jax.ShapeDtypeStruct((b, s, LANES), BF16),
                   jax.ShapeDtypeStruct((b, s, d), BF16)],
        compiler_params=_params(("arbitrary", "arbitrary")),
    )(x, mod, ng, cos_t, sin_t, kvg, kdc, kdr, kdrr, kvn, uk, uv, dq, qn, uqn, uqr, uqrr)


def _attn_kernel(qn_ref, qr_ref, kn_ref, kr_ref, v_ref, o_ref, *, tq):
    s = qn_ref.shape[0]
    neg = jnp.finfo(F32).min
    lane = lax.broadcasted_iota(jnp.int32, (1, LANES), 1)
    ri = lax.broadcasted_iota(jnp.int32, (tq, tq), 0)
    ci = lax.broadcasted_iota(jnp.int32, (tq, tq), 1)
    causal = ri >= ci
    kr = kr_ref[...]
    k_cat = [jnp.concatenate([kn_ref[:, hd * MLA_NOPE:(hd + 1) * MLA_NOPE], kr], axis=1)
             for hd in range(2)]

    def scores(hd, qi):
        rows = slice(qi * tq, (qi + 1) * tq)
        in_head = (lane >> 6) == hd
        q_rope = jnp.where(in_head, qr_ref[rows, :], jnp.zeros((), BF16))
        q_cat = jnp.concatenate([qn_ref[rows, hd * MLA_NOPE:(hd + 1) * MLA_NOPE], q_rope], axis=1)
        lo = qi * tq
        sd = jnp.where(causal, _dot_nt(q_cat, k_cat[hd][lo:lo + tq, :]), neg)
        sl = _dot_nt(q_cat, k_cat[hd][:lo, :]) if qi > 0 else None
        return sd, sl

    def finish(hd, qi, sd, sl):
        rows = slice(qi * tq, (qi + 1) * tq)
        cols = slice(hd * MLA_HEAD_V, (hd + 1) * MLA_HEAD_V)
        lo = qi * tq
        m = jnp.max(sd, axis=-1, keepdims=True)
        if sl is not None:
            m = jnp.maximum(m, jnp.max(sl, axis=-1, keepdims=True))
        pd = jnp.exp2(sd - m)
        den = jnp.sum(pd, axis=-1, keepdims=True)
        acc = _dot(pd, v_ref[lo:lo + tq, cols])
        if sl is not None:
            pl_ = jnp.exp2(sl - m)
            den = den + jnp.sum(pl_, axis=-1, keepdims=True)
            acc = acc + _dot(pl_, v_ref[:lo, cols])
        o_ref[rows, cols] = (acc / den).astype(o_ref.dtype)

    items = [(hd, qi) for qi in range(s // tq) for hd in range(2)]
    nxt = scores(*items[0])
    for idx, item in enumerate(items):
        cur = nxt
        if idx + 1 < len(items):
            nxt = scores(*items[idx + 1])
        finish(*item, *cur)


def _attention(qn, qr, kn, kr, v, tq):
    b, s, d = qn.shape
    npair = d // (2 * MLA_NOPE)
    wide = pl.BlockSpec((None, s, 2 * MLA_NOPE), lambda bi, hp: (bi, 0, hp))
    return pl.pallas_call(
        functools.partial(_attn_kernel, tq=tq),
        grid=(b, npair),
        in_specs=[wide, pl.BlockSpec((None, s, LANES), lambda bi, hp: (bi, 0, hp)), wide,
                  pl.BlockSpec((None, s, LANES), lambda bi, hp: (bi, 0, 0)), wide],
        out_specs=wide,
        out_shape=jax.ShapeDtypeStruct((b, s, d), BF16),
        compiler_params=_params(("arbitrary", "arbitrary")),
    )(qn, qr, kn, kr, v)


def _pad_cols(w, n):
    return jnp.pad(w, ((0, 0), (0, n - w.shape[1])))


def _pad_rows(w, n):
    return jnp.pad(w, ((0, n - w.shape[0]), (0, 0)))


def _rot_half_cols(w):
    k, n = w.shape
    half = MLA_ROPE // 2
    w3 = w.reshape(k, n // MLA_ROPE, MLA_ROPE)
    return jnp.concatenate([-w3[..., half:], w3[..., :half]], axis=-1).reshape(k, n)


def kernel(x, c, positions, ada_w, ada_b, norm_g, mlp_up, mlp_down, rw_mu, rw_rkv, rw_w0, rw_w1,
           rw_w2, rw_a0, rw_a1, rw_a2, rw_g1, rw_g2, rw_kk, rw_ka, rw_rk, rw_lnx, rw_o, mla_dq,
           mla_qnorm, mla_uq, mla_o, kv_in_g, kv_down, kv_norm, kv_uk, kv_uv):
    b, s, d = x.shape
    depth = ada_w.shape[0]
    n_rw = rw_mu.shape[0]
    kv_lora = kv_norm.shape[0]
    heads = d // MLA_HEAD_V
    assert d % (4 * LANES) == 0 and s % RW_CHUNK == 0

    tm = min(s, 512)
    tm_small = min(s, 256)
    tf = min(mlp_up.shape[2], 1024)
    tb = min(s, 512)
    tq = min(s, 512)
    scan_pairs = 4

    up_all = _to_bf16_col_tiles(mlp_up, tf)
    dn_all = _to_bf16(mlp_down)
    rkv_all = _to_bf16(rw_rkv)
    rwo_all = _to_bf16(rw_o)
    mlao_all = _to_bf16(mla_o)
    mod_all = _ada_mod(c, ada_w, ada_b).reshape(depth, b, 6, d)
    cos_t, sin_t = _rope_tables(positions)
    shared = None

    for l in range(depth):
        mod = mod_all[l]
        ng = norm_g[l]
        if l < n_rw:
            i = l
            lora = max(LANES, -(-rw_w1.shape[2] // LANES) * LANES)
            r, k, v, lw, a, g = _rw_in(
                x, mod, ng, rw_mu[i], rkv_all[i], rw_w0[i].reshape(1, d),
                _pad_cols(rw_w1[i], lora).astype(BF16), _pad_rows(rw_w2[i], lora).astype(BF16),
                rw_a0[i].reshape(1, d),
                _pad_cols(rw_a1[i], lora).astype(BF16), _pad_rows(rw_a2[i], lora).astype(BF16),
                rw_g1[i].astype(BF16), rw_g2[i].astype(BF16), tm_small)
            mixed = _rw_scan(r, k, v, lw, a, g, rw_kk[i].reshape(1, d), rw_ka[i].reshape(1, d),
                             rw_rk[i].reshape(1, d), rw_lnx[i, 0].reshape(1, d),
                             rw_lnx[i, 1].reshape(1, d), tb, scan_pairs)
            x = _oproj(mixed, rwo_all[i], x, mod, ng, tm)
        else:
            i = l - n_rw
            uq = mla_uq[i]
            q_lora = uq.shape[0]
            uqn = uq[:, :, :MLA_NOPE].reshape(q_lora, heads * MLA_NOPE)
            uqr = uq[:, :, MLA_NOPE:].reshape(q_lora, heads * MLA_ROPE)
            kdr = kv_down[:, kv_lora:]
            kdr2 = jnp.concatenate([kdr, kdr], axis=1)
            scale = float((MLA_NOPE + MLA_ROPE) ** -0.5) * math.log2(math.e)
            qn, qr, kn, kr, v = _mla_proj(
                x, mod, ng, cos_t, sin_t, kv_in_g.reshape(1, d),
                kv_down[:, :kv_lora].astype(BF16), kdr2.astype(BF16),
                _rot_half_cols(kdr2).astype(BF16), kv_norm.reshape(1, kv_lora),
                kv_uk.reshape(kv_lora, -1).astype(BF16), kv_uv.reshape(kv_lora, -1).astype(BF16),
                mla_dq[i].astype(BF16), mla_qnorm[i].reshape(1, q_lora), uqn.astype(BF16),
                uqr.astype(BF16), _rot_half_cols(uqr).astype(BF16), scale, tm_small)
            if shared is None:
                shared = (kn, kr, v)
            kn, kr, v = shared
            att = _attention(qn, qr, kn, kr, v, tq)
            x = _oproj(att, mlao_all[i], x, mod, ng, tm)
        x = _mlp(x, mod, ng, up_all, dn_all, l, tm)
    return x
```

```python
import functools
import math

import jax
import jax.numpy as jnp
from jax import lax
from jax.experimental import pallas as pl
from jax.experimental.pallas import tpu as pltpu

F32 = jnp.float32
BF16 = jnp.bfloat16

LANES = 128
NORM_EPS = 1e-6
RW_HEAD = 64
RW_GN_EPS = RW_HEAD * 1e-5
RW_CHUNK = 64
MLA_HEAD_V = 128
MLA_NOPE = 128
MLA_ROPE = 64
ROPE_THETA = 10000.0
VMEM_LIMIT = 56 * 1024 * 1024


def _params(sem):
    return pltpu.CompilerParams(dimension_semantics=sem, vmem_limit_bytes=VMEM_LIMIT)


def _dot(a, b):
    return jnp.dot(a.astype(BF16), b.astype(BF16), preferred_element_type=F32)


def _dot_nt(a, b):
    return lax.dot_general(a.astype(BF16), b.astype(BF16), (((1,), (1,)), ((), ())),
                           preferred_element_type=F32)


def _rms(x):
    return x * lax.rsqrt(jnp.mean(x * x, axis=-1, keepdims=True) + NORM_EPS)


def _normmod(x, g, scale, shift):
    return _rms(x) * g * (1.0 + scale) + shift


def _sigmoid(x):
    return 1.0 / (1.0 + jnp.exp(-x))


def _sigmoid_t(x):
    return 0.5 * jnp.tanh(0.5 * x) + 0.5


def _split2(x):
    hi = x.astype(BF16)
    lo = (x - hi.astype(F32)).astype(BF16)
    return hi, lo


def _split3(x):
    hi = x.astype(BF16)
    r1 = x - hi.astype(F32)
    mid = r1.astype(BF16)
    lo = (r1 - mid.astype(F32)).astype(BF16)
    return hi, mid, lo


CAST_BLOCK_BYTES = 8 * 1024 * 1024


def _cast_kernel(x_ref, o_ref):
    o_ref[...] = x_ref[...].astype(o_ref.dtype)


def _to_bf16(w):
    shape = w.shape
    w2 = w.reshape(-1, shape[-1])
    r, c = w2.shape
    tr = min(r, max(16, CAST_BLOCK_BYTES // (4 * c)))
    if r % tr or tr % 16:
        return w.astype(BF16)
    out = pl.pallas_call(
        _cast_kernel,
        grid=(r // tr,),
        in_specs=[pl.BlockSpec((tr, c), lambda i: (i, 0))],
        out_specs=pl.BlockSpec((tr, c), lambda i: (i, 0)),
        out_shape=jax.ShapeDtypeStruct((r, c), BF16),
        compiler_params=_params(("arbitrary",)),
    )(w2)
    return out.reshape(shape)


def _ada_kernel(c_ref, w_ref, b_ref, o_ref):
    c = c_ref[...]
    c_hi, c_lo = _split2(c * _sigmoid(c))
    w_hi, w_lo = _split2(w_ref[...])
    o_ref[...] = (jnp.dot(c_hi, w_hi, preferred_element_type=F32)
                  + jnp.dot(c_lo, w_hi, preferred_element_type=F32)
                  + jnp.dot(c_hi, w_lo, preferred_element_type=F32)) + b_ref[...]


def _ada_mod(c, ada_w, ada_b):
    depth, d, n = ada_w.shape
    b = c.shape[0]
    tn = 1536 if n % 1536 == 0 else n
    return pl.pallas_call(
        _ada_kernel,
        grid=(depth, n // tn),
        in_specs=[
            pl.BlockSpec((b, d), lambda l, j: (0, 0)),
            pl.BlockSpec((None, d, tn), lambda l, j: (l, 0, j)),
            pl.BlockSpec((None, 1, tn), lambda l, j: (l, 0, j)),
        ],
        out_specs=pl.BlockSpec((None, b, tn), lambda l, j: (l, 0, j)),
        out_shape=jax.ShapeDtypeStruct((depth, b, n), F32),
        compiler_params=_params(("arbitrary", "arbitrary")),
    )(c, ada_w, ada_b.reshape(depth, 1, n))


def _rw_in_kernel(x_ref, xp_ref, mod_ref, ng_ref, mu_ref, w_ref, w0_ref, w1_ref, w2_ref,
                  a0_ref, a1_ref, a2_ref, g1_ref, g2_ref,
                  r_ref, k_ref, v_ref, lw_ref, a_ref, g_ref, *, sub):
    i = pl.program_id(1)
    g = ng_ref[0:1, :]
    shift = mod_ref[0:1, :]
    scale = mod_ref[1:2, :]
    prev = jnp.where(i == 0, 0.0, _normmod(xp_ref[7:8, :], g, scale, shift))
    row = lax.broadcasted_iota(jnp.int32, (8, x_ref.shape[1]), 0)
    for sb in range(x_ref.shape[0] // sub):
        rows = slice(sb * sub, (sb + 1) * sub)
        h = _normmod(x_ref[rows, :], g, scale, shift)
        hs = pltpu.roll(h, 1, 0)
        hs = jnp.concatenate([jnp.where(row == 0, prev, hs[0:8, :]), hs[8:, :]], axis=0)
        prev = h[sub - 1:sub, :]
        xx = hs - h

        def mix(s):
            return (h + xx * mu_ref[s:s + 1, :]).astype(BF16)

        for s, o_ref in enumerate((r_ref, k_ref, v_ref)):
            o_ref[rows, :] = jnp.dot(mix(s), w_ref[s],
                                     preferred_element_type=F32).astype(o_ref.dtype)
        wpre = w0_ref[...] + _dot(jnp.tanh(_dot(mix(3), w1_ref[...])), w2_ref[...])
        lw_ref[rows, :] = (-math.exp(-0.5)) * _sigmoid_t(wpre)
        a_ref[rows, :] = _sigmoid_t(
            a0_ref[...] + _dot(_dot(mix(4), a1_ref[...]), a2_ref[...])).astype(a_ref.dtype)
        g_ref[rows, :] = _dot(_sigmoid_t(_dot(mix(5), g1_ref[...])),
                              g2_ref[...]).astype(g_ref.dtype)


def _rw_in(x, mod, ng, mu, w_rkv, w0, w1, w2, a0, a1, a2, g1, g2, tm):
    b, s, d = x.shape
    row = pl.BlockSpec((None, tm, d), lambda bi, i: (bi, i, 0))

    def resident(arr):
        return pl.BlockSpec(arr.shape, lambda bi, i: (0,) * arr.ndim,
                            pipeline_mode=pl.Buffered(1))

    return pl.pallas_call(
        functools.partial(_rw_in_kernel, sub=tm // 2),
        grid=(b, s // tm),
        in_specs=[
            row,
            pl.BlockSpec((None, 8, d), lambda bi, i: (bi, jnp.maximum(i * (tm // 8) - 1, 0), 0)),
            pl.BlockSpec((None, 6, d), lambda bi, i: (bi, 0, 0)),
            pl.BlockSpec((4, d), lambda bi, i: (0, 0)),
            pl.BlockSpec((6, d), lambda bi, i: (0, 0)),
            resident(w_rkv), resident(w0), resident(w1), resident(w2),
            resident(a0), resident(a1), resident(a2), resident(g1), resident(g2),
        ],
        out_specs=[row] * 6,
        out_shape=[jax.ShapeDtypeStruct((b, s, d), BF16)] * 3
        + [jax.ShapeDtypeStruct((b, s, d), F32)] + [jax.ShapeDtypeStruct((b, s, d), BF16)] * 2,
        compiler_params=_params(("arbitrary", "arbitrary")),
    )(x, x, mod, ng, mu, w_rkv, w0, w1, w2, a0, a1, a2, g1, g2)


def _stack2(x, m0, m1):
    return jnp.concatenate([x * m0, x * m1], axis=0)


def _rw_scan_kernel(r_ref, k_ref, v_ref, lw_ref, a_ref, g_ref, kkw_ref, ka_ref, rk_ref,
                    lng_ref, lnb_ref, o_ref, s_ref, q_ref, m_ref, bt_ref, yi_ref, gc_ref,
                    bonus_ref, gate_ref, y_ref, *, nchunk, npp, nblk):
    C = RW_CHUNK
    C2 = 2 * C
    nunit = npp * nchunk
    k_step = pl.program_id(0)
    slot_x = k_step % 2
    slot_y = 1 - slot_x
    base_x = slot_x * nunit
    base_y = slot_y * nunit
    first_block = (k_step + (nblk - 1)) % nblk == 0

    @pl.when(k_step == 0)
    def _():
        for ref in (s_ref, q_ref, m_ref, bt_ref, yi_ref, gc_ref, bonus_ref, gate_ref):
            ref[...] = jnp.zeros_like(ref)

    lane = lax.broadcasted_iota(jnp.int32, (1, LANES), 1)
    m0 = (lane < RW_HEAD).astype(F32)
    m1 = 1.0 - m0
    ri = lax.broadcasted_iota(jnp.int32, (C2, C2), 0)
    ci = lax.broadcasted_iota(jnp.int32, (C2, C2), 1)
    same_head = (ri >> 6) == (ci >> 6)
    strict = (ri & (C - 1)) > (ci & (C - 1))
    incl = (ri & (C - 1)) >= (ci & (C - 1))
    eye = (ri == ci).astype(F32)
    off_masks = []
    for j in range(C.bit_length() - 1):
        off_masks.append(((ri >> (j + 1)) == (ci >> (j + 1)))
                         & (((ri >> j) & 1) == 1) & (((ci >> j) & 1) == 0))
    block_ones = same_head.astype(BF16)
    tri = (incl[:C, :C]).astype(BF16)

    def head_sum(x):
        hi, lo = _split2(x)
        return (jnp.dot(hi, block_ones, preferred_element_type=F32)
                + jnp.dot(lo, block_ones, preferred_element_type=F32))

    def stack(x):
        return _stack2(x, m0, m1).astype(BF16)

    units = []

    def prep_inputs(pp):
        cols = slice(pp * LANES, (pp + 1) * LANES)
        K = k_ref[:, cols].astype(F32)
        A = a_ref[:, cols].astype(F32)
        LW = lw_ref[:, cols]
        kk0 = K * kkw_ref[:, cols]
        kk = kk0 / jnp.maximum(jnp.sqrt(head_sum(kk0 * kk0)), 1e-12)
        k2 = K * (1.0 + (A - 1.0) * ka_ref[:, cols])
        kka = kk * A
        bonus_ref[slot_x, :, cols] = (head_sum(r_ref[:, cols] * k2 * rk_ref[:, cols])
                                      * v_ref[:, cols])
        gate_ref[slot_x, :, cols] = g_ref[:, cols].astype(F32)
        cl_cat = None
        for part in _split3(LW):
            cat = jnp.concatenate([part[c * C:(c + 1) * C, :] for c in range(nchunk)], axis=1)
            term = jnp.dot(tri, cat, preferred_element_type=F32)
            cl_cat = term if cl_cat is None else cl_cat + term
        for c in range(nchunk):
            rows = slice(c * C, (c + 1) * C)
            cl = cl_cat[:, c * LANES:(c + 1) * LANES]
            cl_end = cl[C - 1:C, :]
            gam_inv = jnp.exp(-cl)
            gam_tail = jnp.exp(cl_end - cl)
            units.append(dict(
                idx=pp * nchunk + c,
                al_s=stack(-kk[rows] * jnp.exp(cl - LW[rows])),
                rb_s=_stack2(r_ref[rows, cols] * jnp.exp(cl), m0, m1),
                be_s=stack(kka[rows] * gam_inv),
                kb_s=stack(k2[rows] * gam_inv),
                bt_s=stack(kka[rows] * gam_tail),
                kt_s=stack(k2[rows] * gam_tail),
                v_s=stack(v_ref[rows, cols]),
                gc=jnp.exp(cl_end)))

    def stage_gram():
        for u in units:
            G = _dot_nt(jnp.concatenate([u["al_s"], u["rb_s"].astype(BF16)], axis=0),
                        jnp.concatenate([u["be_s"], u["kb_s"]], axis=0))
            u["Lb"] = jnp.where(strict, G[:C2, :C2], 0.0)
            u["Lk"] = jnp.where(strict, G[:C2, C2:], 0.0).astype(BF16)
            u["R"] = jnp.concatenate([jnp.where(incl, G[C2:, :C2], 0.0),
                                      jnp.where(incl, G[C2:, C2:], 0.0)], axis=1).astype(BF16)

    def stage_init():
        for u in units:
            u["lkv"] = _dot(u["Lk"], u["v_s"])
            u["T"] = eye + jnp.where(off_masks[0], u["Lb"], 0.0)

    def lower_rows(x, s):
        return jnp.concatenate([x[r:r + s, :] for r in range(s, C2, 2 * s)], axis=0)

    def stage_left(j):
        s = 1 << j
        for u in units:
            lo = jnp.where(off_masks[j], u["Lb"], 0.0)
            u["TL"] = _dot(lower_rows(u["T"], s) if s >= 8 else u["T"], lo)

    def stage_right(j):
        s = 1 << j
        for u in units:
            upd = _dot(u["TL"], u["T"])
            if s >= 8:
                pieces = []
                for m in range(C2 // (2 * s)):
                    pieces.append(u["T"][2 * m * s:(2 * m + 1) * s, :])
                    pieces.append(u["T"][(2 * m + 1) * s:(2 * m + 2) * s, :] + upd[m * s:(m + 1) * s, :])
                u["T"] = jnp.concatenate(pieces, axis=0)
            else:
                u["T"] = u["T"] + upd

    def stage_solve():
        for u in units:
            u["Z"] = _dot(u["T"], jnp.concatenate([u["al_s"], u["lkv"].astype(BF16)], axis=1))

    def stage_store():
        for u in units:
            X = jnp.concatenate(
                [u["Z"],
                 jnp.concatenate([jnp.zeros((C2, C2), F32), u["v_s"].astype(F32)], axis=1)],
                axis=0)
            QY = _dot(u["R"], X)
            MB = _dot(X.T, jnp.concatenate([u["bt_s"], u["kt_s"]], axis=0))
            i = base_x + u["idx"]
            q_ref[i] = (u["rb_s"] + QY[:, :C2]).astype(BF16)
            yi_ref[i] = QY[:, C2:]
            m_ref[i] = MB[:C2, :].astype(BF16)
            bt_ref[i] = MB[C2:, :]
            gc_ref[i] = jnp.broadcast_to(u["gc"], (8, LANES))

    prepare = [functools.partial(prep_inputs, pp) for pp in range(npp)]
    prepare += [stage_gram, stage_init]
    for j in range(1, len(off_masks)):
        prepare += [functools.partial(stage_left, j), functools.partial(stage_right, j)]
    prepare += [stage_solve, stage_store]

    S = [jnp.where(first_block, 0.0, s_ref[pp]) for pp in range(npp)]

    def state_step(c):
        for pp in range(npp):
            i = base_y + pp * nchunk + c
            Sb = S[pp].astype(BF16)
            ys = _dot_nt(q_ref[i], Sb) + yi_ref[i]
            y_ref[c * C:(c + 1) * C, pp * LANES:(pp + 1) * LANES] = ys[:C, :] + ys[C:, :]
            S[pp] = (S[pp] * gc_ref[i][0:1, :]
                     + jnp.dot(Sb, m_ref[i], preferred_element_type=F32) + bt_ref[i])

    def finish_out():
        inv_n = 1.0 / RW_HEAD
        for pp in range(npp):
            s_ref[pp] = S[pp]
            cols = slice(pp * LANES, (pp + 1) * LANES)
            y = y_ref[:, cols]
            mean = head_sum(y) * inv_n
            yc = y - mean
            var = head_sum(yc * yc) * inv_n
            yn = yc * lax.rsqrt(var + RW_GN_EPS) * lng_ref[:, cols] + lnb_ref[:, cols]
            o_ref[:, cols] = ((yn + bonus_ref[slot_y, :, cols])
                              * gate_ref[slot_y, :, cols]).astype(o_ref.dtype)

    finish = [functools.partial(state_step, c) for c in range(nchunk)] + [finish_out]

    done = 0
    for si, stage in enumerate(prepare):
        want = ((si + 1) * len(finish)) // len(prepare)
        while done < want:
            finish[done]()
            done += 1
        stage()


def _rw_scan(r, k, v, lw, a, g, kkw, ka, rk, lng, lnb, tb, npp):
    b, s, d = r.shape
    nchunk = tb // RW_CHUNK
    width = npp * LANES
    C2 = 2 * RW_CHUNK
    nunit = npp * nchunk
    nblk = s // tb
    ngrp = d // width
    total = b * ngrp * nblk

    def split(kk):
        return kk // (ngrp * nblk), kk % nblk, (kk // nblk) % ngrp

    def tok_in(kq):
        return split(jnp.minimum(kq, total - 1))

    def tok_out(kq):
        return split(jnp.maximum(kq - 1, 0))

    tin = pl.BlockSpec((None, tb, width), tok_in)
    pin = pl.BlockSpec((1, width), lambda kq: (0, tok_in(kq)[2]))
    pout = pl.BlockSpec((1, width), lambda kq: (0, tok_out(kq)[2]))
    return pl.pallas_call(
        functools.partial(_rw_scan_kernel, nchunk=nchunk, npp=npp, nblk=nblk),
        grid=(total + 1,),
        in_specs=[tin, tin, tin, tin, tin, tin, pin, pin, pin, pout, pout],
        out_specs=pl.BlockSpec((None, tb, width), tok_out),
        out_shape=jax.ShapeDtypeStruct((b, s, d), BF16),
        scratch_shapes=[
            pltpu.VMEM((npp, C2, C2), F32),
            pltpu.VMEM((2 * nunit, C2, C2), BF16),
            pltpu.VMEM((2 * nunit, C2, C2), BF16),
            pltpu.VMEM((2 * nunit, C2, C2), F32),
            pltpu.VMEM((2 * nunit, C2, C2), F32),
            pltpu.VMEM((2 * nunit, 8, LANES), F32),
            pltpu.VMEM((2, tb, width), F32),
            pltpu.VMEM((2, tb, width), F32),
            pltpu.VMEM((tb, width), F32),
        ],
        compiler_params=_params(("arbitrary",)),
    )(r, k, v, lw, a, g, kkw, ka, rk, lng, lnb)


def _oproj_kernel(a_ref, w_ref, x_ref, mod_ref, ng_ref, o_ref):
    nsub = 2
    sub = a_ref.shape[0] // nsub
    for sb in range(nsub):
        rows = slice(sb * sub, (sb + 1) * sub)
        y = jnp.dot(a_ref[rows, :], w_ref[...], preferred_element_type=F32)
        o_ref[rows, :] = x_ref[rows, :] + mod_ref[2:3, :] * (_rms(y) * ng_ref[1:2, :])


def _oproj(a, w, x, mod, ng, tm):
    b, s, d = x.shape
    k = a.shape[-1]
    return pl.pallas_call(
        _oproj_kernel,
        grid=(b, s // tm),
        in_specs=[
            pl.BlockSpec((None, tm, k), lambda bi, i: (bi, i, 0)),
            pl.BlockSpec((k, d), lambda bi, i: (0, 0)),
            pl.BlockSpec((None, tm, d), lambda bi, i: (bi, i, 0)),
            pl.BlockSpec((None, 6, d), lambda bi, i: (bi, 0, 0)),
            pl.BlockSpec((4, d), lambda bi, i: (0, 0)),
        ],
        out_specs=pl.BlockSpec((None, tm, d), lambda bi, i: (bi, i, 0)),
        out_shape=jax.ShapeDtypeStruct((b, s, d), F32),
        compiler_params=_params(("arbitrary", "arbitrary")),
    )(a, w, x, mod, ng)


def _mlp_kernel(x_ref, mod_ref, ng_ref, up_ref, dn_ref, o_ref, h_ref, acc_ref, *, nsub):
    j = pl.program_id(2)
    last = pl.num_programs(2) - 1
    sub = x_ref.shape[0] // nsub

    def ffn(h):
        u = jnp.maximum(jnp.dot(h, up_ref[...], preferred_element_type=F32), 0.0)
        return jnp.dot((u * u).astype(BF16), dn_ref[...], preferred_element_type=F32)

    @pl.when(j == 0)
    def _():
        for sb in range(nsub):
            rows = slice(sb * sub, (sb + 1) * sub)
            h = _normmod(x_ref[rows, :], ng_ref[2:3, :], mod_ref[4:5, :], mod_ref[3:4, :])
            h = h.astype(BF16)
            h_ref[rows, :] = h
            acc_ref[rows, :] = ffn(h)

    @pl.when(jnp.logical_and(j > 0, j < last))
    def _():
        acc_ref[...] += ffn(h_ref[...])

    @pl.when(j == last)
    def _():
        for sb in range(nsub):
            rows = slice(sb * sub, (sb + 1) * sub)
            y = acc_ref[rows, :] + ffn(h_ref[rows, :])
            o_ref[rows, :] = x_ref[rows, :] + mod_ref[5:6, :] * (_rms(y) * ng_ref[3:4, :])


def _mlp(x, mod, ng, up_all, dn_all, layer, tm, tf):
    b, s, d = x.shape
    f = up_all.shape[2]
    assert f // tf >= 2
    return pl.pallas_call(
        functools.partial(_mlp_kernel, nsub=2),
        grid=(b, s // tm, f // tf),
        in_specs=[
            pl.BlockSpec((None, tm, d), lambda bi, i, j: (bi, i, 0)),
            pl.BlockSpec((None, 6, d), lambda bi, i, j: (bi, 0, 0)),
            pl.BlockSpec((4, d), lambda bi, i, j: (0, 0)),
            pl.BlockSpec((None, d, tf), lambda bi, i, j: (layer, 0, j)),
            pl.BlockSpec((None, tf, d), lambda bi, i, j: (layer, j, 0)),
        ],
        out_specs=pl.BlockSpec((None, tm, d), lambda bi, i, j: (bi, i, 0)),
        out_shape=jax.ShapeDtypeStruct((b, s, d), F32),
        scratch_shapes=[pltpu.VMEM((tm, d), BF16), pltpu.VMEM((tm, d), F32)],
        compiler_params=_params(("arbitrary", "arbitrary", "arbitrary")),
    )(x, mod, ng, up_all, dn_all)


def _mla_proj_kernel(x_ref, mod_ref, ng_ref, pos_ref, invf_ref, kvg_ref, kdc_ref, kdr_ref, kdrr_ref,
                     kvn_ref, uk_ref, uv_ref, dq_ref, qn_ref, uqn_ref, uqr_ref, uqrr_ref,
                     qnope_ref, qrope_ref, knope_ref, krope_ref, v_ref, *, scale):
    x = x_ref[...]
    xn = _rms(x)
    ang = pos_ref[...] * invf_ref[...]
    cos = jnp.cos(ang)
    sin = jnp.sin(ang)

    hs = (xn * kvg_ref[...]).astype(BF16)
    ckv = _rms(jnp.dot(hs, kdc_ref[...], preferred_element_type=F32)) * kvn_ref[...]
    ckv = ckv.astype(BF16)
    knope_ref[...] = jnp.dot(ckv, uk_ref[...], preferred_element_type=F32).astype(knope_ref.dtype)
    v_ref[...] = jnp.dot(ckv, uv_ref[...], preferred_element_type=F32).astype(v_ref.dtype)
    kr = (jnp.dot(hs, kdr_ref[...], preferred_element_type=F32) * cos
          + jnp.dot(hs, kdrr_ref[...], preferred_element_type=F32) * sin)
    krope_ref[...] = kr.astype(krope_ref.dtype)

    h = (xn * ng_ref[0:1, :] * (1.0 + mod_ref[1:2, :]) + mod_ref[0:1, :]).astype(BF16)
    cq = _rms(jnp.dot(h, dq_ref[...], preferred_element_type=F32)) * qn_ref[...]
    cq = cq.astype(BF16)
    qnope = jnp.dot(cq, uqn_ref[...], preferred_element_type=F32) * scale
    qnope_ref[...] = qnope.astype(qnope_ref.dtype)
    qr = jnp.dot(cq, uqr_ref[...], preferred_element_type=F32)
    qrr = jnp.dot(cq, uqrr_ref[...], preferred_element_type=F32)
    reps = qr.shape[1] // LANES
    cos_w = jnp.concatenate([cos] * reps, axis=1)
    sin_w = jnp.concatenate([sin] * reps, axis=1)
    qrope_ref[...] = ((qr * cos_w + qrr * sin_w) * scale).astype(qrope_ref.dtype)


def _mla_proj(x, mod, ng, positions, kvg, kdc, kdr, kdrr, kvn, uk, uv, dq, qn, uqn, uqr, uqrr,
              scale, tm):
    b, s, d = x.shape
    dr = uqr.shape[1]
    half = MLA_ROPE // 2
    inv_freq = 1.0 / (ROPE_THETA ** (jnp.arange(0, MLA_ROPE, 2, dtype=F32) / MLA_ROPE))
    invf = jnp.tile(inv_freq, LANES // half).reshape(1, LANES)
    pos = positions.astype(F32).reshape(b, s, 1)
    full = lambda arr: pl.BlockSpec(arr.shape, lambda bi, i: (0,) * arr.ndim)
    row = lambda w: pl.BlockSpec((None, tm, w), lambda bi, i: (bi, i, 0))
    return pl.pallas_call(
        functools.partial(_mla_proj_kernel, scale=scale),
        grid=(b, s // tm),
        in_specs=[row(d), pl.BlockSpec((None, 6, d), lambda bi, i: (bi, 0, 0)), full(ng),
                  row(1), full(invf), full(kvg), full(kdc), full(kdr), full(kdrr), full(kvn),
                  full(uk), full(uv), full(dq), full(qn), full(uqn), full(uqr), full(uqrr)],
        out_specs=[row(d), row(dr), row(d), row(LANES), row(d)],
        out_shape=[jax.ShapeDtypeStruct((b, s, d), BF16), jax.ShapeDtypeStruct((b, s, dr), BF16),
                   jax.ShapeDtypeStruct((b, s, d), BF16), jax.ShapeDtypeStruct((b, s, LANES), BF16),
                   jax.ShapeDtypeStruct((b, s, d), BF16)],
        compiler_params=_params(("arbitrary", "arbitrary")),
    )(x, mod, ng, pos, invf, kvg, kdc, kdr, kdrr, kvn, uk, uv, dq, qn, uqn, uqr, uqrr)


def _attn_kernel(qn_ref, qr_ref, kn_ref, kr_ref, v_ref, o_ref, *, tq):
    s = qn_ref.shape[0]
    neg = jnp.finfo(F32).min
    lane = lax.broadcasted_iota(jnp.int32, (1, LANES), 1)
    ri = lax.broadcasted_iota(jnp.int32, (tq, tq), 0)
    ci = lax.broadcasted_iota(jnp.int32, (tq, tq), 1)
    causal = ri >= ci
    kr = kr_ref[...]
    k_cat = [jnp.concatenate([kn_ref[:, hd * MLA_NOPE:(hd + 1) * MLA_NOPE], kr], axis=1)
             for hd in range(2)]

    def scores(hd, qi):
        rows = slice(qi * tq, (qi + 1) * tq)
        in_head = (lane >> 6) == hd
        q_rope = jnp.where(in_head, qr_ref[rows, :], jnp.zeros((), BF16))
        q_cat = jnp.concatenate([qn_ref[rows, hd * MLA_NOPE:(hd + 1) * MLA_NOPE], q_rope], axis=1)
        lo = qi * tq
        sd = jnp.where(causal, _dot_nt(q_cat, k_cat[hd][lo:lo + tq, :]), neg)
        sl = _dot_nt(q_cat, k_cat[hd][:lo, :]) if qi > 0 else None
        return sd, sl

    def finish(hd, qi, sd, sl):
        rows = slice(qi * tq, (qi + 1) * tq)
        cols = slice(hd * MLA_HEAD_V, (hd + 1) * MLA_HEAD_V)
        lo = qi * tq
        m = jnp.max(sd, axis=-1, keepdims=True)
        if sl is not None:
            m = jnp.maximum(m, jnp.max(sl, axis=-1, keepdims=True))
        pd = jnp.exp2(sd - m)
        den = jnp.sum(pd, axis=-1, keepdims=True)
        acc = _dot(pd, v_ref[lo:lo + tq, cols])
        if sl is not None:
            pl_ = jnp.exp2(sl - m)
            den = den + jnp.sum(pl_, axis=-1, keepdims=True)
            acc = acc + _dot(pl_, v_ref[:lo, cols])
        o_ref[rows, cols] = (acc / den).astype(o_ref.dtype)

    items = [(hd, qi) for qi in range(s // tq) for hd in range(2)]
    nxt = scores(*items[0])
    for idx, item in enumerate(items):
        cur = nxt
        if idx + 1 < len(items):
            nxt = scores(*items[idx + 1])
        finish(*item, *cur)


def _attention(qn, qr, kn, kr, v, tq):
    b, s, d = qn.shape
    npair = d // (2 * MLA_NOPE)
    wide = pl.BlockSpec((None, s, 2 * MLA_NOPE), lambda bi, hp: (bi, 0, hp))
    return pl.pallas_call(
        functools.partial(_attn_kernel, tq=tq),
        grid=(b, npair),
        in_specs=[wide, pl.BlockSpec((None, s, LANES), lambda bi, hp: (bi, 0, hp)), wide,
                  pl.BlockSpec((None, s, LANES), lambda bi, hp: (bi, 0, 0)), wide],
        out_specs=wide,
        out_shape=jax.ShapeDtypeStruct((b, s, d), BF16),
        compiler_params=_params(("arbitrary", "arbitrary")),
    )(qn, qr, kn, kr, v)


def _pad_cols(w, n):
    return jnp.pad(w, ((0, 0), (0, n - w.shape[1])))


def _pad_rows(w, n):
    return jnp.pad(w, ((0, n - w.shape[0]), (0, 0)))


def _rot_half_cols(w):
    k, n = w.shape
    half = MLA_ROPE // 2
    w3 = w.reshape(k, n // MLA_ROPE, MLA_ROPE)
    return jnp.concatenate([-w3[..., half:], w3[..., :half]], axis=-1).reshape(k, n)


def kernel(x, c, positions, ada_w, ada_b, norm_g, mlp_up, mlp_down, rw_mu, rw_rkv, rw_w0, rw_w1,
           rw_w2, rw_a0, rw_a1, rw_a2, rw_g1, rw_g2, rw_kk, rw_ka, rw_rk, rw_lnx, rw_o, mla_dq,
           mla_qnorm, mla_uq, mla_o, kv_in_g, kv_down, kv_norm, kv_uk, kv_uv):
    b, s, d = x.shape
    depth = ada_w.shape[0]
    n_rw = rw_mu.shape[0]
    kv_lora = kv_norm.shape[0]
    heads = d // MLA_HEAD_V
    assert d % (4 * LANES) == 0 and s % RW_CHUNK == 0

    tm = min(s, 512)
    tm_small = min(s, 256)
    tf = min(mlp_up.shape[2], 1024)
    tb = min(s, 512)
    tq = min(s, 512)
    scan_pairs = 4

    up_all = _to_bf16(mlp_up)
    dn_all = _to_bf16(mlp_down)
    rkv_all = _to_bf16(rw_rkv)
    rwo_all = _to_bf16(rw_o)
    mlao_all = _to_bf16(mla_o)
    mod_all = _ada_mod(c, ada_w, ada_b).reshape(depth, b, 6, d)
    shared = None

    for l in range(depth):
        mod = mod_all[l]
        ng = norm_g[l]
        if l < n_rw:
            i = l
            lora = max(LANES, -(-rw_w1.shape[2] // LANES) * LANES)
            r, k, v, lw, a, g = _rw_in(
                x, mod, ng, rw_mu[i], rkv_all[i], rw_w0[i].reshape(1, d),
                _pad_cols(rw_w1[i], lora).astype(BF16), _pad_rows(rw_w2[i], lora).astype(BF16),
                rw_a0[i].reshape(1, d),
                _pad_cols(rw_a1[i], lora).astype(BF16), _pad_rows(rw_a2[i], lora).astype(BF16),
                rw_g1[i].astype(BF16), rw_g2[i].astype(BF16), tm_small)
            mixed = _rw_scan(r, k, v, lw, a, g, rw_kk[i].reshape(1, d), rw_ka[i].reshape(1, d),
                             rw_rk[i].reshape(1, d), rw_lnx[i, 0].reshape(1, d),
                             rw_lnx[i, 1].reshape(1, d), tb, scan_pairs)
            x = _oproj(mixed, rwo_all[i], x, mod, ng, tm)
        else:
            i = l - n_rw
            uq = mla_uq[i]
            q_lora = uq.shape[0]
            uqn = uq[:, :, :MLA_NOPE].reshape(q_lora, heads * MLA_NOPE)
            uqr = uq[:, :, MLA_NOPE:].reshape(q_lora, heads * MLA_ROPE)
            kdr = kv_down[:, kv_lora:]
            kdr2 = jnp.concatenate([kdr, kdr], axis=1)
            scale = float((MLA_NOPE + MLA_ROPE) ** -0.5) * math.log2(math.e)
            qn, qr, kn, kr, v = _mla_proj(
                x, mod, ng, positions, kv_in_g.reshape(1, d),
                kv_down[:, :kv_lora].astype(BF16), kdr2.astype(BF16),
                _rot_half_cols(kdr2).astype(BF16), kv_norm.reshape(1, kv_lora),
                kv_uk.reshape(kv_lora, -1).astype(BF16), kv_uv.reshape(kv_lora, -1).astype(BF16),
                mla_dq[i].astype(BF16), mla_qnorm[i].reshape(1, q_lora), uqn.astype(BF16),
                uqr.astype(BF16), _rot_half_cols(uqr).astype(BF16), scale, tm_small)
            if shared is None:
                shared = (kn, kr, v)
            kn, kr, v = shared
            att = _attention(qn, qr, kn, kr, v, tq)
            x = _oproj(att, mlao_all[i], x, mod, ng, tm)
        x = _mlp(x, mod, ng, up_all, dn_all, l, tm, tf)
    return x
```

```python
import functools
import math

import jax
import jax.numpy as jnp
from jax import lax
from jax.experimental import pallas as pl
from jax.experimental.pallas import tpu as pltpu

F32 = jnp.float32
BF16 = jnp.bfloat16

LANES = 128
NORM_EPS = 1e-6
RW_HEAD = 64
RW_GN_EPS = RW_HEAD * 1e-5
RW_CHUNK = 64
MLA_HEAD_V = 128
MLA_NOPE = 128
MLA_ROPE = 64
ROPE_THETA = 10000.0
VMEM_LIMIT = 56 * 1024 * 1024


def _params(sem):
    return pltpu.CompilerParams(dimension_semantics=sem, vmem_limit_bytes=VMEM_LIMIT)


def _dot(a, b):
    return jnp.dot(a.astype(BF16), b.astype(BF16), preferred_element_type=F32)


def _dot_nt(a, b):
    return lax.dot_general(a.astype(BF16), b.astype(BF16), (((1,), (1,)), ((), ())),
                           preferred_element_type=F32)


def _rms(x):
    return x * lax.rsqrt(jnp.mean(x * x, axis=-1, keepdims=True) + NORM_EPS)


def _normmod(x, g, scale, shift):
    return _rms(x) * g * (1.0 + scale) + shift


def _sigmoid(x):
    return 1.0 / (1.0 + jnp.exp(-x))


def _sigmoid_t(x):
    return 0.5 * jnp.tanh(0.5 * x) + 0.5


def _split2(x):
    hi = x.astype(BF16)
    lo = (x - hi.astype(F32)).astype(BF16)
    return hi, lo


def _split3(x):
    hi = x.astype(BF16)
    r1 = x - hi.astype(F32)
    mid = r1.astype(BF16)
    lo = (r1 - mid.astype(F32)).astype(BF16)
    return hi, mid, lo


CAST_BLOCK_BYTES = 8 * 1024 * 1024


def _cast_kernel(x_ref, o_ref):
    o_ref[...] = x_ref[...].astype(o_ref.dtype)


def _to_bf16(w):
    shape = w.shape
    w2 = w.reshape(-1, shape[-1])
    r, c = w2.shape
    tr = min(r, max(16, CAST_BLOCK_BYTES // (4 * c)))
    if r % tr or tr % 16:
        return w.astype(BF16)
    out = pl.pallas_call(
        _cast_kernel,
        grid=(r // tr,),
        in_specs=[pl.BlockSpec((tr, c), lambda i: (i, 0))],
        out_specs=pl.BlockSpec((tr, c), lambda i: (i, 0)),
        out_shape=jax.ShapeDtypeStruct((r, c), BF16),
        compiler_params=_params(("arbitrary",)),
    )(w2)
    return out.reshape(shape)


def _ada_kernel(c_ref, w_ref, b_ref, o_ref):
    c = c_ref[...]
    c_hi, c_lo = _split2(c * _sigmoid(c))
    w_hi, w_lo = _split2(w_ref[...])
    o_ref[...] = (jnp.dot(c_hi, w_hi, preferred_element_type=F32)
                  + jnp.dot(c_lo, w_hi, preferred_element_type=F32)
                  + jnp.dot(c_hi, w_lo, preferred_element_type=F32)) + b_ref[...]


def _ada_mod(c, ada_w, ada_b):
    depth, d, n = ada_w.shape
    b = c.shape[0]
    tn = 1536 if n % 1536 == 0 else n
    return pl.pallas_call(
        _ada_kernel,
        grid=(depth, n // tn),
        in_specs=[
            pl.BlockSpec((b, d), lambda l, j: (0, 0)),
            pl.BlockSpec((None, d, tn), lambda l, j: (l, 0, j)),
            pl.BlockSpec((None, 1, tn), lambda l, j: (l, 0, j)),
        ],
        out_specs=pl.BlockSpec((None, b, tn), lambda l, j: (l, 0, j)),
        out_shape=jax.ShapeDtypeStruct((depth, b, n), F32),
        compiler_params=_params(("arbitrary", "arbitrary")),
    )(c, ada_w, ada_b.reshape(depth, 1, n))


def _rw_in_kernel(x_ref, xp_ref, mod_ref, ng_ref, mu_ref, w_ref, w0_ref, w1_ref, w2_ref,
                  a0_ref, a1_ref, a2_ref, g1_ref, g2_ref,
                  r_ref, k_ref, v_ref, lw_ref, a_ref, g_ref, *, sub):
    i = pl.program_id(1)
    g = ng_ref[0:1, :]
    shift = mod_ref[0:1, :]
    scale = mod_ref[1:2, :]
    prev = jnp.where(i == 0, 0.0, _normmod(xp_ref[7:8, :], g, scale, shift))
    row = lax.broadcasted_iota(jnp.int32, (8, x_ref.shape[1]), 0)
    for sb in range(x_ref.shape[0] // sub):
        rows = slice(sb * sub, (sb + 1) * sub)
        h = _normmod(x_ref[rows, :], g, scale, shift)
        hs = pltpu.roll(h, 1, 0)
        hs = jnp.concatenate([jnp.where(row == 0, prev, hs[0:8, :]), hs[8:, :]], axis=0)
        prev = h[sub - 1:sub, :]
        xx = hs - h

        def mix(s):
            return (h + xx * mu_ref[s:s + 1, :]).astype(BF16)

        for s, o_ref in enumerate((r_ref, k_ref, v_ref)):
            o_ref[rows, :] = jnp.dot(mix(s), w_ref[s],
                                     preferred_element_type=F32).astype(o_ref.dtype)
        wpre = w0_ref[...] + _dot(jnp.tanh(_dot(mix(3), w1_ref[...])), w2_ref[...])
        lw_ref[rows, :] = (-math.exp(-0.5)) * _sigmoid_t(wpre)
        a_ref[rows, :] = _sigmoid_t(
            a0_ref[...] + _dot(_dot(mix(4), a1_ref[...]), a2_ref[...])).astype(a_ref.dtype)
        g_ref[rows, :] = _dot(_sigmoid_t(_dot(mix(5), g1_ref[...])),
                              g2_ref[...]).astype(g_ref.dtype)


def _rw_in(x, mod, ng, mu, w_rkv, w0, w1, w2, a0, a1, a2, g1, g2, tm):
    b, s, d = x.shape
    row = pl.BlockSpec((None, tm, d), lambda bi, i: (bi, i, 0))

    def resident(arr):
        return pl.BlockSpec(arr.shape, lambda bi, i: (0,) * arr.ndim,
                            pipeline_mode=pl.Buffered(1))

    return pl.pallas_call(
        functools.partial(_rw_in_kernel, sub=tm // 2),
        grid=(b, s // tm),
        in_specs=[
            row,
            pl.BlockSpec((None, 8, d), lambda bi, i: (bi, jnp.maximum(i * (tm // 8) - 1, 0), 0)),
            pl.BlockSpec((None, 6, d), lambda bi, i: (bi, 0, 0)),
            pl.BlockSpec((4, d), lambda bi, i: (0, 0)),
            pl.BlockSpec((6, d), lambda bi, i: (0, 0)),
            resident(w_rkv), resident(w0), resident(w1), resident(w2),
            resident(a0), resident(a1), resident(a2), resident(g1), resident(g2),
        ],
        out_specs=[row] * 6,
        out_shape=[jax.ShapeDtypeStruct((b, s, d), BF16)] * 3
        + [jax.ShapeDtypeStruct((b, s, d), F32)] + [jax.ShapeDtypeStruct((b, s, d), BF16)] * 2,
        compiler_params=_params(("arbitrary", "arbitrary")),
    )(x, x, mod, ng, mu, w_rkv, w0, w1, w2, a0, a1, a2, g1, g2)


def _stack2(x, m0, m1):
    return jnp.concatenate([x * m0, x * m1], axis=0)


def _rw_scan_kernel(r_ref, k_ref, v_ref, lw_ref, a_ref, g_ref, kkw_ref, ka_ref, rk_ref,
                    lng_ref, lnb_ref, o_ref, s_ref, q_ref, m_ref, bt_ref, yi_ref, gc_ref,
                    bonus_ref, gate_ref, y_ref, *, nchunk, npp, nblk):
    C = RW_CHUNK
    C2 = 2 * C
    nunit = npp * nchunk
    k_step = pl.program_id(0)
    slot_x = k_step % 2
    slot_y = 1 - slot_x
    base_x = slot_x * nunit
    base_y = slot_y * nunit
    first_block = (k_step + (nblk - 1)) % nblk == 0

    @pl.when(k_step == 0)
    def _():
        for ref in (s_ref, q_ref, m_ref, bt_ref, yi_ref, gc_ref, bonus_ref, gate_ref):
            ref[...] = jnp.zeros_like(ref)

    lane = lax.broadcasted_iota(jnp.int32, (1, LANES), 1)
    m0 = (lane < RW_HEAD).astype(F32)
    m1 = 1.0 - m0
    ri = lax.broadcasted_iota(jnp.int32, (C2, C2), 0)
    ci = lax.broadcasted_iota(jnp.int32, (C2, C2), 1)
    same_head = (ri >> 6) == (ci >> 6)
    strict = (ri & (C - 1)) > (ci & (C - 1))
    incl = (ri & (C - 1)) >= (ci & (C - 1))
    eye = (ri == ci).astype(F32)
    off_masks = []
    for j in range(C.bit_length() - 1):
        off_masks.append(((ri >> (j + 1)) == (ci >> (j + 1)))
                         & (((ri >> j) & 1) == 1) & (((ci >> j) & 1) == 0))
    block_ones = same_head.astype(BF16)
    tri = (incl[:C, :C]).astype(BF16)

    def head_sum(x):
        hi, lo = _split2(x)
        return (jnp.dot(hi, block_ones, preferred_element_type=F32)
                + jnp.dot(lo, block_ones, preferred_element_type=F32))

    def stack(x):
        return _stack2(x, m0, m1).astype(BF16)

    units = []

    def prep_inputs(pp):
        cols = slice(pp * LANES, (pp + 1) * LANES)
        K = k_ref[:, cols].astype(F32)
        A = a_ref[:, cols].astype(F32)
        LW = lw_ref[:, cols]
        kk0 = K * kkw_ref[:, cols]
        kk = kk0 / jnp.maximum(jnp.sqrt(head_sum(kk0 * kk0)), 1e-12)
        k2 = K * (1.0 + (A - 1.0) * ka_ref[:, cols])
        kka = kk * A
        bonus_ref[slot_x, :, cols] = (head_sum(r_ref[:, cols] * k2 * rk_ref[:, cols])
                                      * v_ref[:, cols])
        gate_ref[slot_x, :, cols] = g_ref[:, cols].astype(F32)
        cl_cat = None
        for part in _split3(LW):
            cat = jnp.concatenate([part[c * C:(c + 1) * C, :] for c in range(nchunk)], axis=1)
            term = jnp.dot(tri, cat, preferred_element_type=F32)
            cl_cat = term if cl_cat is None else cl_cat + term
        for c in range(nchunk):
            rows = slice(c * C, (c + 1) * C)
            cl = cl_cat[:, c * LANES:(c + 1) * LANES]
            cl_end = cl[C - 1:C, :]
            gam_inv = jnp.exp(-cl)
            units.append(dict(
                idx=pp * nchunk + c,
                al_s=stack(-kk[rows] * jnp.exp(cl - LW[rows])),
                rb_s=_stack2(r_ref[rows, cols] * jnp.exp(cl), m0, m1),
                be_s=stack(kka[rows] * gam_inv),
                kb_s=stack(k2[rows] * gam_inv),
                v_s=stack(v_ref[rows, cols]),
                gc=jnp.exp(cl_end)))

    def stage_gram():
        for u in units:
            G = _dot_nt(jnp.concatenate([u["al_s"], u["rb_s"].astype(BF16)], axis=0),
                        jnp.concatenate([u["be_s"], u["kb_s"]], axis=0))
            u["Lb"] = jnp.where(strict, G[:C2, :C2], 0.0)
            u["Lk"] = jnp.where(strict, G[:C2, C2:], 0.0).astype(BF16)
            u["R"] = jnp.concatenate([jnp.where(incl, G[C2:, :C2], 0.0),
                                      jnp.where(incl, G[C2:, C2:], 0.0)], axis=1).astype(BF16)

    def stage_init():
        for u in units:
            u["lkv"] = _dot(u["Lk"], u["v_s"]).astype(BF16)
            u["T"] = eye + jnp.where(off_masks[0], u["Lb"], 0.0)

    def lower_rows(x, s):
        return jnp.concatenate([x[r:r + s, :] for r in range(s, C2, 2 * s)], axis=0)

    def stage_left(j):
        s = 1 << j
        for u in units:
            lo = jnp.where(off_masks[j], u["Lb"], 0.0)
            u["TL"] = _dot(lower_rows(u["T"], s) if s >= 8 else u["T"], lo).astype(BF16)

    def stage_right(j):
        s = 1 << j
        for u in units:
            upd = _dot(u["TL"], u["T"])
            if s >= 8:
                pieces = []
                for m in range(C2 // (2 * s)):
                    pieces.append(u["T"][2 * m * s:(2 * m + 1) * s, :])
                    pieces.append(u["T"][(2 * m + 1) * s:(2 * m + 2) * s, :] + upd[m * s:(m + 1) * s, :])
                u["T"] = jnp.concatenate(pieces, axis=0)
            else:
                u["T"] = u["T"] + upd

    def stage_solve():
        for u in units:
            u["Z"] = _dot(u["T"], jnp.concatenate([u["al_s"], u["lkv"]], axis=1))

    def stage_store():
        for u in units:
            X = jnp.concatenate(
                [u["Z"],
                 jnp.concatenate([jnp.zeros((C2, C2), F32), u["v_s"].astype(F32)], axis=1)],
                axis=0)
            QY = _dot(u["R"], X)
            MB = _dot(X.T, jnp.concatenate([u["be_s"], u["kb_s"]], axis=0)) * u["gc"]
            i = base_x + u["idx"]
            q_ref[i] = (u["rb_s"] + QY[:, :C2]).astype(BF16)
            yi_ref[i] = QY[:, C2:]
            m_ref[i] = MB[:C2, :].astype(BF16)
            bt_ref[i] = MB[C2:, :]
            gc_ref[i] = jnp.broadcast_to(u["gc"], (8, LANES))

    prepare = [functools.partial(prep_inputs, pp) for pp in range(npp)]
    prepare += [stage_gram, stage_init]
    for j in range(1, len(off_masks)):
        prepare += [functools.partial(stage_left, j), functools.partial(stage_right, j)]
    prepare += [stage_solve, stage_store]

    S = [jnp.where(first_block, 0.0, s_ref[pp]) for pp in range(npp)]

    def state_step(c):
        for pp in range(npp):
            i = base_y + pp * nchunk + c
            Sb = S[pp].astype(BF16)
            ys = _dot_nt(q_ref[i], Sb) + yi_ref[i]
            y_ref[c * C:(c + 1) * C, pp * LANES:(pp + 1) * LANES] = ys[:C, :] + ys[C:, :]
            S[pp] = (S[pp] * gc_ref[i][0:1, :]
                     + jnp.dot(Sb, m_ref[i], preferred_element_type=F32) + bt_ref[i])

    def finish_out():
        inv_n = 1.0 / RW_HEAD
        for pp in range(npp):
            s_ref[pp] = S[pp]
            cols = slice(pp * LANES, (pp + 1) * LANES)
            y = y_ref[:, cols]
            mean = head_sum(y) * inv_n
            yc = y - mean
            var = head_sum(yc * yc) * inv_n
            yn = yc * lax.rsqrt(var + RW_GN_EPS) * lng_ref[:, cols] + lnb_ref[:, cols]
            o_ref[:, cols] = ((yn + bonus_ref[slot_y, :, cols])
                              * gate_ref[slot_y, :, cols]).astype(o_ref.dtype)

    finish = [functools.partial(state_step, c) for c in range(nchunk)] + [finish_out]

    done = 0
    for si, stage in enumerate(prepare):
        want = ((si + 1) * len(finish)) // len(prepare)
        while done < want:
            finish[done]()
            done += 1
        stage()


def _rw_scan(r, k, v, lw, a, g, kkw, ka, rk, lng, lnb, tb, npp):
    b, s, d = r.shape
    nchunk = tb // RW_CHUNK
    width = npp * LANES
    C2 = 2 * RW_CHUNK
    nunit = npp * nchunk
    nblk = s // tb
    ngrp = d // width
    total = b * ngrp * nblk

    def split(kk):
        return kk // (ngrp * nblk), kk % nblk, (kk // nblk) % ngrp

    def tok_in(kq):
        return split(jnp.minimum(kq, total - 1))

    def tok_out(kq):
        return split(jnp.maximum(kq - 1, 0))

    tin = pl.BlockSpec((None, tb, width), tok_in)
    pin = pl.BlockSpec((1, width), lambda kq: (0, tok_in(kq)[2]))
    pout = pl.BlockSpec((1, width), lambda kq: (0, tok_out(kq)[2]))
    return pl.pallas_call(
        functools.partial(_rw_scan_kernel, nchunk=nchunk, npp=npp, nblk=nblk),
        grid=(total + 1,),
        in_specs=[tin, tin, tin, tin, tin, tin, pin, pin, pin, pout, pout],
        out_specs=pl.BlockSpec((None, tb, width), tok_out),
        out_shape=jax.ShapeDtypeStruct((b, s, d), BF16),
        scratch_shapes=[
            pltpu.VMEM((npp, C2, C2), F32),
            pltpu.VMEM((2 * nunit, C2, C2), BF16),
            pltpu.VMEM((2 * nunit, C2, C2), BF16),
            pltpu.VMEM((2 * nunit, C2, C2), F32),
            pltpu.VMEM((2 * nunit, C2, C2), F32),
            pltpu.VMEM((2 * nunit, 8, LANES), F32),
            pltpu.VMEM((2, tb, width), F32),
            pltpu.VMEM((2, tb, width), F32),
            pltpu.VMEM((tb, width), F32),
        ],
        compiler_params=_params(("arbitrary",)),
    )(r, k, v, lw, a, g, kkw, ka, rk, lng, lnb)


def _oproj_kernel(a_ref, w_ref, x_ref, mod_ref, ng_ref, o_ref):
    y = jnp.dot(a_ref[...], w_ref[...], preferred_element_type=F32)
    o_ref[...] = x_ref[...] + mod_ref[2:3, :] * (_rms(y) * ng_ref[1:2, :])


def _oproj(a, w, x, mod, ng, tm):
    b, s, d = x.shape
    k = a.shape[-1]
    return pl.pallas_call(
        _oproj_kernel,
        grid=(b, s // tm),
        in_specs=[
            pl.BlockSpec((None, tm, k), lambda bi, i: (bi, i, 0)),
            pl.BlockSpec((k, d), lambda bi, i: (0, 0)),
            pl.BlockSpec((None, tm, d), lambda bi, i: (bi, i, 0)),
            pl.BlockSpec((None, 6, d), lambda bi, i: (bi, 0, 0)),
            pl.BlockSpec((4, d), lambda bi, i: (0, 0)),
        ],
        out_specs=pl.BlockSpec((None, tm, d), lambda bi, i: (bi, i, 0)),
        out_shape=jax.ShapeDtypeStruct((b, s, d), F32),
        compiler_params=_params(("arbitrary", "arbitrary")),
    )(a, w, x, mod, ng)


def _mlp_kernel(x_ref, mod_ref, ng_ref, up_ref, dn_ref, o_ref, h_ref, acc_ref, *, nsub):
    j = pl.program_id(2)
    last = pl.num_programs(2) - 1
    sub = x_ref.shape[0] // nsub

    def ffn(h):
        u = jnp.maximum(jnp.dot(h, up_ref[...], preferred_element_type=F32), 0.0)
        return jnp.dot((u * u).astype(BF16), dn_ref[...], preferred_element_type=F32)

    @pl.when(j == 0)
    def _():
        for sb in range(nsub):
            rows = slice(sb * sub, (sb + 1) * sub)
            h = _normmod(x_ref[rows, :], ng_ref[2:3, :], mod_ref[4:5, :], mod_ref[3:4, :])
            h = h.astype(BF16)
            h_ref[rows, :] = h
            acc_ref[rows, :] = ffn(h)

    @pl.when(jnp.logical_and(j > 0, j < last))
    def _():
        acc_ref[...] += ffn(h_ref[...])

    @pl.when(j == last)
    def _():
        for sb in range(nsub):
            rows = slice(sb * sub, (sb + 1) * sub)
            y = acc_ref[rows, :] + ffn(h_ref[rows, :])
            o_ref[rows, :] = x_ref[rows, :] + mod_ref[5:6, :] * (_rms(y) * ng_ref[3:4, :])


def _mlp(x, mod, ng, up_all, dn_all, layer, tm, tf):
    b, s, d = x.shape
    f = up_all.shape[2]
    assert f // tf >= 2
    return pl.pallas_call(
        functools.partial(_mlp_kernel, nsub=2),
        grid=(b, s // tm, f // tf),
        in_specs=[
            pl.BlockSpec((None, tm, d), lambda bi, i, j: (bi, i, 0)),
            pl.BlockSpec((None, 6, d), lambda bi, i, j: (bi, 0, 0)),
            pl.BlockSpec((4, d), lambda bi, i, j: (0, 0)),
            pl.BlockSpec((None, d, tf), lambda bi, i, j: (layer, 0, j)),
            pl.BlockSpec((None, tf, d), lambda bi, i, j: (layer, j, 0)),
        ],
        out_specs=pl.BlockSpec((None, tm, d), lambda bi, i, j: (bi, i, 0)),
        out_shape=jax.ShapeDtypeStruct((b, s, d), F32),
        scratch_shapes=[pltpu.VMEM((tm, d), BF16), pltpu.VMEM((tm, d), F32)],
        compiler_params=_params(("arbitrary", "arbitrary", "arbitrary")),
    )(x, mod, ng, up_all, dn_all)


def _mla_proj_kernel(x_ref, mod_ref, ng_ref, pos_ref, invf_ref, kvg_ref, kdc_ref, kdr_ref, kdrr_ref,
                     kvn_ref, uk_ref, uv_ref, dq_ref, qn_ref, uqn_ref, uqr_ref, uqrr_ref,
                     qnope_ref, qrope_ref, knope_ref, krope_ref, v_ref, *, scale):
    x = x_ref[...]
    xn = _rms(x)
    ang = pos_ref[...] * invf_ref[...]
    cos = jnp.cos(ang)
    sin = jnp.sin(ang)

    hs = (xn * kvg_ref[...]).astype(BF16)
    ckv = _rms(jnp.dot(hs, kdc_ref[...], preferred_element_type=F32)) * kvn_ref[...]
    ckv = ckv.astype(BF16)
    knope_ref[...] = jnp.dot(ckv, uk_ref[...], preferred_element_type=F32).astype(knope_ref.dtype)
    v_ref[...] = jnp.dot(ckv, uv_ref[...], preferred_element_type=F32).astype(v_ref.dtype)
    kr = (jnp.dot(hs, kdr_ref[...], preferred_element_type=F32) * cos
          + jnp.dot(hs, kdrr_ref[...], preferred_element_type=F32) * sin)
    krope_ref[...] = kr.astype(krope_ref.dtype)

    h = (xn * ng_ref[0:1, :] * (1.0 + mod_ref[1:2, :]) + mod_ref[0:1, :]).astype(BF16)
    cq = _rms(jnp.dot(h, dq_ref[...], preferred_element_type=F32)) * qn_ref[...]
    cq = cq.astype(BF16)
    qnope = jnp.dot(cq, uqn_ref[...], preferred_element_type=F32) * scale
    qnope_ref[...] = qnope.astype(qnope_ref.dtype)
    qr = jnp.dot(cq, uqr_ref[...], preferred_element_type=F32)
    qrr = jnp.dot(cq, uqrr_ref[...], preferred_element_type=F32)
    reps = qr.shape[1] // LANES
    cos_w = jnp.concatenate([cos] * reps, axis=1)
    sin_w = jnp.concatenate([sin] * reps, axis=1)
    qrope_ref[...] = ((qr * cos_w + qrr * sin_w) * scale).astype(qrope_ref.dtype)


def _mla_proj(x, mod, ng, positions, kvg, kdc, kdr, kdrr, kvn, uk, uv, dq, qn, uqn, uqr, uqrr,
              scale, tm):
    b, s, d = x.shape
    dr = uqr.shape[1]
    half = MLA_ROPE // 2
    inv_freq = 1.0 / (ROPE_THETA ** (jnp.arange(0, MLA_ROPE, 2, dtype=F32) / MLA_ROPE))
    invf = jnp.tile(inv_freq, LANES // half).reshape(1, LANES)
    pos = positions.astype(F32).reshape(b, s, 1)
    full = lambda arr: pl.BlockSpec(arr.shape, lambda bi, i: (0,) * arr.ndim)
    row = lambda w: pl.BlockSpec((None, tm, w), lambda bi, i: (bi, i, 0))
    return pl.pallas_call(
        functools.partial(_mla_proj_kernel, scale=scale),
        grid=(b, s // tm),
        in_specs=[row(d), pl.BlockSpec((None, 6, d), lambda bi, i: (bi, 0, 0)), full(ng),
                  row(1), full(invf), full(kvg), full(kdc), full(kdr), full(kdrr), full(kvn),
                  full(uk), full(uv), full(dq), full(qn), full(uqn), full(uqr), full(uqrr)],
        out_specs=[row(d), row(dr), row(d), row(LANES), row(d)],
        out_shape=[jax.ShapeDtypeStruct((b, s, d), BF16), jax.ShapeDtypeStruct((b, s, dr), BF16),
                   jax.ShapeDtypeStruct((b, s, d), BF16), jax.ShapeDtypeStruct((b, s, LANES), BF16),
                   jax.ShapeDtypeStruct((b, s, d), BF16)],
        compiler_params=_params(("arbitrary", "arbitrary")),
    )(x, mod, ng, pos, invf, kvg, kdc, kdr, kdrr, kvn, uk, uv, dq, qn, uqn, uqr, uqrr)


def _attn_kernel(qn_ref, qr_ref, kn_ref, kr_ref, v_ref, o_ref, *, tq):
    s = qn_ref.shape[0]
    neg = jnp.finfo(F32).min
    lane = lax.broadcasted_iota(jnp.int32, (1, LANES), 1)
    ri = lax.broadcasted_iota(jnp.int32, (tq, tq), 0)
    ci = lax.broadcasted_iota(jnp.int32, (tq, tq), 1)
    causal = ri >= ci
    kr = kr_ref[...]
    k_cat = [jnp.concatenate([kn_ref[:, hd * MLA_NOPE:(hd + 1) * MLA_NOPE], kr], axis=1)
             for hd in range(2)]

    def scores(hd, qi):
        rows = slice(qi * tq, (qi + 1) * tq)
        in_head = (lane >> 6) == hd
        q_rope = jnp.where(in_head, qr_ref[rows, :], jnp.zeros((), BF16))
        q_cat = jnp.concatenate([qn_ref[rows, hd * MLA_NOPE:(hd + 1) * MLA_NOPE], q_rope], axis=1)
        lo = qi * tq
        sd = jnp.where(causal, _dot_nt(q_cat, k_cat[hd][lo:lo + tq, :]), neg)
        sl = _dot_nt(q_cat, k_cat[hd][:lo, :]) if qi > 0 else None
        return sd, sl

    def finish(hd, qi, sd, sl):
        rows = slice(qi * tq, (qi + 1) * tq)
        cols = slice(hd * MLA_HEAD_V, (hd + 1) * MLA_HEAD_V)
        lo = qi * tq
        m = jnp.max(sd, axis=-1, keepdims=True)
        if sl is not None:
            m = jnp.maximum(m, jnp.max(sl, axis=-1, keepdims=True))
        pd = jnp.exp2(sd - m)
        den = jnp.sum(pd, axis=-1, keepdims=True)
        acc = _dot(pd, v_ref[lo:lo + tq, cols])
        if sl is not None:
            pl_ = jnp.exp2(sl - m)
            den = den + jnp.sum(pl_, axis=-1, keepdims=True)
            acc = acc + _dot(pl_, v_ref[:lo, cols])
        o_ref[rows, cols] = (acc / den).astype(o_ref.dtype)

    items = [(hd, qi) for qi in range(s // tq) for hd in range(2)]
    nxt = scores(*items[0])
    for idx, item in enumerate(items):
        cur = nxt
        if idx + 1 < len(items):
            nxt = scores(*items[idx + 1])
        finish(*item, *cur)


def _attention(qn, qr, kn, kr, v, tq):
    b, s, d = qn.shape
    npair = d // (2 * MLA_NOPE)
    wide = pl.BlockSpec((None, s, 2 * MLA_NOPE), lambda bi, hp: (bi, 0, hp))
    return pl.pallas_call(
        functools.partial(_attn_kernel, tq=tq),
        grid=(b, npair),
        in_specs=[wide, pl.BlockSpec((None, s, LANES), lambda bi, hp: (bi, 0, hp)), wide,
                  pl.BlockSpec((None, s, LANES), lambda bi, hp: (bi, 0, 0)), wide],
        out_specs=wide,
        out_shape=jax.ShapeDtypeStruct((b, s, d), BF16),
        compiler_params=_params(("arbitrary", "arbitrary")),
    )(qn, qr, kn, kr, v)


def _pad_cols(w, n):
    return jnp.pad(w, ((0, 0), (0, n - w.shape[1])))


def _pad_rows(w, n):
    return jnp.pad(w, ((0, n - w.shape[0]), (0, 0)))


def _rot_half_cols(w):
    k, n = w.shape
    half = MLA_ROPE // 2
    w3 = w.reshape(k, n // MLA_ROPE, MLA_ROPE)
    return jnp.concatenate([-w3[..., half:], w3[..., :half]], axis=-1).reshape(k, n)


def kernel(x, c, positions, ada_w, ada_b, norm_g, mlp_up, mlp_down, rw_mu, rw_rkv, rw_w0, rw_w1,
           rw_w2, rw_a0, rw_a1, rw_a2, rw_g1, rw_g2, rw_kk, rw_ka, rw_rk, rw_lnx, rw_o, mla_dq,
           mla_qnorm, mla_uq, mla_o, kv_in_g, kv_down, kv_norm, kv_uk, kv_uv):
    b, s, d = x.shape
    depth = ada_w.shape[0]
    n_rw = rw_mu.shape[0]
    kv_lora = kv_norm.shape[0]
    heads = d // MLA_HEAD_V
    assert d % (4 * LANES) == 0 and s % RW_CHUNK == 0

    tm = min(s, 512)
    tm_small = min(s, 256)
    tf = min(mlp_up.shape[2], 1024)
    tb = min(s, 512)
    tq = min(s, 512)
    scan_pairs = 4

    up_all = _to_bf16(mlp_up)
    dn_all = _to_bf16(mlp_down)
    rkv_all = _to_bf16(rw_rkv)
    rwo_all = _to_bf16(rw_o)
    mlao_all = _to_bf16(mla_o)
    mod_all = _ada_mod(c, ada_w, ada_b).reshape(depth, b, 6, d)
    shared = None

    for l in range(depth):
        mod = mod_all[l]
        ng = norm_g[l]
        if l < n_rw:
            i = l
            lora = max(LANES, -(-rw_w1.shape[2] // LANES) * LANES)
            r, k, v, lw, a, g = _rw_in(
                x, mod, ng, rw_mu[i], rkv_all[i], rw_w0[i].reshape(1, d),
                _pad_cols(rw_w1[i], lora).astype(BF16), _pad_rows(rw_w2[i], lora).astype(BF16),
                rw_a0[i].reshape(1, d),
                _pad_cols(rw_a1[i], lora).astype(BF16), _pad_rows(rw_a2[i], lora).astype(BF16),
                rw_g1[i].astype(BF16), rw_g2[i].astype(BF16), tm_small)
            mixed = _rw_scan(r, k, v, lw, a, g, rw_kk[i].reshape(1, d), rw_ka[i].reshape(1, d),
                             rw_rk[i].reshape(1, d), rw_lnx[i, 0].reshape(1, d),
                             rw_lnx[i, 1].reshape(1, d), tb, scan_pairs)
            x = _oproj(mixed, rwo_all[i], x, mod, ng, tm)
        else:
            i = l - n_rw
            uq = mla_uq[i]
            q_lora = uq.shape[0]
            uqn = uq[:, :, :MLA_NOPE].reshape(q_lora, heads * MLA_NOPE)
            uqr = uq[:, :, MLA_NOPE:].reshape(q_lora, heads * MLA_ROPE)
            kdr = kv_down[:, kv_lora:]
            kdr2 = jnp.concatenate([kdr, kdr], axis=1)
            scale = float((MLA_NOPE + MLA_ROPE) ** -0.5) * math.log2(math.e)
            qn, qr, kn, kr, v = _mla_proj(
                x, mod, ng, positions, kv_in_g.reshape(1, d),
                kv_down[:, :kv_lora].astype(BF16), kdr2.astype(BF16),
                _rot_half_cols(kdr2).astype(BF16), kv_norm.reshape(1, kv_lora),
                kv_uk.reshape(kv_lora, -1).astype(BF16), kv_uv.reshape(kv_lora, -1).astype(BF16),
                mla_dq[i].astype(BF16), mla_qnorm[i].reshape(1, q_lora), uqn.astype(BF16),
                uqr.astype(BF16), _rot_half_cols(uqr).astype(BF16), scale, tm_small)
            if shared is None:
                shared = (kn, kr, v)
            kn, kr, v = shared
            att = _attention(qn, qr, kn, kr, v, tq)
            x = _oproj(att, mlao_all[i], x, mod, ng, tm)
        x = _mlp(x, mod, ng, up_all, dn_all, l, tm, tf)
    return x
```

```python
import functools
import math

import jax
import jax.numpy as jnp
from jax import lax
from jax.experimental import pallas as pl
from jax.experimental.pallas import tpu as pltpu

F32 = jnp.float32
BF16 = jnp.bfloat16

LANES = 128
NORM_EPS = 1e-6
RW_HEAD = 64
RW_GN_EPS = RW_HEAD * 1e-5
RW_CHUNK = 64
MLA_HEAD_V = 128
MLA_NOPE = 128
MLA_ROPE = 64
ROPE_THETA = 10000.0
VMEM_LIMIT = 56 * 1024 * 1024


def _params(sem):
    return pltpu.CompilerParams(dimension_semantics=sem, vmem_limit_bytes=VMEM_LIMIT)


def _dot(a, b):
    return jnp.dot(a.astype(BF16), b.astype(BF16), preferred_element_type=F32)


def _dot_nt(a, b):
    return lax.dot_general(a.astype(BF16), b.astype(BF16), (((1,), (1,)), ((), ())),
                           preferred_element_type=F32)


def _rms(x):
    return x * lax.rsqrt(jnp.mean(x * x, axis=-1, keepdims=True) + NORM_EPS)


def _normmod(x, g, scale, shift):
    return _rms(x) * g * (1.0 + scale) + shift


def _sigmoid(x):
    return 1.0 / (1.0 + jnp.exp(-x))


def _sigmoid_t(x):
    return 0.5 * jnp.tanh(0.5 * x) + 0.5


def _split2(x):
    hi = x.astype(BF16)
    lo = (x - hi.astype(F32)).astype(BF16)
    return hi, lo


def _split3(x):
    hi = x.astype(BF16)
    r1 = x - hi.astype(F32)
    mid = r1.astype(BF16)
    lo = (r1 - mid.astype(F32)).astype(BF16)
    return hi, mid, lo


CAST_BLOCK_BYTES = 8 * 1024 * 1024


def _cast_kernel(x_ref, o_ref):
    o_ref[...] = x_ref[...].astype(o_ref.dtype)


def _to_bf16(w):
    shape = w.shape
    w2 = w.reshape(-1, shape[-1])
    r, c = w2.shape
    tr = min(r, max(16, CAST_BLOCK_BYTES // (4 * c)))
    if r % tr or tr % 16:
        return w.astype(BF16)
    out = pl.pallas_call(
        _cast_kernel,
        grid=(r // tr,),
        in_specs=[pl.BlockSpec((tr, c), lambda i: (i, 0))],
        out_specs=pl.BlockSpec((tr, c), lambda i: (i, 0)),
        out_shape=jax.ShapeDtypeStruct((r, c), BF16),
        compiler_params=_params(("arbitrary",)),
    )(w2)
    return out.reshape(shape)


def _ada_kernel(c_ref, w_ref, b_ref, o_ref):
    c = c_ref[...]
    c_hi, c_lo = _split2(c * _sigmoid(c))
    w_hi, w_lo = _split2(w_ref[...])
    o_ref[...] = (jnp.dot(c_hi, w_hi, preferred_element_type=F32)
                  + jnp.dot(c_lo, w_hi, preferred_element_type=F32)
                  + jnp.dot(c_hi, w_lo, preferred_element_type=F32)) + b_ref[...]


def _ada_mod(c, ada_w, ada_b):
    depth, d, n = ada_w.shape
    b = c.shape[0]
    tn = 1536 if n % 1536 == 0 else n
    return pl.pallas_call(
        _ada_kernel,
        grid=(depth, n // tn),
        in_specs=[
            pl.BlockSpec((b, d), lambda l, j: (0, 0)),
            pl.BlockSpec((None, d, tn), lambda l, j: (l, 0, j)),
            pl.BlockSpec((None, 1, tn), lambda l, j: (l, 0, j)),
        ],
        out_specs=pl.BlockSpec((None, b, tn), lambda l, j: (l, 0, j)),
        out_shape=jax.ShapeDtypeStruct((depth, b, n), F32),
        compiler_params=_params(("arbitrary", "arbitrary")),
    )(c, ada_w, ada_b.reshape(depth, 1, n))


def _rw_in_kernel(x_ref, xp_ref, mod_ref, ng_ref, mu_ref, w_ref, w0_ref, w1_ref, w2_ref,
                  a0_ref, a1_ref, a2_ref, g1_ref, g2_ref,
                  r_ref, k_ref, v_ref, lw_ref, a_ref, g_ref, *, sub):
    i = pl.program_id(1)
    g = ng_ref[0:1, :]
    shift = mod_ref[0:1, :]
    scale = mod_ref[1:2, :]
    prev = jnp.where(i == 0, 0.0, _normmod(xp_ref[7:8, :], g, scale, shift))
    row = lax.broadcasted_iota(jnp.int32, (8, x_ref.shape[1]), 0)
    for sb in range(x_ref.shape[0] // sub):
        rows = slice(sb * sub, (sb + 1) * sub)
        h = _normmod(x_ref[rows, :], g, scale, shift)
        hs = pltpu.roll(h, 1, 0)
        hs = jnp.concatenate([jnp.where(row == 0, prev, hs[0:8, :]), hs[8:, :]], axis=0)
        prev = h[sub - 1:sub, :]
        xx = hs - h

        def mix(s):
            return (h + xx * mu_ref[s:s + 1, :]).astype(BF16)

        for s, o_ref in enumerate((r_ref, k_ref, v_ref)):
            o_ref[rows, :] = jnp.dot(mix(s), w_ref[s],
                                     preferred_element_type=F32).astype(o_ref.dtype)
        wpre = w0_ref[...] + _dot(jnp.tanh(_dot(mix(3), w1_ref[...])), w2_ref[...])
        lw_ref[rows, :] = (-math.exp(-0.5)) * _sigmoid_t(wpre)
        a_ref[rows, :] = _sigmoid_t(
            a0_ref[...] + _dot(_dot(mix(4), a1_ref[...]), a2_ref[...])).astype(a_ref.dtype)
        g_ref[rows, :] = _dot(_sigmoid_t(_dot(mix(5), g1_ref[...])),
                              g2_ref[...]).astype(g_ref.dtype)


def _rw_in(x, mod, ng, mu, w_rkv, w0, w1, w2, a0, a1, a2, g1, g2, tm):
    b, s, d = x.shape
    row = pl.BlockSpec((None, tm, d), lambda bi, i: (bi, i, 0))

    def resident(arr):
        return pl.BlockSpec(arr.shape, lambda bi, i: (0,) * arr.ndim,
                            pipeline_mode=pl.Buffered(1))

    return pl.pallas_call(
        functools.partial(_rw_in_kernel, sub=tm // 2),
        grid=(b, s // tm),
        in_specs=[
            row,
            pl.BlockSpec((None, 8, d), lambda bi, i: (bi, jnp.maximum(i * (tm // 8) - 1, 0), 0)),
            pl.BlockSpec((None, 6, d), lambda bi, i: (bi, 0, 0)),
            pl.BlockSpec((4, d), lambda bi, i: (0, 0)),
            pl.BlockSpec((6, d), lambda bi, i: (0, 0)),
            resident(w_rkv), resident(w0), resident(w1), resident(w2),
            resident(a0), resident(a1), resident(a2), resident(g1), resident(g2),
        ],
        out_specs=[row] * 6,
        out_shape=[jax.ShapeDtypeStruct((b, s, d), BF16)] * 3
        + [jax.ShapeDtypeStruct((b, s, d), F32)] + [jax.ShapeDtypeStruct((b, s, d), BF16)] * 2,
        compiler_params=_params(("arbitrary", "arbitrary")),
    )(x, x, mod, ng, mu, w_rkv, w0, w1, w2, a0, a1, a2, g1, g2)


def _stack2(x, m0, m1):
    return jnp.concatenate([x * m0, x * m1], axis=0)


def _rw_scan_kernel(*refs, nchunk, npp, nblk, ncast):
    (r_ref, k_ref, v_ref, lw_ref, a_ref, g_ref, kkw_ref, ka_ref, rk_ref, lng_ref,
     lnb_ref) = refs[:11]
    cast_in = refs[11:11 + ncast]
    o_ref = refs[11 + ncast]
    cast_out = refs[12 + ncast:12 + 2 * ncast]
    (s_ref, q_ref, m_ref, bt_ref, yi_ref, gc_ref, bonus_ref, gate_ref,
     y_ref) = refs[12 + 2 * ncast:]
    for src, dst in zip(cast_in, cast_out):
        dst[...] = src[...].astype(dst.dtype)

    C = RW_CHUNK
    C2 = 2 * C
    nunit = npp * nchunk
    k_step = pl.program_id(0)
    slot_x = k_step % 2
    slot_y = 1 - slot_x
    base_x = slot_x * nunit
    base_y = slot_y * nunit
    first_block = (k_step + (nblk - 1)) % nblk == 0

    @pl.when(k_step == 0)
    def _():
        for ref in (s_ref, q_ref, m_ref, bt_ref, yi_ref, gc_ref, bonus_ref, gate_ref):
            ref[...] = jnp.zeros_like(ref)

    lane = lax.broadcasted_iota(jnp.int32, (1, LANES), 1)
    m0 = (lane < RW_HEAD).astype(F32)
    m1 = 1.0 - m0
    ri = lax.broadcasted_iota(jnp.int32, (C2, C2), 0)
    ci = lax.broadcasted_iota(jnp.int32, (C2, C2), 1)
    same_head = (ri >> 6) == (ci >> 6)
    strict = (ri & (C - 1)) > (ci & (C - 1))
    incl = (ri & (C - 1)) >= (ci & (C - 1))
    eye = (ri == ci).astype(F32)
    off_masks = []
    for j in range(C.bit_length() - 1):
        off_masks.append(((ri >> (j + 1)) == (ci >> (j + 1)))
                         & (((ri >> j) & 1) == 1) & (((ci >> j) & 1) == 0))
    block_ones = same_head.astype(BF16)
    tri = (incl[:C, :C]).astype(BF16)

    def head_sum(x):
        hi, lo = _split2(x)
        return (jnp.dot(hi, block_ones, preferred_element_type=F32)
                + jnp.dot(lo, block_ones, preferred_element_type=F32))

    def stack(x):
        return _stack2(x, m0, m1).astype(BF16)

    units = []

    def prep_inputs(pp):
        cols = slice(pp * LANES, (pp + 1) * LANES)
        K = k_ref[:, cols].astype(F32)
        A = a_ref[:, cols].astype(F32)
        LW = lw_ref[:, cols]
        kk0 = K * kkw_ref[:, cols]
        kk = kk0 / jnp.maximum(jnp.sqrt(head_sum(kk0 * kk0)), 1e-12)
        k2 = K * (1.0 + (A - 1.0) * ka_ref[:, cols])
        kka = kk * A
        bonus_ref[slot_x, :, cols] = (head_sum(r_ref[:, cols] * k2 * rk_ref[:, cols])
                                      * v_ref[:, cols])
        gate_ref[slot_x, :, cols] = g_ref[:, cols].astype(F32)
        cl_cat = None
        for part in _split3(LW):
            cat = jnp.concatenate([part[c * C:(c + 1) * C, :] for c in range(nchunk)], axis=1)
            term = jnp.dot(tri, cat, preferred_element_type=F32)
            cl_cat = term if cl_cat is None else cl_cat + term
        for c in range(nchunk):
            rows = slice(c * C, (c + 1) * C)
            cl = cl_cat[:, c * LANES:(c + 1) * LANES]
            cl_end = cl[C - 1:C, :]
            gam_inv = jnp.exp(-cl)
            units.append(dict(
                idx=pp * nchunk + c,
                al_s=stack(-kk[rows] * jnp.exp(cl - LW[rows])),
                rb_s=_stack2(r_ref[rows, cols] * jnp.exp(cl), m0, m1),
                be_s=stack(kka[rows] * gam_inv),
                kb_s=stack(k2[rows] * gam_inv),
                v_s=stack(v_ref[rows, cols]),
                gc=jnp.exp(cl_end)))

    def stage_gram():
        for u in units:
            G = _dot_nt(jnp.concatenate([u["al_s"], u["rb_s"].astype(BF16)], axis=0),
                        jnp.concatenate([u["be_s"], u["kb_s"]], axis=0))
            u["Lb"] = jnp.where(strict, G[:C2, :C2], 0.0)
            u["Lk"] = jnp.where(strict, G[:C2, C2:], 0.0).astype(BF16)
            u["R"] = jnp.concatenate([jnp.where(incl, G[C2:, :C2], 0.0),
                                      jnp.where(incl, G[C2:, C2:], 0.0)], axis=1).astype(BF16)

    def stage_init():
        for u in units:
            u["lkv"] = _dot(u["Lk"], u["v_s"]).astype(BF16)
            u["T"] = eye + jnp.where(off_masks[0], u["Lb"], 0.0)

    def lower_rows(x, s):
        return jnp.concatenate([x[r:r + s, :] for r in range(s, C2, 2 * s)], axis=0)

    def stage_left(j):
        s = 1 << j
        for u in units:
            lo = jnp.where(off_masks[j], u["Lb"], 0.0)
            u["TL"] = _dot(lower_rows(u["T"], s) if s >= 8 else u["T"], lo).astype(BF16)

    def stage_right(j):
        s = 1 << j
        for u in units:
            upd = _dot(u["TL"], u["T"])
            if s >= 8:
                pieces = []
                for m in range(C2 // (2 * s)):
                    pieces.append(u["T"][2 * m * s:(2 * m + 1) * s, :])
                    pieces.append(u["T"][(2 * m + 1) * s:(2 * m + 2) * s, :] + upd[m * s:(m + 1) * s, :])
                u["T"] = jnp.concatenate(pieces, axis=0)
            else:
                u["T"] = u["T"] + upd

    def stage_solve():
        for u in units:
            u["Z"] = _dot(u["T"], jnp.concatenate([u["al_s"], u["lkv"]], axis=1))

    def stage_store():
        for u in units:
            X = jnp.concatenate(
                [u["Z"],
                 jnp.concatenate([jnp.zeros((C2, C2), F32), u["v_s"].astype(F32)], axis=1)],
                axis=0)
            QY = _dot(u["R"], X)
            MB = _dot(X.T, jnp.concatenate([u["be_s"], u["kb_s"]], axis=0)) * u["gc"]
            i = base_x + u["idx"]
            q_ref[i] = (u["rb_s"] + QY[:, :C2]).astype(BF16)
            yi_ref[i] = QY[:, C2:]
            m_ref[i] = MB[:C2, :].astype(BF16)
            bt_ref[i] = MB[C2:, :]
            gc_ref[i] = jnp.broadcast_to(u["gc"], (8, LANES))

    prepare = [functools.partial(prep_inputs, pp) for pp in range(npp)]
    prepare += [stage_gram, stage_init]
    for j in range(1, len(off_masks)):
        prepare += [functools.partial(stage_left, j), functools.partial(stage_right, j)]
    prepare += [stage_solve, stage_store]

    S = [jnp.where(first_block, 0.0, s_ref[pp]) for pp in range(npp)]

    def state_step(c):
        for pp in range(npp):
            i = base_y + pp * nchunk + c
            Sb = S[pp].astype(BF16)
            ys = _dot_nt(q_ref[i], Sb) + yi_ref[i]
            y_ref[c * C:(c + 1) * C, pp * LANES:(pp + 1) * LANES] = ys[:C, :] + ys[C:, :]
            S[pp] = (S[pp] * gc_ref[i][0:1, :]
                     + jnp.dot(Sb, m_ref[i], preferred_element_type=F32) + bt_ref[i])

    def finish_out():
        inv_n = 1.0 / RW_HEAD
        for pp in range(npp):
            s_ref[pp] = S[pp]
            cols = slice(pp * LANES, (pp + 1) * LANES)
            y = y_ref[:, cols]
            mean = head_sum(y) * inv_n
            yc = y - mean
            var = head_sum(yc * yc) * inv_n
            yn = yc * lax.rsqrt(var + RW_GN_EPS) * lng_ref[:, cols] + lnb_ref[:, cols]
            o_ref[:, cols] = ((yn + bonus_ref[slot_y, :, cols])
                              * gate_ref[slot_y, :, cols]).astype(o_ref.dtype)

    finish = [functools.partial(state_step, c) for c in range(nchunk)] + [finish_out]

    done = 0
    for si, stage in enumerate(prepare):
        want = ((si + 1) * len(finish)) // len(prepare)
        while done < want:
            finish[done]()
            done += 1
        stage()


def _rw_scan(r, k, v, lw, a, g, kkw, ka, rk, lng, lnb, tb, npp, weights=()):
    b, s, d = r.shape
    nchunk = tb // RW_CHUNK
    width = npp * LANES
    C2 = 2 * RW_CHUNK
    nunit = npp * nchunk
    nblk = s // tb
    ngrp = d // width
    total = b * ngrp * nblk

    def split(kk):
        return kk // (ngrp * nblk), kk % nblk, (kk // nblk) % ngrp

    def tok_in(kq):
        return split(jnp.minimum(kq, total - 1))

    def tok_out(kq):
        return split(jnp.maximum(kq - 1, 0))

    tin = pl.BlockSpec((None, tb, width), tok_in)
    pin = pl.BlockSpec((1, width), lambda kq: (0, tok_in(kq)[2]))
    pout = pl.BlockSpec((1, width), lambda kq: (0, tok_out(kq)[2]))
    flat = [w.reshape(-1, w.shape[-1]) for w in weights]
    for w2 in flat:
        assert w2.shape[0] % total == 0 and (w2.shape[0] // total) % 16 == 0
    cast_specs = [pl.BlockSpec((w2.shape[0] // total, w2.shape[1]),
                               lambda kq: (jnp.minimum(kq, total - 1), 0)) for w2 in flat]
    outs = pl.pallas_call(
        functools.partial(_rw_scan_kernel, nchunk=nchunk, npp=npp, nblk=nblk, ncast=len(flat)),
        grid=(total + 1,),
        in_specs=[tin, tin, tin, tin, tin, tin, pin, pin, pin, pout, pout] + cast_specs,
        out_specs=[pl.BlockSpec((None, tb, width), tok_out)] + cast_specs,
        out_shape=[jax.ShapeDtypeStruct((b, s, d), BF16)]
        + [jax.ShapeDtypeStruct(w2.shape, BF16) for w2 in flat],
        scratch_shapes=[
            pltpu.VMEM((npp, C2, C2), F32),
            pltpu.VMEM((2 * nunit, C2, C2), BF16),
            pltpu.VMEM((2 * nunit, C2, C2), BF16),
            pltpu.VMEM((2 * nunit, C2, C2), F32),
            pltpu.VMEM((2 * nunit, C2, C2), F32),
            pltpu.VMEM((2 * nunit, 8, LANES), F32),
            pltpu.VMEM((2, tb, width), F32),
            pltpu.VMEM((2, tb, width), F32),
            pltpu.VMEM((tb, width), F32),
        ],
        compiler_params=_params(("arbitrary",)),
    )(r, k, v, lw, a, g, kkw, ka, rk, lng, lnb, *flat)
    return outs[0], [o.reshape(w.shape) for o, w in zip(outs[1:], weights)]


def _oproj_kernel(a_ref, w_ref, x_ref, mod_ref, ng_ref, o_ref):
    y = jnp.dot(a_ref[...], w_ref[...], preferred_element_type=F32)
    o_ref[...] = x_ref[...] + mod_ref[2:3, :] * (_rms(y) * ng_ref[1:2, :])


def _oproj(a, w, x, mod, ng, tm):
    b, s, d = x.shape
    k = a.shape[-1]
    return pl.pallas_call(
        _oproj_kernel,
        grid=(b, s // tm),
        in_specs=[
            pl.BlockSpec((None, tm, k), lambda bi, i: (bi, i, 0)),
            pl.BlockSpec((k, d), lambda bi, i: (0, 0)),
            pl.BlockSpec((None, tm, d), lambda bi, i: (bi, i, 0)),
            pl.BlockSpec((None, 6, d), lambda bi, i: (bi, 0, 0)),
            pl.BlockSpec((4, d), lambda bi, i: (0, 0)),
        ],
        out_specs=pl.BlockSpec((None, tm, d), lambda bi, i: (bi, i, 0)),
        out_shape=jax.ShapeDtypeStruct((b, s, d), F32),
        compiler_params=_params(("arbitrary", "arbitrary")),
    )(a, w, x, mod, ng)


def _mlp_kernel(x_ref, mod_ref, ng_ref, up_ref, dn_ref, o_ref, h_ref, acc_ref, *, nsub):
    j = pl.program_id(2)
    last = pl.num_programs(2) - 1
    sub = x_ref.shape[0] // nsub

    def ffn(h):
        u = jnp.maximum(jnp.dot(h, up_ref[...], preferred_element_type=F32), 0.0)
        return jnp.dot((u * u).astype(BF16), dn_ref[...], preferred_element_type=F32)

    @pl.when(j == 0)
    def _():
        for sb in range(nsub):
            rows = slice(sb * sub, (sb + 1) * sub)
            h = _normmod(x_ref[rows, :], ng_ref[2:3, :], mod_ref[4:5, :], mod_ref[3:4, :])
            h = h.astype(BF16)
            h_ref[rows, :] = h
            acc_ref[rows, :] = ffn(h)

    @pl.when(jnp.logical_and(j > 0, j < last))
    def _():
        acc_ref[...] += ffn(h_ref[...])

    @pl.when(j == last)
    def _():
        for sb in range(nsub):
            rows = slice(sb * sub, (sb + 1) * sub)
            y = acc_ref[rows, :] + ffn(h_ref[rows, :])
            o_ref[rows, :] = x_ref[rows, :] + mod_ref[5:6, :] * (_rms(y) * ng_ref[3:4, :])


def _mlp(x, mod, ng, up_all, dn_all, layer, tm, tf):
    b, s, d = x.shape
    f = up_all.shape[2]
    assert f // tf >= 2
    return pl.pallas_call(
        functools.partial(_mlp_kernel, nsub=2),
        grid=(b, s // tm, f // tf),
        in_specs=[
            pl.BlockSpec((None, tm, d), lambda bi, i, j: (bi, i, 0)),
            pl.BlockSpec((None, 6, d), lambda bi, i, j: (bi, 0, 0)),
            pl.BlockSpec((4, d), lambda bi, i, j: (0, 0)),
            pl.BlockSpec((None, d, tf), lambda bi, i, j: (layer, 0, j)),
            pl.BlockSpec((None, tf, d), lambda bi, i, j: (layer, j, 0)),
        ],
        out_specs=pl.BlockSpec((None, tm, d), lambda bi, i, j: (bi, i, 0)),
        out_shape=jax.ShapeDtypeStruct((b, s, d), F32),
        scratch_shapes=[pltpu.VMEM((tm, d), BF16), pltpu.VMEM((tm, d), F32)],
        compiler_params=_params(("arbitrary", "arbitrary", "arbitrary")),
    )(x, mod, ng, up_all, dn_all)


def _mla_proj_kernel(x_ref, mod_ref, ng_ref, pos_ref, invf_ref, kvg_ref, kdc_ref, kdr_ref, kdrr_ref,
                     kvn_ref, uk_ref, uv_ref, dq_ref, qn_ref, uqn_ref, uqr_ref, uqrr_ref,
                     qnope_ref, qrope_ref, knope_ref, krope_ref, v_ref, *, scale):
    x = x_ref[...]
    xn = _rms(x)
    ang = pos_ref[...] * invf_ref[...]
    cos = jnp.cos(ang)
    sin = jnp.sin(ang)

    hs = (xn * kvg_ref[...]).astype(BF16)
    ckv = _rms(jnp.dot(hs, kdc_ref[...], preferred_element_type=F32)) * kvn_ref[...]
    ckv = ckv.astype(BF16)
    knope_ref[...] = jnp.dot(ckv, uk_ref[...], preferred_element_type=F32).astype(knope_ref.dtype)
    v_ref[...] = jnp.dot(ckv, uv_ref[...], preferred_element_type=F32).astype(v_ref.dtype)
    kr = (jnp.dot(hs, kdr_ref[...], preferred_element_type=F32) * cos
          + jnp.dot(hs, kdrr_ref[...], preferred_element_type=F32) * sin)
    krope_ref[...] = kr.astype(krope_ref.dtype)

    h = (xn * ng_ref[0:1, :] * (1.0 + mod_ref[1:2, :]) + mod_ref[0:1, :]).astype(BF16)
    cq = _rms(jnp.dot(h, dq_ref[...], preferred_element_type=F32)) * qn_ref[...]
    cq = cq.astype(BF16)
    qnope = jnp.dot(cq, uqn_ref[...], preferred_element_type=F32) * scale
    qnope_ref[...] = qnope.astype(qnope_ref.dtype)
    qr = jnp.dot(cq, uqr_ref[...], preferred_element_type=F32)
    qrr = jnp.dot(cq, uqrr_ref[...], preferred_element_type=F32)
    reps = qr.shape[1] // LANES
    cos_w = jnp.concatenate([cos] * reps, axis=1)
    sin_w = jnp.concatenate([sin] * reps, axis=1)
    qrope_ref[...] = ((qr * cos_w + qrr * sin_w) * scale).astype(qrope_ref.dtype)


def _mla_proj(x, mod, ng, positions, kvg, kdc, kdr, kdrr, kvn, uk, uv, dq, qn, uqn, uqr, uqrr,
              scale, tm):
    b, s, d = x.shape
    dr = uqr.shape[1]
    half = MLA_ROPE // 2
    inv_freq = 1.0 / (ROPE_THETA ** (jnp.arange(0, MLA_ROPE, 2, dtype=F32) / MLA_ROPE))
    invf = jnp.tile(inv_freq, LANES // half).reshape(1, LANES)
    pos = positions.astype(F32).reshape(b, s, 1)
    full = lambda arr: pl.BlockSpec(arr.shape, lambda bi, i: (0,) * arr.ndim)
    row = lambda w: pl.BlockSpec((None, tm, w), lambda bi, i: (bi, i, 0))
    return pl.pallas_call(
        functools.partial(_mla_proj_kernel, scale=scale),
        grid=(b, s // tm),
        in_specs=[row(d), pl.BlockSpec((None, 6, d), lambda bi, i: (bi, 0, 0)), full(ng),
                  row(1), full(invf), full(kvg), full(kdc), full(kdr), full(kdrr), full(kvn),
                  full(uk), full(uv), full(dq), full(qn), full(uqn), full(uqr), full(uqrr)],
        out_specs=[row(d), row(dr), row(d), row(LANES), row(d)],
        out_shape=[jax.ShapeDtypeStruct((b, s, d), BF16), jax.ShapeDtypeStruct((b, s, dr), BF16),
                   jax.ShapeDtypeStruct((b, s, d), BF16), jax.ShapeDtypeStruct((b, s, LANES), BF16),
                   jax.ShapeDtypeStruct((b, s, d), BF16)],
        compiler_params=_params(("arbitrary", "arbitrary")),
    )(x, mod, ng, pos, invf, kvg, kdc, kdr, kdrr, kvn, uk, uv, dq, qn, uqn, uqr, uqrr)


def _attn_kernel(qn_ref, qr_ref, kn_ref, kr_ref, v_ref, o_ref, *, tq):
    s = qn_ref.shape[0]
    neg = jnp.finfo(F32).min
    lane = lax.broadcasted_iota(jnp.int32, (1, LANES), 1)
    ri = lax.broadcasted_iota(jnp.int32, (tq, tq), 0)
    ci = lax.broadcasted_iota(jnp.int32, (tq, tq), 1)
    causal = ri >= ci
    kr = kr_ref[...]
    k_cat = [jnp.concatenate([kn_ref[:, hd * MLA_NOPE:(hd + 1) * MLA_NOPE], kr], axis=1)
             for hd in range(2)]

    def scores(hd, qi):
        rows = slice(qi * tq, (qi + 1) * tq)
        in_head = (lane >> 6) == hd
        q_rope = jnp.where(in_head, qr_ref[rows, :], jnp.zeros((), BF16))
        q_cat = jnp.concatenate([qn_ref[rows, hd * MLA_NOPE:(hd + 1) * MLA_NOPE], q_rope], axis=1)
        lo = qi * tq
        sd = jnp.where(causal, _dot_nt(q_cat, k_cat[hd][lo:lo + tq, :]), neg)
        sl = _dot_nt(q_cat, k_cat[hd][:lo, :]) if qi > 0 else None
        return sd, sl

    def finish(hd, qi, sd, sl):
        rows = slice(qi * tq, (qi + 1) * tq)
        cols = slice(hd * MLA_HEAD_V, (hd + 1) * MLA_HEAD_V)
        lo = qi * tq
        m = jnp.max(sd, axis=-1, keepdims=True)
        if sl is not None:
            m = jnp.maximum(m, jnp.max(sl, axis=-1, keepdims=True))
        pd = jnp.exp2(sd - m)
        den = jnp.sum(pd, axis=-1, keepdims=True)
        acc = _dot(pd, v_ref[lo:lo + tq, cols])
        if sl is not None:
            pl_ = jnp.exp2(sl - m)
            den = den + jnp.sum(pl_, axis=-1, keepdims=True)
            acc = acc + _dot(pl_, v_ref[:lo, cols])
        o_ref[rows, cols] = (acc / den).astype(o_ref.dtype)

    items = [(hd, qi) for qi in range(s // tq) for hd in range(2)]
    nxt = scores(*items[0])
    for idx, item in enumerate(items):
        cur = nxt
        if idx + 1 < len(items):
            nxt = scores(*items[idx + 1])
        finish(*item, *cur)


def _attention(qn, qr, kn, kr, v, tq):
    b, s, d = qn.shape
    npair = d // (2 * MLA_NOPE)
    wide = pl.BlockSpec((None, s, 2 * MLA_NOPE), lambda bi, hp: (bi, 0, hp))
    return pl.pallas_call(
        functools.partial(_attn_kernel, tq=tq),
        grid=(b, npair),
        in_specs=[wide, pl.BlockSpec((None, s, LANES), lambda bi, hp: (bi, 0, hp)), wide,
                  pl.BlockSpec((None, s, LANES), lambda bi, hp: (bi, 0, 0)), wide],
        out_specs=wide,
        out_shape=jax.ShapeDtypeStruct((b, s, d), BF16),
        compiler_params=_params(("arbitrary", "arbitrary")),
    )(qn, qr, kn, kr, v)


def _pad_cols(w, n):
    return jnp.pad(w, ((0, 0), (0, n - w.shape[1])))


def _pad_rows(w, n):
    return jnp.pad(w, ((0, n - w.shape[0]), (0, 0)))


def _rot_half_cols(w):
    k, n = w.shape
    half = MLA_ROPE // 2
    w3 = w.reshape(k, n // MLA_ROPE, MLA_ROPE)
    return jnp.concatenate([-w3[..., half:], w3[..., :half]], axis=-1).reshape(k, n)


def kernel(x, c, positions, ada_w, ada_b, norm_g, mlp_up, mlp_down, rw_mu, rw_rkv, rw_w0, rw_w1,
           rw_w2, rw_a0, rw_a1, rw_a2, rw_g1, rw_g2, rw_kk, rw_ka, rw_rk, rw_lnx, rw_o, mla_dq,
           mla_qnorm, mla_uq, mla_o, kv_in_g, kv_down, kv_norm, kv_uk, kv_uv):
    b, s, d = x.shape
    depth = ada_w.shape[0]
    n_rw = rw_mu.shape[0]
    kv_lora = kv_norm.shape[0]
    heads = d // MLA_HEAD_V
    assert d % (4 * LANES) == 0 and s % RW_CHUNK == 0

    tm = min(s, 512)
    tm_small = min(s, 256)
    tf = min(mlp_up.shape[2], 1024)
    tb = min(s, 512)
    tq = min(s, 512)
    scan_pairs = 4

    late_weights = (mlp_up, mlp_down, rw_o, mla_o)
    if n_rw == 0:
        up_all, dn_all, rwo_all, mlao_all = [_to_bf16(w) for w in late_weights]
    rkv_all = _to_bf16(rw_rkv)
    mod_all = _ada_mod(c, ada_w, ada_b).reshape(depth, b, 6, d)
    shared = None

    for l in range(depth):
        mod = mod_all[l]
        ng = norm_g[l]
        if l < n_rw:
            i = l
            lora = max(LANES, -(-rw_w1.shape[2] // LANES) * LANES)
            r, k, v, lw, a, g = _rw_in(
                x, mod, ng, rw_mu[i], rkv_all[i], rw_w0[i].reshape(1, d),
                _pad_cols(rw_w1[i], lora).astype(BF16), _pad_rows(rw_w2[i], lora).astype(BF16),
                rw_a0[i].reshape(1, d),
                _pad_cols(rw_a1[i], lora).astype(BF16), _pad_rows(rw_a2[i], lora).astype(BF16),
                rw_g1[i].astype(BF16), rw_g2[i].astype(BF16), tm_small)
            mixed, cast = _rw_scan(r, k, v, lw, a, g, rw_kk[i].reshape(1, d),
                                   rw_ka[i].reshape(1, d), rw_rk[i].reshape(1, d),
                                   rw_lnx[i, 0].reshape(1, d), rw_lnx[i, 1].reshape(1, d),
                                   tb, scan_pairs, late_weights if i == 0 else ())
            if i == 0:
                up_all, dn_all, rwo_all, mlao_all = cast
            x = _oproj(mixed, rwo_all[i], x, mod, ng, tm)
        else:
            i = l - n_rw
            uq = mla_uq[i]
            q_lora = uq.shape[0]
            uqn = uq[:, :, :MLA_NOPE].reshape(q_lora, heads * MLA_NOPE)
            uqr = uq[:, :, MLA_NOPE:].reshape(q_lora, heads * MLA_ROPE)
            kdr = kv_down[:, kv_lora:]
            kdr2 = jnp.concatenate([kdr, kdr], axis=1)
            scale = float((MLA_NOPE + MLA_ROPE) ** -0.5) * math.log2(math.e)
            qn, qr, kn, kr, v = _mla_proj(
                x, mod, ng, positions, kv_in_g.reshape(1, d),
                kv_down[:, :kv_lora].astype(BF16), kdr2.astype(BF16),
                _rot_half_cols(kdr2).astype(BF16), kv_norm.reshape(1, kv_lora),
                kv_uk.reshape(kv_lora, -1).astype(BF16), kv_uv.reshape(kv_lora, -1).astype(BF16),
                mla_dq[i].astype(BF16), mla_qnorm[i].reshape(1, q_lora), uqn.astype(BF16),
                uqr.astype(BF16), _rot_half_cols(uqr).astype(BF16), scale, tm_small)
            if shared is None:
                shared = (kn, kr, v)
            kn, kr, v = shared
            att = _attention(qn, qr, kn, kr, v, tq)
            x = _oproj(att, mlao_all[i], x, mod, ng, tm)
        x = _mlp(x, mod, ng, up_all, dn_all, l, tm, tf)
    return x
```

```python
import functools
import math

import jax
import jax.numpy as jnp
from jax import lax
from jax.experimental import pallas as pl
from jax.experimental.pallas import tpu as pltpu

F32 = jnp.float32
BF16 = jnp.bfloat16

LANES = 128
NORM_EPS = 1e-6
RW_HEAD = 64
RW_GN_EPS = RW_HEAD * 1e-5
RW_CHUNK = 64
MLA_HEAD_V = 128
MLA_NOPE = 128
MLA_ROPE = 64
ROPE_THETA = 10000.0
VMEM_LIMIT = 56 * 1024 * 1024


def _params(sem):
    return pltpu.CompilerParams(dimension_semantics=sem, vmem_limit_bytes=VMEM_LIMIT)


def _dot(a, b):
    return jnp.dot(a.astype(BF16), b.astype(BF16), preferred_element_type=F32)


def _dot_nt(a, b):
    return lax.dot_general(a.astype(BF16), b.astype(BF16), (((1,), (1,)), ((), ())),
                           preferred_element_type=F32)


def _rms(x):
    return x * lax.rsqrt(jnp.mean(x * x, axis=-1, keepdims=True) + NORM_EPS)


def _normmod(x, g, scale, shift):
    return _rms(x) * g * (1.0 + scale) + shift


def _sigmoid(x):
    return 1.0 / (1.0 + jnp.exp(-x))


def _sigmoid_t(x):
    return 0.5 * jnp.tanh(0.5 * x) + 0.5


def _split2(x):
    hi = x.astype(BF16)
    lo = (x - hi.astype(F32)).astype(BF16)
    return hi, lo


def _split3(x):
    hi = x.astype(BF16)
    r1 = x - hi.astype(F32)
    mid = r1.astype(BF16)
    lo = (r1 - mid.astype(F32)).astype(BF16)
    return hi, mid, lo


CAST_BLOCK_BYTES = 8 * 1024 * 1024


def _cast_kernel(x_ref, o_ref):
    o_ref[...] = x_ref[...].astype(o_ref.dtype)


def _to_bf16(w):
    shape = w.shape
    w2 = w.reshape(-1, shape[-1])
    r, c = w2.shape
    tr = min(r, max(16, CAST_BLOCK_BYTES // (4 * c)))
    if r % tr or tr % 16:
        return w.astype(BF16)
    out = pl.pallas_call(
        _cast_kernel,
        grid=(r // tr,),
        in_specs=[pl.BlockSpec((tr, c), lambda i: (i, 0))],
        out_specs=pl.BlockSpec((tr, c), lambda i: (i, 0)),
        out_shape=jax.ShapeDtypeStruct((r, c), BF16),
        compiler_params=_params(("arbitrary",)),
    )(w2)
    return out.reshape(shape)


def _ada_kernel(c_ref, w_ref, b_ref, o_ref):
    c = c_ref[...]
    c_hi, c_lo = _split2(c * _sigmoid(c))
    w_hi, w_lo = _split2(w_ref[...])
    o_ref[...] = (jnp.dot(c_hi, w_hi, preferred_element_type=F32)
                  + jnp.dot(c_lo, w_hi, preferred_element_type=F32)
                  + jnp.dot(c_hi, w_lo, preferred_element_type=F32)) + b_ref[...]


def _ada_mod(c, ada_w, ada_b):
    depth, d, n = ada_w.shape
    b = c.shape[0]
    tn = 1536 if n % 1536 == 0 else n
    return pl.pallas_call(
        _ada_kernel,
        grid=(depth, n // tn),
        in_specs=[
            pl.BlockSpec((b, d), lambda l, j: (0, 0)),
            pl.BlockSpec((None, d, tn), lambda l, j: (l, 0, j)),
            pl.BlockSpec((None, 1, tn), lambda l, j: (l, 0, j)),
        ],
        out_specs=pl.BlockSpec((None, b, tn), lambda l, j: (l, 0, j)),
        out_shape=jax.ShapeDtypeStruct((depth, b, n), F32),
        compiler_params=_params(("arbitrary", "arbitrary")),
    )(c, ada_w, ada_b.reshape(depth, 1, n))


def _rw_in_kernel(x_ref, xp_ref, mod_ref, ng_ref, mu_ref, w_ref, w0_ref, w1_ref, w2_ref,
                  a0_ref, a1_ref, a2_ref, g1_ref, g2_ref,
                  r_ref, k_ref, v_ref, lw_ref, a_ref, g_ref, *, sub):
    i = pl.program_id(1)
    g = ng_ref[0:1, :]
    shift = mod_ref[0:1, :]
    scale = mod_ref[1:2, :]
    prev = jnp.where(i == 0, 0.0, _normmod(xp_ref[7:8, :], g, scale, shift))
    row = lax.broadcasted_iota(jnp.int32, (8, x_ref.shape[1]), 0)
    for sb in range(x_ref.shape[0] // sub):
        rows = slice(sb * sub, (sb + 1) * sub)
        h = _normmod(x_ref[rows, :], g, scale, shift)
        hs = pltpu.roll(h, 1, 0)
        hs = jnp.concatenate([jnp.where(row == 0, prev, hs[0:8, :]), hs[8:, :]], axis=0)
        prev = h[sub - 1:sub, :]
        xx = hs - h

        def mix(s):
            return (h + xx * mu_ref[s:s + 1, :]).astype(BF16)

        for s, o_ref in enumerate((r_ref, k_ref, v_ref)):
            o_ref[rows, :] = jnp.dot(mix(s), w_ref[s],
                                     preferred_element_type=F32).astype(o_ref.dtype)
        wpre = w0_ref[...] + _dot(jnp.tanh(_dot(mix(3), w1_ref[...])), w2_ref[...])
        lw_ref[rows, :] = (-math.exp(-0.5)) * _sigmoid_t(wpre)
        a_ref[rows, :] = _sigmoid_t(
            a0_ref[...] + _dot(_dot(mix(4), a1_ref[...]), a2_ref[...])).astype(a_ref.dtype)
        g_ref[rows, :] = _dot(_sigmoid_t(_dot(mix(5), g1_ref[...])),
                              g2_ref[...]).astype(g_ref.dtype)


def _rw_in(x, mod, ng, mu, w_rkv, w0, w1, w2, a0, a1, a2, g1, g2, tm):
    b, s, d = x.shape
    row = pl.BlockSpec((None, tm, d), lambda bi, i: (bi, i, 0))

    def resident(arr):
        return pl.BlockSpec(arr.shape, lambda bi, i: (0,) * arr.ndim,
                            pipeline_mode=pl.Buffered(1))

    return pl.pallas_call(
        functools.partial(_rw_in_kernel, sub=tm // 2),
        grid=(b, s // tm),
        in_specs=[
            row,
            pl.BlockSpec((None, 8, d), lambda bi, i: (bi, jnp.maximum(i * (tm // 8) - 1, 0), 0)),
            pl.BlockSpec((None, 6, d), lambda bi, i: (bi, 0, 0)),
            pl.BlockSpec((4, d), lambda bi, i: (0, 0)),
            pl.BlockSpec((6, d), lambda bi, i: (0, 0)),
            resident(w_rkv), resident(w0), resident(w1), resident(w2),
            resident(a0), resident(a1), resident(a2), resident(g1), resident(g2),
        ],
        out_specs=[row] * 6,
        out_shape=[jax.ShapeDtypeStruct((b, s, d), BF16)] * 3
        + [jax.ShapeDtypeStruct((b, s, d), F32)] + [jax.ShapeDtypeStruct((b, s, d), BF16)] * 2,
        compiler_params=_params(("arbitrary", "arbitrary")),
    )(x, x, mod, ng, mu, w_rkv, w0, w1, w2, a0, a1, a2, g1, g2)


def _stack2(x, m0, m1):
    return jnp.concatenate([x * m0, x * m1], axis=0)


def _rw_scan_kernel(*refs, nchunk, npp, nblk, ncast):
    (r_ref, k_ref, v_ref, lw_ref, a_ref, g_ref, kkw_ref, ka_ref, rk_ref, lng_ref,
     lnb_ref) = refs[:11]
    cast_in = refs[11:11 + ncast]
    o_ref = refs[11 + ncast]
    cast_out = refs[12 + ncast:12 + 2 * ncast]
    (s_ref, q_ref, m_ref, bt_ref, yi_ref, gc_ref, bonus_ref, gate_ref,
     y_ref) = refs[12 + 2 * ncast:]
    for src, dst in zip(cast_in, cast_out):
        dst[...] = src[...].astype(dst.dtype)

    C = RW_CHUNK
    C2 = 2 * C
    nunit = npp * nchunk
    k_step = pl.program_id(0)
    slot_x = k_step % 2
    slot_y = 1 - slot_x
    base_x = slot_x * nunit
    base_y = slot_y * nunit
    first_block = (k_step + (nblk - 1)) % nblk == 0

    @pl.when(k_step == 0)
    def _():
        for ref in (s_ref, q_ref, m_ref, bt_ref, yi_ref, gc_ref, bonus_ref, gate_ref):
            ref[...] = jnp.zeros_like(ref)

    lane = lax.broadcasted_iota(jnp.int32, (1, LANES), 1)
    m0 = (lane < RW_HEAD).astype(F32)
    m1 = 1.0 - m0
    ri = lax.broadcasted_iota(jnp.int32, (C2, C2), 0)
    ci = lax.broadcasted_iota(jnp.int32, (C2, C2), 1)
    same_head = (ri >> 6) == (ci >> 6)
    strict = (ri & (C - 1)) > (ci & (C - 1))
    incl = (ri & (C - 1)) >= (ci & (C - 1))
    eye = (ri == ci).astype(F32)
    off_masks = []
    for j in range(C.bit_length() - 1):
        off_masks.append(((ri >> (j + 1)) == (ci >> (j + 1)))
                         & (((ri >> j) & 1) == 1) & (((ci >> j) & 1) == 0))
    block_ones = same_head.astype(BF16)
    tri = (incl[:C, :C]).astype(BF16)

    def head_sum(x):
        hi, lo = _split2(x)
        return (jnp.dot(hi, block_ones, preferred_element_type=F32)
                + jnp.dot(lo, block_ones, preferred_element_type=F32))

    def stack(x):
        return _stack2(x, m0, m1).astype(BF16)

    units = []

    def prep_inputs(pp):
        cols = slice(pp * LANES, (pp + 1) * LANES)
        K = k_ref[:, cols].astype(F32)
        A = a_ref[:, cols].astype(F32)
        LW = lw_ref[:, cols]
        kk0 = K * kkw_ref[:, cols]
        kk = kk0 * lax.rsqrt(jnp.maximum(head_sum(kk0 * kk0), 1e-24))
        k2 = K * (1.0 + (A - 1.0) * ka_ref[:, cols])
        kka = kk * A
        bonus_ref[slot_x, :, cols] = (head_sum(r_ref[:, cols] * k2 * rk_ref[:, cols])
                                      * v_ref[:, cols])
        gate_ref[slot_x, :, cols] = g_ref[:, cols].astype(gate_ref.dtype)
        cl_cat = None
        for part in _split3(LW):
            cat = jnp.concatenate([part[c * C:(c + 1) * C, :] for c in range(nchunk)], axis=1)
            term = jnp.dot(tri, cat, preferred_element_type=F32)
            cl_cat = term if cl_cat is None else cl_cat + term
        for c in range(nchunk):
            rows = slice(c * C, (c + 1) * C)
            cl = cl_cat[:, c * LANES:(c + 1) * LANES]
            cl_end = cl[C - 1:C, :]
            gam_inv = jnp.exp(-cl)
            units.append(dict(
                idx=pp * nchunk + c,
                al_s=stack(-kk[rows] * jnp.exp(cl - LW[rows])),
                rb_s=_stack2(r_ref[rows, cols] * jnp.exp(cl), m0, m1),
                be_s=stack(kka[rows] * gam_inv),
                kb_s=stack(k2[rows] * gam_inv),
                v_s=stack(v_ref[rows, cols]),
                gc=jnp.exp(cl_end)))

    def stage_gram():
        for u in units:
            G = _dot_nt(jnp.concatenate([u["al_s"], u["rb_s"].astype(BF16)], axis=0),
                        jnp.concatenate([u["be_s"], u["kb_s"]], axis=0))
            u["Lb"] = jnp.where(strict, G[:C2, :C2], 0.0)
            u["Lk"] = jnp.where(strict, G[:C2, C2:], 0.0).astype(BF16)
            u["R"] = jnp.concatenate([jnp.where(incl, G[C2:, :C2], 0.0),
                                      jnp.where(incl, G[C2:, C2:], 0.0)], axis=1).astype(BF16)

    def stage_init():
        for u in units:
            u["lkv"] = _dot(u["Lk"], u["v_s"]).astype(BF16)
            u["T"] = eye + jnp.where(off_masks[0], u["Lb"], 0.0)

    def lower_rows(x, s):
        return jnp.concatenate([x[r:r + s, :] for r in range(s, C2, 2 * s)], axis=0)

    def stage_left(j):
        s = 1 << j
        for u in units:
            lo = jnp.where(off_masks[j], u["Lb"], 0.0)
            u["TL"] = _dot(lower_rows(u["T"], s) if s >= 8 else u["T"], lo).astype(BF16)

    def stage_right(j):
        s = 1 << j
        for u in units:
            upd = _dot(u["TL"], u["T"])
            if s >= 8:
                pieces = []
                for m in range(C2 // (2 * s)):
                    pieces.append(u["T"][2 * m * s:(2 * m + 1) * s, :])
                    pieces.append(u["T"][(2 * m + 1) * s:(2 * m + 2) * s, :] + upd[m * s:(m + 1) * s, :])
                u["T"] = jnp.concatenate(pieces, axis=0)
            else:
                u["T"] = u["T"] + upd

    def stage_solve():
        for u in units:
            u["Z"] = _dot(u["T"], jnp.concatenate([u["al_s"], u["lkv"]], axis=1))

    def stage_store():
        for u in units:
            X = jnp.concatenate(
                [u["Z"],
                 jnp.concatenate([jnp.zeros((C2, C2), F32), u["v_s"].astype(F32)], axis=1)],
                axis=0)
            QY = _dot(u["R"], X)
            MB = _dot(X.T, jnp.concatenate([u["be_s"], u["kb_s"]], axis=0)) * u["gc"]
            i = base_x + u["idx"]
            q_ref[i] = (u["rb_s"] + QY[:, :C2]).astype(BF16)
            yi_ref[i] = QY[:, C2:]
            m_ref[i] = MB[:C2, :].astype(BF16)
            bt_ref[i] = MB[C2:, :]
            gc_ref[i] = jnp.broadcast_to(u["gc"], (8, LANES))

    prepare = [functools.partial(prep_inputs, pp) for pp in range(npp)]
    prepare += [stage_gram, stage_init]
    for j in range(1, len(off_masks)):
        prepare += [functools.partial(stage_left, j), functools.partial(stage_right, j)]
    prepare += [stage_solve, stage_store]

    S = [jnp.where(first_block, 0.0, s_ref[pp]) for pp in range(npp)]

    def state_step(c):
        for pp in range(npp):
            i = base_y + pp * nchunk + c
            Sb = S[pp].astype(BF16)
            ys = _dot_nt(q_ref[i], Sb) + yi_ref[i]
            y_ref[c * C:(c + 1) * C, pp * LANES:(pp + 1) * LANES] = ys[:C, :] + ys[C:, :]
            S[pp] = (S[pp] * gc_ref[i][0:1, :]
                     + jnp.dot(Sb, m_ref[i], preferred_element_type=F32) + bt_ref[i])

    def finish_out():
        inv_n = 1.0 / RW_HEAD
        for pp in range(npp):
            s_ref[pp] = S[pp]
            cols = slice(pp * LANES, (pp + 1) * LANES)
            y = y_ref[:, cols]
            mean = head_sum(y) * inv_n
            yc = y - mean
            var = head_sum(yc * yc) * inv_n
            yn = yc * lax.rsqrt(var + RW_GN_EPS) * lng_ref[:, cols] + lnb_ref[:, cols]
            o_ref[:, cols] = ((yn + bonus_ref[slot_y, :, cols])
                              * gate_ref[slot_y, :, cols]).astype(o_ref.dtype)

    finish = [functools.partial(state_step, c) for c in range(nchunk)] + [finish_out]

    done = 0
    for si, stage in enumerate(prepare):
        want = ((si + 1) * len(finish)) // len(prepare)
        while done < want:
            finish[done]()
            done += 1
        stage()


def _rw_scan(r, k, v, lw, a, g, kkw, ka, rk, lng, lnb, tb, npp, weights=()):
    b, s, d = r.shape
    nchunk = tb // RW_CHUNK
    width = npp * LANES
    C2 = 2 * RW_CHUNK
    nunit = npp * nchunk
    nblk = s // tb
    ngrp = d // width
    total = b * ngrp * nblk

    def split(kk):
        return kk // (ngrp * nblk), kk % nblk, (kk // nblk) % ngrp

    def tok_in(kq):
        return split(jnp.minimum(kq, total - 1))

    def tok_out(kq):
        return split(jnp.maximum(kq - 1, 0))

    tin = pl.BlockSpec((None, tb, width), tok_in)
    pin = pl.BlockSpec((1, width), lambda kq: (0, tok_in(kq)[2]))
    pout = pl.BlockSpec((1, width), lambda kq: (0, tok_out(kq)[2]))
    flat = [w.reshape(-1, w.shape[-1]) for w in weights]
    for w2 in flat:
        assert w2.shape[0] % total == 0 and (w2.shape[0] // total) % 16 == 0
    cast_specs = [pl.BlockSpec((w2.shape[0] // total, w2.shape[1]),
                               lambda kq: (jnp.minimum(kq, total - 1), 0)) for w2 in flat]
    outs = pl.pallas_call(
        functools.partial(_rw_scan_kernel, nchunk=nchunk, npp=npp, nblk=nblk, ncast=len(flat)),
        grid=(total + 1,),
        in_specs=[tin, tin, tin, tin, tin, tin, pin, pin, pin, pout, pout] + cast_specs,
        out_specs=[pl.BlockSpec((None, tb, width), tok_out)] + cast_specs,
        out_shape=[jax.ShapeDtypeStruct((b, s, d), BF16)]
        + [jax.ShapeDtypeStruct(w2.shape, BF16) for w2 in flat],
        scratch_shapes=[
            pltpu.VMEM((npp, C2, C2), F32),
            pltpu.VMEM((2 * nunit, C2, C2), BF16),
            pltpu.VMEM((2 * nunit, C2, C2), BF16),
            pltpu.VMEM((2 * nunit, C2, C2), F32),
            pltpu.VMEM((2 * nunit, C2, C2), F32),
            pltpu.VMEM((2 * nunit, 8, LANES), F32),
            pltpu.VMEM((2, tb, width), F32),
            pltpu.VMEM((2, tb, width), BF16),
            pltpu.VMEM((tb, width), F32),
        ],
        compiler_params=_params(("arbitrary",)),
    )(r, k, v, lw, a, g, kkw, ka, rk, lng, lnb, *flat)
    return outs[0], [o.reshape(w.shape) for o, w in zip(outs[1:], weights)]


def _oproj_kernel(a_ref, w_ref, x_ref, mod_ref, ng_ref, o_ref):
    y = jnp.dot(a_ref[...], w_ref[...], preferred_element_type=F32)
    o_ref[...] = x_ref[...] + mod_ref[2:3, :] * (_rms(y) * ng_ref[1:2, :])


def _oproj(a, w, x, mod, ng, tm):
    b, s, d = x.shape
    k = a.shape[-1]
    return pl.pallas_call(
        _oproj_kernel,
        grid=(b, s // tm),
        in_specs=[
            pl.BlockSpec((None, tm, k), lambda bi, i: (bi, i, 0)),
            pl.BlockSpec((k, d), lambda bi, i: (0, 0)),
            pl.BlockSpec((None, tm, d), lambda bi, i: (bi, i, 0)),
            pl.BlockSpec((None, 6, d), lambda bi, i: (bi, 0, 0)),
            pl.BlockSpec((4, d), lambda bi, i: (0, 0)),
        ],
        out_specs=pl.BlockSpec((None, tm, d), lambda bi, i: (bi, i, 0)),
        out_shape=jax.ShapeDtypeStruct((b, s, d), F32),
        compiler_params=_params(("arbitrary", "arbitrary")),
    )(a, w, x, mod, ng)


def _mlp_kernel(x_ref, mod_ref, ng_ref, up_ref, dn_ref, o_ref, h_ref, acc_ref, *, nsub):
    j = pl.program_id(2)
    last = pl.num_programs(2) - 1
    sub = x_ref.shape[0] // nsub

    def ffn(h):
        u = jnp.maximum(jnp.dot(h, up_ref[...], preferred_element_type=F32), 0.0)
        return jnp.dot((u * u).astype(BF16), dn_ref[...], preferred_element_type=F32)

    @pl.when(j == 0)
    def _():
        for sb in range(nsub):
            rows = slice(sb * sub, (sb + 1) * sub)
            h = _normmod(x_ref[rows, :], ng_ref[2:3, :], mod_ref[4:5, :], mod_ref[3:4, :])
            h = h.astype(BF16)
            h_ref[rows, :] = h
            acc_ref[rows, :] = ffn(h)

    @pl.when(jnp.logical_and(j > 0, j < last))
    def _():
        acc_ref[...] += ffn(h_ref[...])

    @pl.when(j == last)
    def _():
        for sb in range(nsub):
            rows = slice(sb * sub, (sb + 1) * sub)
            y = acc_ref[rows, :] + ffn(h_ref[rows, :])
            o_ref[rows, :] = x_ref[rows, :] + mod_ref[5:6, :] * (_rms(y) * ng_ref[3:4, :])


def _mlp(x, mod, ng, up_all, dn_all, layer, tm, tf):
    b, s, d = x.shape
    f = up_all.shape[2]
    assert f // tf >= 2
    return pl.pallas_call(
        functools.partial(_mlp_kernel, nsub=2),
        grid=(b, s // tm, f // tf),
        in_specs=[
            pl.BlockSpec((None, tm, d), lambda bi, i, j: (bi, i, 0)),
            pl.BlockSpec((None, 6, d), lambda bi, i, j: (bi, 0, 0)),
            pl.BlockSpec((4, d), lambda bi, i, j: (0, 0)),
            pl.BlockSpec((None, d, tf), lambda bi, i, j: (layer, 0, j)),
            pl.BlockSpec((None, tf, d), lambda bi, i, j: (layer, j, 0)),
        ],
        out_specs=pl.BlockSpec((None, tm, d), lambda bi, i, j: (bi, i, 0)),
        out_shape=jax.ShapeDtypeStruct((b, s, d), F32),
        scratch_shapes=[pltpu.VMEM((tm, d), BF16), pltpu.VMEM((tm, d), F32)],
        compiler_params=_params(("arbitrary", "arbitrary", "arbitrary")),
    )(x, mod, ng, up_all, dn_all)


def _mla_proj_kernel(x_ref, mod_ref, ng_ref, pos_ref, invf_ref, kvg_ref, kdc_ref, kdr_ref, kdrr_ref,
                     kvn_ref, uk_ref, uv_ref, dq_ref, qn_ref, uqn_ref, uqr_ref, uqrr_ref,
                     qnope_ref, qrope_ref, knope_ref, krope_ref, v_ref, *, scale):
    x = x_ref[...]
    xn = _rms(x)
    ang = pos_ref[...] * invf_ref[...]
    cos = jnp.cos(ang)
    sin = jnp.sin(ang)

    hs = (xn * kvg_ref[...]).astype(BF16)
    ckv = _rms(jnp.dot(hs, kdc_ref[...], preferred_element_type=F32)) * kvn_ref[...]
    ckv = ckv.astype(BF16)
    knope_ref[...] = jnp.dot(ckv, uk_ref[...], preferred_element_type=F32).astype(knope_ref.dtype)
    v_ref[...] = jnp.dot(ckv, uv_ref[...], preferred_element_type=F32).astype(v_ref.dtype)
    kr = (jnp.dot(hs, kdr_ref[...], preferred_element_type=F32) * cos
          + jnp.dot(hs, kdrr_ref[...], preferred_element_type=F32) * sin)
    krope_ref[...] = kr.astype(krope_ref.dtype)

    h = (xn * ng_ref[0:1, :] * (1.0 + mod_ref[1:2, :]) + mod_ref[0:1, :]).astype(BF16)
    cq = _rms(jnp.dot(h, dq_ref[...], preferred_element_type=F32)) * qn_ref[...]
    cq = cq.astype(BF16)
    qnope = jnp.dot(cq, uqn_ref[...], preferred_element_type=F32) * scale
    qnope_ref[...] = qnope.astype(qnope_ref.dtype)
    qr = jnp.dot(cq, uqr_ref[...], preferred_element_type=F32)
    qrr = jnp.dot(cq, uqrr_ref[...], preferred_element_type=F32)
    reps = qr.shape[1] // LANES
    cos_w = jnp.concatenate([cos] * reps, axis=1)
    sin_w = jnp.concatenate([sin] * reps, axis=1)
    qrope_ref[...] = ((qr * cos_w + qrr * sin_w) * scale).astype(qrope_ref.dtype)


def _mla_proj(x, mod, ng, positions, kvg, kdc, kdr, kdrr, kvn, uk, uv, dq, qn, uqn, uqr, uqrr,
              scale, tm):
    b, s, d = x.shape
    dr = uqr.shape[1]
    half = MLA_ROPE // 2
    inv_freq = 1.0 / (ROPE_THETA ** (jnp.arange(0, MLA_ROPE, 2, dtype=F32) / MLA_ROPE))
    invf = jnp.tile(inv_freq, LANES // half).reshape(1, LANES)
    pos = positions.astype(F32).reshape(b, s, 1)
    full = lambda arr: pl.BlockSpec(arr.shape, lambda bi, i: (0,) * arr.ndim)
    row = lambda w: pl.BlockSpec((None, tm, w), lambda bi, i: (bi, i, 0))
    return pl.pallas_call(
        functools.partial(_mla_proj_kernel, scale=scale),
        grid=(b, s // tm),
        in_specs=[row(d), pl.BlockSpec((None, 6, d), lambda bi, i: (bi, 0, 0)), full(ng),
                  row(1), full(invf), full(kvg), full(kdc), full(kdr), full(kdrr), full(kvn),
                  full(uk), full(uv), full(dq), full(qn), full(uqn), full(uqr), full(uqrr)],
        out_specs=[row(d), row(dr), row(d), row(LANES), row(d)],
        out_shape=[jax.ShapeDtypeStruct((b, s, d), BF16), jax.ShapeDtypeStruct((b, s, dr), BF16),
                   jax.ShapeDtypeStruct((b, s, d), BF16), jax.ShapeDtypeStruct((b, s, LANES), BF16),
                   jax.ShapeDtypeStruct((b, s, d), BF16)],
        compiler_params=_params(("arbitrary", "arbitrary")),
    )(x, mod, ng, pos, invf, kvg, kdc, kdr, kdrr, kvn, uk, uv, dq, qn, uqn, uqr, uqrr)


def _attn_kernel(qn_ref, qr_ref, kn_ref, kr_ref, v_ref, o_ref, *, tq):
    s = qn_ref.shape[0]
    neg = jnp.finfo(F32).min
    lane = lax.broadcasted_iota(jnp.int32, (1, LANES), 1)
    ri = lax.broadcasted_iota(jnp.int32, (tq, tq), 0)
    ci = lax.broadcasted_iota(jnp.int32, (tq, tq), 1)
    causal = ri >= ci
    kr = kr_ref[...]
    k_cat = [jnp.concatenate([kn_ref[:, hd * MLA_NOPE:(hd + 1) * MLA_NOPE], kr], axis=1)
             for hd in range(2)]

    def scores(hd, qi):
        rows = slice(qi * tq, (qi + 1) * tq)
        in_head = (lane >> 6) == hd
        q_rope = jnp.where(in_head, qr_ref[rows, :], jnp.zeros((), BF16))
        q_cat = jnp.concatenate([qn_ref[rows, hd * MLA_NOPE:(hd + 1) * MLA_NOPE], q_rope], axis=1)
        lo = qi * tq
        sd = jnp.where(causal, _dot_nt(q_cat, k_cat[hd][lo:lo + tq, :]), neg)
        sl = _dot_nt(q_cat, k_cat[hd][:lo, :]) if qi > 0 else None
        return sd, sl

    def finish(hd, qi, sd, sl):
        rows = slice(qi * tq, (qi + 1) * tq)
        cols = slice(hd * MLA_HEAD_V, (hd + 1) * MLA_HEAD_V)
        lo = qi * tq
        m = jnp.max(sd, axis=-1, keepdims=True)
        if sl is not None:
            m = jnp.maximum(m, jnp.max(sl, axis=-1, keepdims=True))
        pd = jnp.exp2(sd - m)
        den = jnp.sum(pd, axis=-1, keepdims=True)
        acc = _dot(pd, v_ref[lo:lo + tq, cols])
        if sl is not None:
            pl_ = jnp.exp2(sl - m)
            den = den + jnp.sum(pl_, axis=-1, keepdims=True)
            acc = acc + _dot(pl_, v_ref[:lo, cols])
        o_ref[rows, cols] = (acc / den).astype(o_ref.dtype)

    items = [(hd, qi) for qi in range(s // tq) for hd in range(2)]
    nxt = scores(*items[0])
    for idx, item in enumerate(items):
        cur = nxt
        if idx + 1 < len(items):
            nxt = scores(*items[idx + 1])
        finish(*item, *cur)


def _attention(qn, qr, kn, kr, v, tq):
    b, s, d = qn.shape
    npair = d // (2 * MLA_NOPE)
    wide = pl.BlockSpec((None, s, 2 * MLA_NOPE), lambda bi, hp: (bi, 0, hp))
    return pl.pallas_call(
        functools.partial(_attn_kernel, tq=tq),
        grid=(b, npair),
        in_specs=[wide, pl.BlockSpec((None, s, LANES), lambda bi, hp: (bi, 0, hp)), wide,
                  pl.BlockSpec((None, s, LANES), lambda bi, hp: (bi, 0, 0)), wide],
        out_specs=wide,
        out_shape=jax.ShapeDtypeStruct((b, s, d), BF16),
        compiler_params=_params(("arbitrary", "arbitrary")),
    )(qn, qr, kn, kr, v)


def _pad_cols(w, n):
    return jnp.pad(w, ((0, 0), (0, n - w.shape[1])))


def _pad_rows(w, n):
    return jnp.pad(w, ((0, n - w.shape[0]), (0, 0)))


def _rot_half_cols(w):
    k, n = w.shape
    half = MLA_ROPE // 2
    w3 = w.reshape(k, n // MLA_ROPE, MLA_ROPE)
    return jnp.concatenate([-w3[..., half:], w3[..., :half]], axis=-1).reshape(k, n)


def kernel(x, c, positions, ada_w, ada_b, norm_g, mlp_up, mlp_down, rw_mu, rw_rkv, rw_w0, rw_w1,
           rw_w2, rw_a0, rw_a1, rw_a2, rw_g1, rw_g2, rw_kk, rw_ka, rw_rk, rw_lnx, rw_o, mla_dq,
           mla_qnorm, mla_uq, mla_o, kv_in_g, kv_down, kv_norm, kv_uk, kv_uv):
    b, s, d = x.shape
    depth = ada_w.shape[0]
    n_rw = rw_mu.shape[0]
    kv_lora = kv_norm.shape[0]
    heads = d // MLA_HEAD_V
    assert d % (4 * LANES) == 0 and s % RW_CHUNK == 0

    tm = min(s, 512)
    tm_small = min(s, 256)
    tf = min(mlp_up.shape[2], 1024)
    tb = min(s, 512)
    tq = min(s, 512)
    scan_pairs = 4

    late_weights = (mlp_up, mlp_down, rw_o, mla_o)
    if n_rw == 0:
        up_all, dn_all, rwo_all, mlao_all = [_to_bf16(w) for w in late_weights]
    rkv_all = _to_bf16(rw_rkv)
    mod_all = _ada_mod(c, ada_w, ada_b).reshape(depth, b, 6, d)
    shared = None

    for l in range(depth):
        mod = mod_all[l]
        ng = norm_g[l]
        if l < n_rw:
            i = l
            lora = max(LANES, -(-rw_w1.shape[2] // LANES) * LANES)
            r, k, v, lw, a, g = _rw_in(
                x, mod, ng, rw_mu[i], rkv_all[i], rw_w0[i].reshape(1, d),
                _pad_cols(rw_w1[i], lora).astype(BF16), _pad_rows(rw_w2[i], lora).astype(BF16),
                rw_a0[i].reshape(1, d),
                _pad_cols(rw_a1[i], lora).astype(BF16), _pad_rows(rw_a2[i], lora).astype(BF16),
                rw_g1[i].astype(BF16), rw_g2[i].astype(BF16), tm_small)
            mixed, cast = _rw_scan(r, k, v, lw, a, g, rw_kk[i].reshape(1, d),
                                   rw_ka[i].reshape(1, d), rw_rk[i].reshape(1, d),
                                   rw_lnx[i, 0].reshape(1, d), rw_lnx[i, 1].reshape(1, d),
                                   tb, scan_pairs, late_weights if i == 0 else ())
            if i == 0:
                up_all, dn_all, rwo_all, mlao_all = cast
            x = _oproj(mixed, rwo_all[i], x, mod, ng, tm)
        else:
            i = l - n_rw
            uq = mla_uq[i]
            q_lora = uq.shape[0]
            uqn = uq[:, :, :MLA_NOPE].reshape(q_lora, heads * MLA_NOPE)
            uqr = uq[:, :, MLA_NOPE:].reshape(q_lora, heads * MLA_ROPE)
            kdr = kv_down[:, kv_lora:]
            kdr2 = jnp.concatenate([kdr, kdr], axis=1)
            scale = float((MLA_NOPE + MLA_ROPE) ** -0.5) * math.log2(math.e)
            qn, qr, kn, kr, v = _mla_proj(
                x, mod, ng, positions, kv_in_g.reshape(1, d),
                kv_down[:, :kv_lora].astype(BF16), kdr2.astype(BF16),
                _rot_half_cols(kdr2).astype(BF16), kv_norm.reshape(1, kv_lora),
                kv_uk.reshape(kv_lora, -1).astype(BF16), kv_uv.reshape(kv_lora, -1).astype(BF16),
                mla_dq[i].astype(BF16), mla_qnorm[i].reshape(1, q_lora), uqn.astype(BF16),
                uqr.astype(BF16), _rot_half_cols(uqr).astype(BF16), scale, tm_small)
            if shared is None:
                shared = (kn, kr, v)
            kn, kr, v = shared
            att = _attention(qn, qr, kn, kr, v, tq)
            x = _oproj(att, mlao_all[i], x, mod, ng, tm)
        x = _mlp(x, mod, ng, up_all, dn_all, l, tm, tf)
    return x
```

```python
import functools
import math

import jax
import jax.numpy as jnp
from jax import lax
from jax.experimental import pallas as pl
from jax.experimental.pallas import tpu as pltpu

F32 = jnp.float32
BF16 = jnp.bfloat16

LANES = 128
NORM_EPS = 1e-6
RW_HEAD = 64
RW_GN_EPS = RW_HEAD * 1e-5
RW_CHUNK = 64
MLA_HEAD_V = 128
MLA_NOPE = 128
MLA_ROPE = 64
ROPE_THETA = 10000.0
VMEM_LIMIT = 56 * 1024 * 1024


def _params(sem):
    return pltpu.CompilerParams(dimension_semantics=sem, vmem_limit_bytes=VMEM_LIMIT)


def _dot(a, b):
    return jnp.dot(a.astype(BF16), b.astype(BF16), preferred_element_type=F32)


def _dot_nt(a, b):
    return lax.dot_general(a.astype(BF16), b.astype(BF16), (((1,), (1,)), ((), ())),
                           preferred_element_type=F32)


def _rms(x):
    return x * lax.rsqrt(jnp.mean(x * x, axis=-1, keepdims=True) + NORM_EPS)


def _normmod(x, g, scale, shift):
    return _rms(x) * g * (1.0 + scale) + shift


def _sigmoid(x):
    return 1.0 / (1.0 + jnp.exp(-x))


def _sigmoid_t(x):
    return 0.5 * jnp.tanh(0.5 * x) + 0.5


def _split2(x):
    hi = x.astype(BF16)
    lo = (x - hi.astype(F32)).astype(BF16)
    return hi, lo


def _split3(x):
    hi = x.astype(BF16)
    r1 = x - hi.astype(F32)
    mid = r1.astype(BF16)
    lo = (r1 - mid.astype(F32)).astype(BF16)
    return hi, mid, lo


CAST_BLOCK_BYTES = 8 * 1024 * 1024


def _cast_kernel(x_ref, o_ref):
    o_ref[...] = x_ref[...].astype(o_ref.dtype)


def _to_bf16(w):
    shape = w.shape
    w2 = w.reshape(-1, shape[-1])
    r, c = w2.shape
    tr = min(r, max(16, CAST_BLOCK_BYTES // (4 * c)))
    if r % tr or tr % 16:
        return w.astype(BF16)
    out = pl.pallas_call(
        _cast_kernel,
        grid=(r // tr,),
        in_specs=[pl.BlockSpec((tr, c), lambda i: (i, 0))],
        out_specs=pl.BlockSpec((tr, c), lambda i: (i, 0)),
        out_shape=jax.ShapeDtypeStruct((r, c), BF16),
        compiler_params=_params(("arbitrary",)),
    )(w2)
    return out.reshape(shape)


def _ada_kernel(c_ref, w_ref, b_ref, o_ref):
    c = c_ref[...]
    c_hi, c_lo = _split2(c * _sigmoid(c))
    w_hi, w_lo = _split2(w_ref[...])
    o_ref[...] = (jnp.dot(c_hi, w_hi, preferred_element_type=F32)
                  + jnp.dot(c_lo, w_hi, preferred_element_type=F32)
                  + jnp.dot(c_hi, w_lo, preferred_element_type=F32)) + b_ref[...]


def _ada_mod(c, ada_w, ada_b):
    depth, d, n = ada_w.shape
    b = c.shape[0]
    tn = 1536 if n % 1536 == 0 else n
    return pl.pallas_call(
        _ada_kernel,
        grid=(depth, n // tn),
        in_specs=[
            pl.BlockSpec((b, d), lambda l, j: (0, 0)),
            pl.BlockSpec((None, d, tn), lambda l, j: (l, 0, j)),
            pl.BlockSpec((None, 1, tn), lambda l, j: (l, 0, j)),
        ],
        out_specs=pl.BlockSpec((None, b, tn), lambda l, j: (l, 0, j)),
        out_shape=jax.ShapeDtypeStruct((depth, b, n), F32),
        compiler_params=_params(("arbitrary", "arbitrary")),
    )(c, ada_w, ada_b.reshape(depth, 1, n))


def _rw_in_kernel(x_ref, xp_ref, mod_ref, ng_ref, mu_ref, w_ref, w0_ref, w1_ref, w2_ref,
                  a0_ref, a1_ref, a2_ref, g1_ref, g2_ref,
                  r_ref, k_ref, v_ref, lw_ref, a_ref, g_ref, *, sub):
    i = pl.program_id(1)
    g = ng_ref[0:1, :]
    shift = mod_ref[0:1, :]
    scale = mod_ref[1:2, :]
    prev = jnp.where(i == 0, 0.0, _normmod(xp_ref[7:8, :], g, scale, shift))
    row = lax.broadcasted_iota(jnp.int32, (8, x_ref.shape[1]), 0)
    for sb in range(x_ref.shape[0] // sub):
        rows = slice(sb * sub, (sb + 1) * sub)
        h = _normmod(x_ref[rows, :], g, scale, shift)
        hs = pltpu.roll(h, 1, 0)
        hs = jnp.concatenate([jnp.where(row == 0, prev, hs[0:8, :]), hs[8:, :]], axis=0)
        prev = h[sub - 1:sub, :]
        xx = hs - h

        def mix(s):
            return (h + xx * mu_ref[s:s + 1, :]).astype(BF16)

        hid_w = jnp.tanh(_dot(mix(3), w1_ref[...]))
        hid_a = _dot(mix(4), a1_ref[...])
        hid_g = _sigmoid_t(_dot(mix(5), g1_ref[...]))
        for s, o_ref in enumerate((r_ref, k_ref, v_ref)):
            o_ref[rows, :] = jnp.dot(mix(s), w_ref[s],
                                     preferred_element_type=F32).astype(o_ref.dtype)
        wpre = w0_ref[...] + _dot(hid_w, w2_ref[...])
        lw_ref[rows, :] = (-math.exp(-0.5)) * _sigmoid_t(wpre)
        a_ref[rows, :] = _sigmoid_t(a0_ref[...] + _dot(hid_a, a2_ref[...])).astype(a_ref.dtype)
        g_ref[rows, :] = _dot(hid_g, g2_ref[...]).astype(g_ref.dtype)


def _rw_in(x, mod, ng, mu, w_rkv, w0, w1, w2, a0, a1, a2, g1, g2, tm):
    b, s, d = x.shape
    row = pl.BlockSpec((None, tm, d), lambda bi, i: (bi, i, 0))

    def resident(arr):
        return pl.BlockSpec(arr.shape, lambda bi, i: (0,) * arr.ndim,
                            pipeline_mode=pl.Buffered(1))

    return pl.pallas_call(
        functools.partial(_rw_in_kernel, sub=tm // 2),
        grid=(b, s // tm),
        in_specs=[
            row,
            pl.BlockSpec((None, 8, d), lambda bi, i: (bi, jnp.maximum(i * (tm // 8) - 1, 0), 0)),
            pl.BlockSpec((None, 6, d), lambda bi, i: (bi, 0, 0)),
            pl.BlockSpec((4, d), lambda bi, i: (0, 0)),
            pl.BlockSpec((6, d), lambda bi, i: (0, 0)),
            resident(w_rkv), resident(w0), resident(w1), resident(w2),
            resident(a0), resident(a1), resident(a2), resident(g1), resident(g2),
        ],
        out_specs=[row] * 6,
        out_shape=[jax.ShapeDtypeStruct((b, s, d), BF16)] * 3
        + [jax.ShapeDtypeStruct((b, s, d), F32)] + [jax.ShapeDtypeStruct((b, s, d), BF16)] * 2,
        compiler_params=_params(("arbitrary", "arbitrary")),
    )(x, x, mod, ng, mu, w_rkv, w0, w1, w2, a0, a1, a2, g1, g2)


def _stack2(x, m0, m1):
    return jnp.concatenate([x * m0, x * m1], axis=0)


def _rw_scan_kernel(*refs, nchunk, npp, nblk, ncast):
    (r_ref, k_ref, v_ref, lw_ref, a_ref, g_ref, kkw_ref, ka_ref, rk_ref, lng_ref,
     lnb_ref) = refs[:11]
    cast_in = refs[11:11 + ncast]
    o_ref = refs[11 + ncast]
    cast_out = refs[12 + ncast:12 + 2 * ncast]
    (s_ref, q_ref, m_ref, bt_ref, yi_ref, gc_ref, bonus_ref, gate_ref,
     y_ref) = refs[12 + 2 * ncast:]
    for src, dst in zip(cast_in, cast_out):
        dst[...] = src[...].astype(dst.dtype)

    C = RW_CHUNK
    C2 = 2 * C
    nunit = npp * nchunk
    k_step = pl.program_id(0)
    slot_x = k_step % 2
    slot_y = 1 - slot_x
    base_x = slot_x * nunit
    base_y = slot_y * nunit
    first_block = (k_step + (nblk - 1)) % nblk == 0

    @pl.when(k_step == 0)
    def _():
        for ref in (s_ref, q_ref, m_ref, bt_ref, yi_ref, gc_ref, bonus_ref, gate_ref):
            ref[...] = jnp.zeros_like(ref)

    lane = lax.broadcasted_iota(jnp.int32, (1, LANES), 1)
    m0 = (lane < RW_HEAD).astype(F32)
    m1 = 1.0 - m0
    ri = lax.broadcasted_iota(jnp.int32, (C2, C2), 0)
    ci = lax.broadcasted_iota(jnp.int32, (C2, C2), 1)
    same_head = (ri >> 6) == (ci >> 6)
    strict = (ri & (C - 1)) > (ci & (C - 1))
    incl = (ri & (C - 1)) >= (ci & (C - 1))
    eye = (ri == ci).astype(F32)
    off_masks = []
    for j in range(C.bit_length() - 1):
        off_masks.append(((ri >> (j + 1)) == (ci >> (j + 1)))
                         & (((ri >> j) & 1) == 1) & (((ci >> j) & 1) == 0))
    block_ones = same_head.astype(BF16)
    tri = (incl[:C, :C]).astype(BF16)

    def head_sum(x):
        hi, lo = _split2(x)
        return (jnp.dot(hi, block_ones, preferred_element_type=F32)
                + jnp.dot(lo, block_ones, preferred_element_type=F32))

    def stack(x):
        return _stack2(x, m0, m1).astype(BF16)

    units = []

    def prep_inputs(pp):
        cols = slice(pp * LANES, (pp + 1) * LANES)
        K = k_ref[:, cols].astype(F32)
        A = a_ref[:, cols].astype(F32)
        LW = lw_ref[:, cols]
        kk0 = K * kkw_ref[:, cols]
        kk = kk0 * lax.rsqrt(jnp.maximum(head_sum(kk0 * kk0), 1e-24))
        k2 = K * (1.0 + (A - 1.0) * ka_ref[:, cols])
        kka = kk * A
        bonus_ref[slot_x, :, cols] = (head_sum(r_ref[:, cols] * k2 * rk_ref[:, cols])
                                      * v_ref[:, cols])
        gate_ref[slot_x, :, cols] = g_ref[:, cols].astype(gate_ref.dtype)
        cl_cat = None
        for part in _split3(LW):
            cat = jnp.concatenate([part[c * C:(c + 1) * C, :] for c in range(nchunk)], axis=1)
            term = jnp.dot(tri, cat, preferred_element_type=F32)
            cl_cat = term if cl_cat is None else cl_cat + term
        for c in range(nchunk):
            rows = slice(c * C, (c + 1) * C)
            cl = cl_cat[:, c * LANES:(c + 1) * LANES]
            cl_end = cl[C - 1:C, :]
            gam_inv = jnp.exp(-cl)
            units.append(dict(
                idx=pp * nchunk + c,
                al_s=stack(-kk[rows] * jnp.exp(cl - LW[rows])),
                rb_s=_stack2(r_ref[rows, cols] * jnp.exp(cl), m0, m1),
                be_s=stack(kka[rows] * gam_inv),
                kb_s=stack(k2[rows] * gam_inv),
                v_s=stack(v_ref[rows, cols]),
                gc=jnp.exp(cl_end)))

    def stage_gram():
        for u in units:
            G = _dot_nt(jnp.concatenate([u["al_s"], u["rb_s"].astype(BF16)], axis=0),
                        jnp.concatenate([u["be_s"], u["kb_s"]], axis=0))
            u["Lb"] = jnp.where(strict, G[:C2, :C2], 0.0)
            u["Lk"] = jnp.where(strict, G[:C2, C2:], 0.0).astype(BF16)
            u["R"] = jnp.concatenate([jnp.where(incl, G[C2:, :C2], 0.0),
                                      jnp.where(incl, G[C2:, C2:], 0.0)], axis=1).astype(BF16)

    def stage_init():
        for u in units:
            u["lkv"] = _dot(u["Lk"], u["v_s"]).astype(BF16)
            u["T"] = eye + jnp.where(off_masks[0], u["Lb"], 0.0)

    def lower_rows(x, s):
        return jnp.concatenate([x[r:r + s, :] for r in range(s, C2, 2 * s)], axis=0)

    def stage_left(j):
        s = 1 << j
        for u in units:
            lo = jnp.where(off_masks[j], u["Lb"], 0.0)
            u["TL"] = _dot(lower_rows(u["T"], s) if s >= 8 else u["T"], lo).astype(BF16)

    def stage_right(j):
        s = 1 << j
        for u in units:
            upd = _dot(u["TL"], u["T"])
            if s >= 8:
                pieces = []
                for m in range(C2 // (2 * s)):
                    pieces.append(u["T"][2 * m * s:(2 * m + 1) * s, :])
                    pieces.append(u["T"][(2 * m + 1) * s:(2 * m + 2) * s, :] + upd[m * s:(m + 1) * s, :])
                u["T"] = jnp.concatenate(pieces, axis=0)
            else:
                u["T"] = u["T"] + upd

    def stage_solve():
        for u in units:
            u["Z"] = _dot(u["T"], jnp.concatenate([u["al_s"], u["lkv"]], axis=1))

    def stage_store():
        for u in units:
            X = jnp.concatenate(
                [u["Z"],
                 jnp.concatenate([jnp.zeros((C2, C2), F32), u["v_s"].astype(F32)], axis=1)],
                axis=0)
            QY = _dot(u["R"], X)
            MB = _dot(X.T, jnp.concatenate([u["be_s"], u["kb_s"]], axis=0)) * u["gc"]
            i = base_x + u["idx"]
            q_ref[i] = (u["rb_s"] + QY[:, :C2]).astype(BF16)
            yi_ref[i] = QY[:, C2:]
            m_ref[i] = MB[:C2, :].astype(BF16)
            bt_ref[i] = MB[C2:, :]
            gc_ref[i] = jnp.broadcast_to(u["gc"], (8, LANES))

    prepare = [functools.partial(prep_inputs, pp) for pp in range(npp)]
    prepare += [stage_gram, stage_init]
    for j in range(1, len(off_masks)):
        prepare += [functools.partial(stage_left, j), functools.partial(stage_right, j)]
    prepare += [stage_solve, stage_store]

    S = [jnp.where(first_block, 0.0, s_ref[pp]) for pp in range(npp)]

    def state_step(c):
        for pp in range(npp):
            i = base_y + pp * nchunk + c
            Sb = S[pp].astype(BF16)
            ys = _dot_nt(q_ref[i], Sb) + yi_ref[i]
            y_ref[c * C:(c + 1) * C, pp * LANES:(pp + 1) * LANES] = ys[:C, :] + ys[C:, :]
            S[pp] = (S[pp] * gc_ref[i][0:1, :]
                     + jnp.dot(Sb, m_ref[i], preferred_element_type=F32) + bt_ref[i])

    def finish_out():
        inv_n = 1.0 / RW_HEAD
        for pp in range(npp):
            s_ref[pp] = S[pp]
            cols = slice(pp * LANES, (pp + 1) * LANES)
            y = y_ref[:, cols]
            mean = head_sum(y) * inv_n
            yc = y - mean
            var = head_sum(yc * yc) * inv_n
            yn = yc * lax.rsqrt(var + RW_GN_EPS) * lng_ref[:, cols] + lnb_ref[:, cols]
            o_ref[:, cols] = ((yn + bonus_ref[slot_y, :, cols])
                              * gate_ref[slot_y, :, cols]).astype(o_ref.dtype)

    finish = [functools.partial(state_step, c) for c in range(nchunk)] + [finish_out]

    done = 0
    for si, stage in enumerate(prepare):
        want = ((si + 1) * len(finish)) // len(prepare)
        while done < want:
            finish[done]()
            done += 1
        stage()


def _rw_scan(r, k, v, lw, a, g, kkw, ka, rk, lng, lnb, tb, npp, weights=()):
    b, s, d = r.shape
    nchunk = tb // RW_CHUNK
    width = npp * LANES
    C2 = 2 * RW_CHUNK
    nunit = npp * nchunk
    nblk = s // tb
    ngrp = d // width
    total = b * ngrp * nblk

    def split(kk):
        return kk // (ngrp * nblk), kk % nblk, (kk // nblk) % ngrp

    def tok_in(kq):
        return split(jnp.minimum(kq, total - 1))

    def tok_out(kq):
        return split(jnp.maximum(kq - 1, 0))

    tin = pl.BlockSpec((None, tb, width), tok_in)
    pin = pl.BlockSpec((1, width), lambda kq: (0, tok_in(kq)[2]))
    pout = pl.BlockSpec((1, width), lambda kq: (0, tok_out(kq)[2]))
    flat = [w.reshape(-1, w.shape[-1]) for w in weights]
    for w2 in flat:
        assert w2.shape[0] % total == 0 and (w2.shape[0] // total) % 16 == 0
    cast_specs = [pl.BlockSpec((w2.shape[0] // total, w2.shape[1]),
                               lambda kq: (jnp.minimum(kq, total - 1), 0)) for w2 in flat]
    outs = pl.pallas_call(
        functools.partial(_rw_scan_kernel, nchunk=nchunk, npp=npp, nblk=nblk, ncast=len(flat)),
        grid=(total + 1,),
        in_specs=[tin, tin, tin, tin, tin, tin, pin, pin, pin, pout, pout] + cast_specs,
        out_specs=[pl.BlockSpec((None, tb, width), tok_out)] + cast_specs,
        out_shape=[jax.ShapeDtypeStruct((b, s, d), BF16)]
        + [jax.ShapeDtypeStruct(w2.shape, BF16) for w2 in flat],
        scratch_shapes=[
            pltpu.VMEM((npp, C2, C2), F32),
            pltpu.VMEM((2 * nunit, C2, C2), BF16),
            pltpu.VMEM((2 * nunit, C2, C2), BF16),
            pltpu.VMEM((2 * nunit, C2, C2), F32),
            pltpu.VMEM((2 * nunit, C2, C2), F32),
            pltpu.VMEM((2 * nunit, 8, LANES), F32),
            pltpu.VMEM((2, tb, width), F32),
            pltpu.VMEM((2, tb, width), BF16),
            pltpu.VMEM((tb, width), F32),
        ],
        compiler_params=_params(("arbitrary",)),
    )(r, k, v, lw, a, g, kkw, ka, rk, lng, lnb, *flat)
    return outs[0], [o.reshape(w.shape) for o, w in zip(outs[1:], weights)]


def _oproj_kernel(a_ref, w_ref, x_ref, mod_ref, ng_ref, o_ref):
    y = jnp.dot(a_ref[...], w_ref[...], preferred_element_type=F32)
    o_ref[...] = x_ref[...] + mod_ref[2:3, :] * (_rms(y) * ng_ref[1:2, :])


def _oproj(a, w, x, mod, ng, tm):
    b, s, d = x.shape
    k = a.shape[-1]
    return pl.pallas_call(
        _oproj_kernel,
        grid=(b, s // tm),
        in_specs=[
            pl.BlockSpec((None, tm, k), lambda bi, i: (bi, i, 0)),
            pl.BlockSpec((k, d), lambda bi, i: (0, 0)),
            pl.BlockSpec((None, tm, d), lambda bi, i: (bi, i, 0)),
            pl.BlockSpec((None, 6, d), lambda bi, i: (bi, 0, 0)),
            pl.BlockSpec((4, d), lambda bi, i: (0, 0)),
        ],
        out_specs=pl.BlockSpec((None, tm, d), lambda bi, i: (bi, i, 0)),
        out_shape=jax.ShapeDtypeStruct((b, s, d), F32),
        compiler_params=_params(("arbitrary", "arbitrary")),
    )(a, w, x, mod, ng)


def _mlp_kernel(x_ref, mod_ref, ng_ref, up_ref, dn_ref, o_ref, h_ref, acc_ref, *, nsub):
    j = pl.program_id(2)
    last = pl.num_programs(2) - 1
    sub = x_ref.shape[0] // nsub

    def ffn(h):
        u = jnp.maximum(jnp.dot(h, up_ref[...], preferred_element_type=F32), 0.0)
        return jnp.dot((u * u).astype(BF16), dn_ref[...], preferred_element_type=F32)

    @pl.when(j == 0)
    def _():
        for sb in range(nsub):
            rows = slice(sb * sub, (sb + 1) * sub)
            h = _normmod(x_ref[rows, :], ng_ref[2:3, :], mod_ref[4:5, :], mod_ref[3:4, :])
            h = h.astype(BF16)
            h_ref[rows, :] = h
            acc_ref[rows, :] = ffn(h)

    @pl.when(jnp.logical_and(j > 0, j < last))
    def _():
        acc_ref[...] += ffn(h_ref[...])

    @pl.when(j == last)
    def _():
        for sb in range(nsub):
            rows = slice(sb * sub, (sb + 1) * sub)
            y = acc_ref[rows, :] + ffn(h_ref[rows, :])
            o_ref[rows, :] = x_ref[rows, :] + mod_ref[5:6, :] * (_rms(y) * ng_ref[3:4, :])


def _mlp(x, mod, ng, up_all, dn_all, layer, tm, tf):
    b, s, d = x.shape
    f = up_all.shape[2]
    assert f // tf >= 2
    return pl.pallas_call(
        functools.partial(_mlp_kernel, nsub=2),
        grid=(b, s // tm, f // tf),
        in_specs=[
            pl.BlockSpec((None, tm, d), lambda bi, i, j: (bi, i, 0)),
            pl.BlockSpec((None, 6, d), lambda bi, i, j: (bi, 0, 0)),
            pl.BlockSpec((4, d), lambda bi, i, j: (0, 0)),
            pl.BlockSpec((None, d, tf), lambda bi, i, j: (layer, 0, j)),
            pl.BlockSpec((None, tf, d), lambda bi, i, j: (layer, j, 0)),
        ],
        out_specs=pl.BlockSpec((None, tm, d), lambda bi, i, j: (bi, i, 0)),
        out_shape=jax.ShapeDtypeStruct((b, s, d), F32),
        scratch_shapes=[pltpu.VMEM((tm, d), BF16), pltpu.VMEM((tm, d), F32)],
        compiler_params=_params(("arbitrary", "arbitrary", "arbitrary")),
    )(x, mod, ng, up_all, dn_all)


def _mla_proj_kernel(x_ref, mod_ref, ng_ref, pos_ref, invf_ref, kvg_ref, kdc_ref, kdr_ref, kdrr_ref,
                     kvn_ref, uk_ref, uv_ref, dq_ref, qn_ref, uqn_ref, uqr_ref, uqrr_ref,
                     qnope_ref, qrope_ref, knope_ref, krope_ref, v_ref, *, scale):
    x = x_ref[...]
    xn = _rms(x)
    ang = pos_ref[...] * invf_ref[...]
    cos = jnp.cos(ang)
    sin = jnp.sin(ang)

    hs = (xn * kvg_ref[...]).astype(BF16)
    ckv = _rms(jnp.dot(hs, kdc_ref[...], preferred_element_type=F32)) * kvn_ref[...]
    ckv = ckv.astype(BF16)
    knope_ref[...] = jnp.dot(ckv, uk_ref[...], preferred_element_type=F32).astype(knope_ref.dtype)
    v_ref[...] = jnp.dot(ckv, uv_ref[...], preferred_element_type=F32).astype(v_ref.dtype)
    kr = (jnp.dot(hs, kdr_ref[...], preferred_element_type=F32) * cos
          + jnp.dot(hs, kdrr_ref[...], preferred_element_type=F32) * sin)
    krope_ref[...] = kr.astype(krope_ref.dtype)

    h = (xn * ng_ref[0:1, :] * (1.0 + mod_ref[1:2, :]) + mod_ref[0:1, :]).astype(BF16)
    cq = _rms(jnp.dot(h, dq_ref[...], preferred_element_type=F32)) * qn_ref[...]
    cq = cq.astype(BF16)
    qnope = jnp.dot(cq, uqn_ref[...], preferred_element_type=F32) * scale
    qnope_ref[...] = qnope.astype(qnope_ref.dtype)
    qr = jnp.dot(cq, uqr_ref[...], preferred_element_type=F32)
    qrr = jnp.dot(cq, uqrr_ref[...], preferred_element_type=F32)
    reps = qr.shape[1] // LANES
    cos_w = jnp.concatenate([cos] * reps, axis=1)
    sin_w = jnp.concatenate([sin] * reps, axis=1)
    qrope_ref[...] = ((qr * cos_w + qrr * sin_w) * scale).astype(qrope_ref.dtype)


def _mla_proj(x, mod, ng, positions, kvg, kdc, kdr, kdrr, kvn, uk, uv, dq, qn, uqn, uqr, uqrr,
              scale, tm):
    b, s, d = x.shape
    dr = uqr.shape[1]
    half = MLA_ROPE // 2
    inv_freq = 1.0 / (ROPE_THETA ** (jnp.arange(0, MLA_ROPE, 2, dtype=F32) / MLA_ROPE))
    invf = jnp.tile(inv_freq, LANES // half).reshape(1, LANES)
    pos = positions.astype(F32).reshape(b, s, 1)
    full = lambda arr: pl.BlockSpec(arr.shape, lambda bi, i: (0,) * arr.ndim)
    row = lambda w: pl.BlockSpec((None, tm, w), lambda bi, i: (bi, i, 0))
    return pl.pallas_call(
        functools.partial(_mla_proj_kernel, scale=scale),
        grid=(b, s // tm),
        in_specs=[row(d), pl.BlockSpec((None, 6, d), lambda bi, i: (bi, 0, 0)), full(ng),
                  row(1), full(invf), full(kvg), full(kdc), full(kdr), full(kdrr), full(kvn),
                  full(uk), full(uv), full(dq), full(qn), full(uqn), full(uqr), full(uqrr)],
        out_specs=[row(d), row(dr), row(d), row(LANES), row(d)],
        out_shape=[jax.ShapeDtypeStruct((b, s, d), BF16), jax.ShapeDtypeStruct((b, s, dr), BF16),
                   jax.ShapeDtypeStruct((b, s, d), BF16), jax.ShapeDtypeStruct((b, s, LANES), BF16),
                   jax.ShapeDtypeStruct((b, s, d), BF16)],
        compiler_params=_params(("arbitrary", "arbitrary")),
    )(x, mod, ng, pos, invf, kvg, kdc, kdr, kdrr, kvn, uk, uv, dq, qn, uqn, uqr, uqrr)


def _attn_kernel(qn_ref, qr_ref, kn_ref, kr_ref, v_ref, o_ref, *, tq):
    s = qn_ref.shape[0]
    neg = jnp.finfo(F32).min
    lane = lax.broadcasted_iota(jnp.int32, (1, LANES), 1)
    ri = lax.broadcasted_iota(jnp.int32, (tq, tq), 0)
    ci = lax.broadcasted_iota(jnp.int32, (tq, tq), 1)
    causal = ri >= ci
    kr = kr_ref[...]
    k_cat = [jnp.concatenate([kn_ref[:, hd * MLA_NOPE:(hd + 1) * MLA_NOPE], kr], axis=1)
             for hd in range(2)]

    def scores(hd, qi):
        rows = slice(qi * tq, (qi + 1) * tq)
        in_head = (lane >> 6) == hd
        q_rope = jnp.where(in_head, qr_ref[rows, :], jnp.zeros((), BF16))
        q_cat = jnp.concatenate([qn_ref[rows, hd * MLA_NOPE:(hd + 1) * MLA_NOPE], q_rope], axis=1)
        lo = qi * tq
        sd = jnp.where(causal, _dot_nt(q_cat, k_cat[hd][lo:lo + tq, :]), neg)
        sl = _dot_nt(q_cat, k_cat[hd][:lo, :]) if qi > 0 else None
        return sd, sl

    def finish(hd, qi, sd, sl):
        rows = slice(qi * tq, (qi + 1) * tq)
        cols = slice(hd * MLA_HEAD_V, (hd + 1) * MLA_HEAD_V)
        lo = qi * tq
        m = jnp.max(sd, axis=-1, keepdims=True)
        if sl is not None:
            m = jnp.maximum(m, jnp.max(sl, axis=-1, keepdims=True))
        pd = jnp.exp2(sd - m)
        den = jnp.sum(pd, axis=-1, keepdims=True)
        acc = _dot(pd, v_ref[lo:lo + tq, cols])
        if sl is not None:
            pl_ = jnp.exp2(sl - m)
            den = den + jnp.sum(pl_, axis=-1, keepdims=True)
            acc = acc + _dot(pl_, v_ref[:lo, cols])
        o_ref[rows, cols] = (acc / den).astype(o_ref.dtype)

    items = [(hd, qi) for qi in range(s // tq) for hd in range(2)]
    nxt = scores(*items[0])
    for idx, item in enumerate(items):
        cur = nxt
        if idx + 1 < len(items):
            nxt = scores(*items[idx + 1])
        finish(*item, *cur)


def _attention(qn, qr, kn, kr, v, tq):
    b, s, d = qn.shape
    npair = d // (2 * MLA_NOPE)
    wide = pl.BlockSpec((None, s, 2 * MLA_NOPE), lambda bi, hp: (bi, 0, hp))
    return pl.pallas_call(
        functools.partial(_attn_kernel, tq=tq),
        grid=(b, npair),
        in_specs=[wide, pl.BlockSpec((None, s, LANES), lambda bi, hp: (bi, 0, hp)), wide,
                  pl.BlockSpec((None, s, LANES), lambda bi, hp: (bi, 0, 0)), wide],
        out_specs=wide,
        out_shape=jax.ShapeDtypeStruct((b, s, d), BF16),
        compiler_params=_params(("arbitrary", "arbitrary")),
    )(qn, qr, kn, kr, v)


def _pad_cols(w, n):
    return jnp.pad(w, ((0, 0), (0, n - w.shape[1])))


def _pad_rows(w, n):
    return jnp.pad(w, ((0, n - w.shape[0]), (0, 0)))


def _rot_half_cols(w):
    k, n = w.shape
    half = MLA_ROPE // 2
    w3 = w.reshape(k, n // MLA_ROPE, MLA_ROPE)
    return jnp.concatenate([-w3[..., half:], w3[..., :half]], axis=-1).reshape(k, n)


def kernel(x, c, positions, ada_w, ada_b, norm_g, mlp_up, mlp_down, rw_mu, rw_rkv, rw_w0, rw_w1,
           rw_w2, rw_a0, rw_a1, rw_a2, rw_g1, rw_g2, rw_kk, rw_ka, rw_rk, rw_lnx, rw_o, mla_dq,
           mla_qnorm, mla_uq, mla_o, kv_in_g, kv_down, kv_norm, kv_uk, kv_uv):
    b, s, d = x.shape
    depth = ada_w.shape[0]
    n_rw = rw_mu.shape[0]
    kv_lora = kv_norm.shape[0]
    heads = d // MLA_HEAD_V
    assert d % (4 * LANES) == 0 and s % RW_CHUNK == 0

    tm = min(s, 512)
    tm_small = min(s, 256)
    tf = min(mlp_up.shape[2], 1024)
    tb = min(s, 512)
    tq = min(s, 512)
    scan_pairs = 4

    late_weights = (mlp_up, mlp_down, rw_o, mla_o)
    if n_rw == 0:
        up_all, dn_all, rwo_all, mlao_all = [_to_bf16(w) for w in late_weights]
    rkv_all = _to_bf16(rw_rkv)
    mod_all = _ada_mod(c, ada_w, ada_b).reshape(depth, b, 6, d)
    shared = None

    for l in range(depth):
        mod = mod_all[l]
        ng = norm_g[l]
        if l < n_rw:
            i = l
            lora = max(LANES, -(-rw_w1.shape[2] // LANES) * LANES)
            r, k, v, lw, a, g = _rw_in(
                x, mod, ng, rw_mu[i], rkv_all[i], rw_w0[i].reshape(1, d),
                _pad_cols(rw_w1[i], lora).astype(BF16), _pad_rows(rw_w2[i], lora).astype(BF16),
                rw_a0[i].reshape(1, d),
                _pad_cols(rw_a1[i], lora).astype(BF16), _pad_rows(rw_a2[i], lora).astype(BF16),
                rw_g1[i].astype(BF16), rw_g2[i].astype(BF16), tm_small)
            mixed, cast = _rw_scan(r, k, v, lw, a, g, rw_kk[i].reshape(1, d),
                                   rw_ka[i].reshape(1, d), rw_rk[i].reshape(1, d),
                                   rw_lnx[i, 0].reshape(1, d), rw_lnx[i, 1].reshape(1, d),
                                   tb, scan_pairs, late_weights if i == 0 else ())
            if i == 0:
                up_all, dn_all, rwo_all, mlao_all = cast
            x = _oproj(mixed, rwo_all[i], x, mod, ng, tm)
        else:
            i = l - n_rw
            uq = mla_uq[i]
            q_lora = uq.shape[0]
            uqn = uq[:, :, :MLA_NOPE].reshape(q_lora, heads * MLA_NOPE)
            uqr = uq[:, :, MLA_NOPE:].reshape(q_lora, heads * MLA_ROPE)
            kdr = kv_down[:, kv_lora:]
            kdr2 = jnp.concatenate([kdr, kdr], axis=1)
            scale = float((MLA_NOPE + MLA_ROPE) ** -0.5) * math.log2(math.e)
            qn, qr, kn, kr, v = _mla_proj(
                x, mod, ng, positions, kv_in_g.reshape(1, d),
                kv_down[:, :kv_lora].astype(BF16), kdr2.astype(BF16),
                _rot_half_cols(kdr2).astype(BF16), kv_norm.reshape(1, kv_lora),
                kv_uk.reshape(kv_lora, -1).astype(BF16), kv_uv.reshape(kv_lora, -1).astype(BF16),
                mla_dq[i].astype(BF16), mla_qnorm[i].reshape(1, q_lora), uqn.astype(BF16),
                uqr.astype(BF16), _rot_half_cols(uqr).astype(BF16), scale, tm_small)
            if shared is None:
                shared = (kn, kr, v)
            kn, kr, v = shared
            att = _attention(qn, qr, kn, kr, v, tq)
            x = _oproj(att, mlao_all[i], x, mod, ng, tm)
        x = _mlp(x, mod, ng, up_all, dn_all, l, tm, tf)
    return x
```

```python
import functools
import math

import jax
import jax.numpy as jnp
from jax import lax
from jax.experimental import pallas as pl
from jax.experimental.pallas import tpu as pltpu

F32 = jnp.float32
BF16 = jnp.bfloat16

LANES = 128
NORM_EPS = 1e-6
RW_HEAD = 64
RW_GN_EPS = RW_HEAD * 1e-5
RW_CHUNK = 64
MLA_HEAD_V = 128
MLA_NOPE = 128
MLA_ROPE = 64
ROPE_THETA = 10000.0
VMEM_LIMIT = 56 * 1024 * 1024


def _params(sem):
    return pltpu.CompilerParams(dimension_semantics=sem, vmem_limit_bytes=VMEM_LIMIT)


def _dot(a, b):
    return jnp.dot(a.astype(BF16), b.astype(BF16), preferred_element_type=F32)


def _dot_nt(a, b):
    return lax.dot_general(a.astype(BF16), b.astype(BF16), (((1,), (1,)), ((), ())),
                           preferred_element_type=F32)


def _rms(x):
    return x * lax.rsqrt(jnp.mean(x * x, axis=-1, keepdims=True) + NORM_EPS)


def _normmod(x, g, scale, shift):
    return _rms(x) * g * (1.0 + scale) + shift


def _sigmoid(x):
    return 1.0 / (1.0 + jnp.exp(-x))


def _sigmoid_t(x):
    return 0.5 * jnp.tanh(0.5 * x) + 0.5


def _split2(x):
    hi = x.astype(BF16)
    lo = (x - hi.astype(F32)).astype(BF16)
    return hi, lo


def _split3(x):
    hi = x.astype(BF16)
    r1 = x - hi.astype(F32)
    mid = r1.astype(BF16)
    lo = (r1 - mid.astype(F32)).astype(BF16)
    return hi, mid, lo


CAST_BLOCK_BYTES = 8 * 1024 * 1024


def _cast_kernel(x_ref, o_ref):
    o_ref[...] = x_ref[...].astype(o_ref.dtype)


def _to_bf16(w):
    shape = w.shape
    w2 = w.reshape(-1, shape[-1])
    r, c = w2.shape
    tr = min(r, max(16, CAST_BLOCK_BYTES // (4 * c)))
    if r % tr or tr % 16:
        return w.astype(BF16)
    out = pl.pallas_call(
        _cast_kernel,
        grid=(r // tr,),
        in_specs=[pl.BlockSpec((tr, c), lambda i: (i, 0))],
        out_specs=pl.BlockSpec((tr, c), lambda i: (i, 0)),
        out_shape=jax.ShapeDtypeStruct((r, c), BF16),
        compiler_params=_params(("arbitrary",)),
    )(w2)
    return out.reshape(shape)


def _ada_kernel(c_ref, w_ref, b_ref, o_ref):
    c = c_ref[...]
    c_hi, c_lo = _split2(c * _sigmoid(c))
    w_hi, w_lo = _split2(w_ref[...])
    o_ref[...] = (jnp.dot(c_hi, w_hi, preferred_element_type=F32)
                  + jnp.dot(c_lo, w_hi, preferred_element_type=F32)
                  + jnp.dot(c_hi, w_lo, preferred_element_type=F32)) + b_ref[...]


def _ada_mod(c, ada_w, ada_b):
    depth, d, n = ada_w.shape
    b = c.shape[0]
    tn = 1536 if n % 1536 == 0 else n
    return pl.pallas_call(
        _ada_kernel,
        grid=(depth, n // tn),
        in_specs=[
            pl.BlockSpec((b, d), lambda l, j: (0, 0)),
            pl.BlockSpec((None, d, tn), lambda l, j: (l, 0, j)),
            pl.BlockSpec((None, 1, tn), lambda l, j: (l, 0, j)),
        ],
        out_specs=pl.BlockSpec((None, b, tn), lambda l, j: (l, 0, j)),
        out_shape=jax.ShapeDtypeStruct((depth, b, n), F32),
        compiler_params=_params(("arbitrary", "arbitrary")),
    )(c, ada_w, ada_b.reshape(depth, 1, n))


def _rw_in_kernel(x_ref, xp_ref, mod_ref, ng_ref, mu_ref, w_ref, w0_ref, w1_ref, w2_ref,
                  a0_ref, a1_ref, a2_ref, g1_ref, g2_ref,
                  r_ref, k_ref, v_ref, lw_ref, a_ref, g_ref, *, sub):
    i = pl.program_id(1)
    g = ng_ref[0:1, :]
    shift = mod_ref[0:1, :]
    scale = mod_ref[1:2, :]
    prev = jnp.where(i == 0, 0.0, _normmod(xp_ref[7:8, :], g, scale, shift))
    row = lax.broadcasted_iota(jnp.int32, (8, x_ref.shape[1]), 0)
    for sb in range(x_ref.shape[0] // sub):
        rows = slice(sb * sub, (sb + 1) * sub)
        h = _normmod(x_ref[rows, :], g, scale, shift)
        hs = pltpu.roll(h, 1, 0)
        hs = jnp.concatenate([jnp.where(row == 0, prev, hs[0:8, :]), hs[8:, :]], axis=0)
        prev = h[sub - 1:sub, :]
        xx = hs - h

        def mix(s):
            return (h + xx * mu_ref[s:s + 1, :]).astype(BF16)

        hid_w = jnp.tanh(_dot(mix(3), w1_ref[...]))
        hid_a = _dot(mix(4), a1_ref[...])
        hid_g = _sigmoid_t(_dot(mix(5), g1_ref[...]))
        for s, o_ref in enumerate((r_ref, k_ref, v_ref)):
            o_ref[rows, :] = jnp.dot(mix(s), w_ref[s],
                                     preferred_element_type=F32).astype(o_ref.dtype)
        wpre = w0_ref[...] + _dot(hid_w, w2_ref[...])
        lw_ref[rows, :] = (-math.exp(-0.5)) * _sigmoid_t(wpre)
        a_ref[rows, :] = _sigmoid_t(a0_ref[...] + _dot(hid_a, a2_ref[...])).astype(a_ref.dtype)
        g_ref[rows, :] = _dot(hid_g, g2_ref[...]).astype(g_ref.dtype)


def _rw_in(x, mod, ng, mu, w_rkv, w0, w1, w2, a0, a1, a2, g1, g2, tm):
    b, s, d = x.shape
    row = pl.BlockSpec((None, tm, d), lambda bi, i: (bi, i, 0))

    def resident(arr):
        return pl.BlockSpec(arr.shape, lambda bi, i: (0,) * arr.ndim,
                            pipeline_mode=pl.Buffered(1))

    return pl.pallas_call(
        functools.partial(_rw_in_kernel, sub=tm // 2),
        grid=(b, s // tm),
        in_specs=[
            row,
            pl.BlockSpec((None, 8, d), lambda bi, i: (bi, jnp.maximum(i * (tm // 8) - 1, 0), 0)),
            pl.BlockSpec((None, 6, d), lambda bi, i: (bi, 0, 0)),
            pl.BlockSpec((4, d), lambda bi, i: (0, 0)),
            pl.BlockSpec((6, d), lambda bi, i: (0, 0)),
            resident(w_rkv), resident(w0), resident(w1), resident(w2),
            resident(a0), resident(a1), resident(a2), resident(g1), resident(g2),
        ],
        out_specs=[row] * 6,
        out_shape=[jax.ShapeDtypeStruct((b, s, d), BF16)] * 3
        + [jax.ShapeDtypeStruct((b, s, d), F32)] + [jax.ShapeDtypeStruct((b, s, d), BF16)] * 2,
        compiler_params=_params(("arbitrary", "arbitrary")),
    )(x, x, mod, ng, mu, w_rkv, w0, w1, w2, a0, a1, a2, g1, g2)


def _stack2(x, m0, m1):
    return jnp.concatenate([x * m0, x * m1], axis=0)


def _rw_scan_kernel(*refs, nchunk, npp, nblk, ncast):
    (r_ref, k_ref, v_ref, lw_ref, a_ref, g_ref, kkw_ref, ka_ref, rk_ref, lng_ref,
     lnb_ref) = refs[:11]
    cast_in = refs[11:11 + ncast]
    o_ref = refs[11 + ncast]
    cast_out = refs[12 + ncast:12 + 2 * ncast]
    (s_ref, q_ref, m_ref, bt_ref, yi_ref, gc_ref, bonus_ref, gate_ref,
     y_ref) = refs[12 + 2 * ncast:]
    for src, dst in zip(cast_in, cast_out):
        dst[...] = src[...].astype(dst.dtype)

    C = RW_CHUNK
    C2 = 2 * C
    nunit = npp * nchunk
    k_step = pl.program_id(0)
    slot_x = k_step % 2
    slot_y = 1 - slot_x
    base_x = slot_x * nunit
    base_y = slot_y * nunit
    first_block = (k_step + (nblk - 1)) % nblk == 0

    @pl.when(k_step == 0)
    def _():
        for ref in (s_ref, q_ref, m_ref, bt_ref, yi_ref, gc_ref, bonus_ref, gate_ref):
            ref[...] = jnp.zeros_like(ref)

    lane = lax.broadcasted_iota(jnp.int32, (1, LANES), 1)
    m0 = (lane < RW_HEAD).astype(F32)
    m1 = 1.0 - m0
    ri = lax.broadcasted_iota(jnp.int32, (C2, C2), 0)
    ci = lax.broadcasted_iota(jnp.int32, (C2, C2), 1)
    same_head = (ri >> 6) == (ci >> 6)
    strict = (ri & (C - 1)) > (ci & (C - 1))
    incl = (ri & (C - 1)) >= (ci & (C - 1))
    eye = (ri == ci).astype(F32)
    off_masks = []
    for j in range(C.bit_length() - 1):
        off_masks.append(((ri >> (j + 1)) == (ci >> (j + 1)))
                         & (((ri >> j) & 1) == 1) & (((ci >> j) & 1) == 0))
    block_ones = same_head.astype(BF16)
    tri = (incl[:C, :C]).astype(BF16)

    def head_sum(x):
        hi, lo = _split2(x)
        return (jnp.dot(hi, block_ones, preferred_element_type=F32)
                + jnp.dot(lo, block_ones, preferred_element_type=F32))

    def stack(x):
        return _stack2(x, m0, m1).astype(BF16)

    units = []

    def prep_inputs(pp):
        cols = slice(pp * LANES, (pp + 1) * LANES)
        K = k_ref[:, cols].astype(F32)
        A = a_ref[:, cols].astype(F32)
        LW = lw_ref[:, cols]
        kk0 = K * kkw_ref[:, cols]
        kk = kk0 * lax.rsqrt(jnp.maximum(head_sum(kk0 * kk0), 1e-24))
        k2 = K * (1.0 + (A - 1.0) * ka_ref[:, cols])
        kka = kk * A
        bonus_ref[slot_x, :, cols] = (head_sum(r_ref[:, cols] * k2 * rk_ref[:, cols])
                                      * v_ref[:, cols])
        gate_ref[slot_x, :, cols] = g_ref[:, cols].astype(gate_ref.dtype)
        cl_cat = None
        for part in _split3(LW):
            cat = jnp.concatenate([part[c * C:(c + 1) * C, :] for c in range(nchunk)], axis=1)
            term = jnp.dot(tri, cat, preferred_element_type=F32)
            cl_cat = term if cl_cat is None else cl_cat + term
        for c in range(nchunk):
            rows = slice(c * C, (c + 1) * C)
            cl = cl_cat[:, c * LANES:(c + 1) * LANES]
            cl_end = cl[C - 1:C, :]
            gam_inv = jnp.exp(-cl)
            units.append(dict(
                idx=pp * nchunk + c,
                al_s=stack(-kk[rows] * jnp.exp(cl - LW[rows])),
                rb_s=_stack2(r_ref[rows, cols] * jnp.exp(cl), m0, m1),
                be_s=stack(kka[rows] * gam_inv),
                kb_s=stack(k2[rows] * gam_inv),
                v_s=stack(v_ref[rows, cols]),
                gc=jnp.exp(cl_end)))

    def stage_gram():
        for u in units:
            G = _dot_nt(jnp.concatenate([u["al_s"], u["rb_s"].astype(BF16)], axis=0),
                        jnp.concatenate([u["be_s"], u["kb_s"]], axis=0))
            u["Lb"] = jnp.where(strict, G[:C2, :C2], 0.0)
            u["Lk"] = jnp.where(strict, G[:C2, C2:], 0.0).astype(BF16)
            u["R"] = jnp.concatenate([jnp.where(incl, G[C2:, :C2], 0.0),
                                      jnp.where(incl, G[C2:, C2:], 0.0)], axis=1).astype(BF16)

    def stage_init():
        for u in units:
            u["lkv"] = _dot(u["Lk"], u["v_s"]).astype(BF16)
            u["T"] = eye + jnp.where(off_masks[0], u["Lb"], 0.0)

    def lower_rows(x, s):
        return jnp.concatenate([x[r:r + s, :] for r in range(s, C2, 2 * s)], axis=0)

    def stage_left(j):
        s = 1 << j
        for u in units:
            lo = jnp.where(off_masks[j], u["Lb"], 0.0)
            u["TL"] = _dot(lower_rows(u["T"], s) if s >= 8 else u["T"], lo).astype(BF16)

    def stage_right(j):
        s = 1 << j
        for u in units:
            upd = _dot(u["TL"], u["T"])
            if s >= 8:
                pieces = []
                for m in range(C2 // (2 * s)):
                    pieces.append(u["T"][2 * m * s:(2 * m + 1) * s, :])
                    pieces.append(u["T"][(2 * m + 1) * s:(2 * m + 2) * s, :] + upd[m * s:(m + 1) * s, :])
                u["T"] = jnp.concatenate(pieces, axis=0)
            else:
                u["T"] = u["T"] + upd

    def stage_solve():
        for u in units:
            u["Z"] = _dot(u["T"], jnp.concatenate([u["al_s"], u["lkv"]], axis=1))

    def stage_store():
        for u in units:
            X = jnp.concatenate(
                [u["Z"],
                 jnp.concatenate([jnp.zeros((C2, C2), F32), u["v_s"].astype(F32)], axis=1)],
                axis=0)
            QY = _dot(u["R"], X)
            MB = _dot(X.T, jnp.concatenate([u["be_s"], u["kb_s"]], axis=0)) * u["gc"]
            i = base_x + u["idx"]
            q_ref[i] = (u["rb_s"] + QY[:, :C2]).astype(BF16)
            yi_ref[i] = QY[:, C2:]
            m_ref[i] = MB[:C2, :].astype(BF16)
            bt_ref[i] = MB[C2:, :]
            gc_ref[i] = jnp.broadcast_to(u["gc"], (8, LANES))

    prepare = [functools.partial(prep_inputs, pp) for pp in range(npp)]
    prepare += [stage_gram, stage_init]
    for j in range(1, len(off_masks)):
        prepare += [functools.partial(stage_left, j), functools.partial(stage_right, j)]
    prepare += [stage_solve, stage_store]

    S = [jnp.where(first_block, 0.0, s_ref[pp]) for pp in range(npp)]

    def state_step(c):
        for pp in range(npp):
            i = base_y + pp * nchunk + c
            Sb = S[pp].astype(BF16)
            ys = _dot_nt(q_ref[i], Sb) + yi_ref[i]
            y_ref[c * C:(c + 1) * C, pp * LANES:(pp + 1) * LANES] = ys[:C, :] + ys[C:, :]
            S[pp] = (S[pp] * gc_ref[i][0:1, :]
                     + jnp.dot(Sb, m_ref[i], preferred_element_type=F32) + bt_ref[i])

    def finish_out():
        inv_n = 1.0 / RW_HEAD
        for pp in range(npp):
            s_ref[pp] = S[pp]
            cols = slice(pp * LANES, (pp + 1) * LANES)
            y = y_ref[:, cols]
            mean = head_sum(y) * inv_n
            yc = y - mean
            var = head_sum(yc * yc) * inv_n
            yn = yc * lax.rsqrt(var + RW_GN_EPS) * lng_ref[:, cols] + lnb_ref[:, cols]
            o_ref[:, cols] = ((yn + bonus_ref[slot_y, :, cols])
                              * gate_ref[slot_y, :, cols]).astype(o_ref.dtype)

    finish = [functools.partial(state_step, c) for c in range(nchunk)] + [finish_out]

    done = 0
    for si, stage in enumerate(prepare):
        want = ((si + 1) * len(finish)) // len(prepare)
        while done < want:
            finish[done]()
            done += 1
        stage()


def _rw_scan(r, k, v, lw, a, g, kkw, ka, rk, lng, lnb, tb, npp, weights=()):
    b, s, d = r.shape
    nchunk = tb // RW_CHUNK
    width = npp * LANES
    C2 = 2 * RW_CHUNK
    nunit = npp * nchunk
    nblk = s // tb
    ngrp = d // width
    total = b * ngrp * nblk

    def split(kk):
        return kk // (ngrp * nblk), kk % nblk, (kk // nblk) % ngrp

    def tok_in(kq):
        return split(jnp.minimum(kq, total - 1))

    def tok_out(kq):
        return split(jnp.maximum(kq - 1, 0))

    tin = pl.BlockSpec((None, tb, width), tok_in)
    pin = pl.BlockSpec((1, width), lambda kq: (0, tok_in(kq)[2]))
    pout = pl.BlockSpec((1, width), lambda kq: (0, tok_out(kq)[2]))
    flat = [w.reshape(-1, w.shape[-1]) for w in weights]
    for w2 in flat:
        assert w2.shape[0] % total == 0 and (w2.shape[0] // total) % 16 == 0
    cast_specs = [pl.BlockSpec((w2.shape[0] // total, w2.shape[1]),
                               lambda kq: (jnp.minimum(kq, total - 1), 0)) for w2 in flat]
    outs = pl.pallas_call(
        functools.partial(_rw_scan_kernel, nchunk=nchunk, npp=npp, nblk=nblk, ncast=len(flat)),
        grid=(total + 1,),
        in_specs=[tin, tin, tin, tin, tin, tin, pin, pin, pin, pout, pout] + cast_specs,
        out_specs=[pl.BlockSpec((None, tb, width), tok_out)] + cast_specs,
        out_shape=[jax.ShapeDtypeStruct((b, s, d), BF16)]
        + [jax.ShapeDtypeStruct(w2.shape, BF16) for w2 in flat],
        scratch_shapes=[
            pltpu.VMEM((npp, C2, C2), F32),
            pltpu.VMEM((2 * nunit, C2, C2), BF16),
            pltpu.VMEM((2 * nunit, C2, C2), BF16),
            pltpu.VMEM((2 * nunit, C2, C2), F32),
            pltpu.VMEM((2 * nunit, C2, C2), F32),
            pltpu.VMEM((2 * nunit, 8, LANES), F32),
            pltpu.VMEM((2, tb, width), F32),
            pltpu.VMEM((2, tb, width), BF16),
            pltpu.VMEM((tb, width), F32),
        ],
        compiler_params=_params(("arbitrary",)),
    )(r, k, v, lw, a, g, kkw, ka, rk, lng, lnb, *flat)
    return outs[0], [o.reshape(w.shape) for o, w in zip(outs[1:], weights)]


def _oproj_kernel(a_ref, w_ref, x_ref, mod_ref, ng_ref, o_ref):
    y = jnp.dot(a_ref[...], w_ref[...], preferred_element_type=F32)
    o_ref[...] = x_ref[...] + mod_ref[2:3, :] * (_rms(y) * ng_ref[1:2, :])


def _oproj(a, w, x, mod, ng, tm):
    b, s, d = x.shape
    k = a.shape[-1]
    return pl.pallas_call(
        _oproj_kernel,
        grid=(b, s // tm),
        in_specs=[
            pl.BlockSpec((None, tm, k), lambda bi, i: (bi, i, 0)),
            pl.BlockSpec((k, d), lambda bi, i: (0, 0)),
            pl.BlockSpec((None, tm, d), lambda bi, i: (bi, i, 0)),
            pl.BlockSpec((None, 6, d), lambda bi, i: (bi, 0, 0)),
            pl.BlockSpec((4, d), lambda bi, i: (0, 0)),
        ],
        out_specs=pl.BlockSpec((None, tm, d), lambda bi, i: (bi, i, 0)),
        out_shape=jax.ShapeDtypeStruct((b, s, d), F32),
        compiler_params=_params(("arbitrary", "arbitrary")),
    )(a, w, x, mod, ng)


def _mlp_kernel(x_ref, mod_ref, ng_ref, up_ref, dn_ref, o_ref, h_ref, acc_ref, *, nsub):
    j = pl.program_id(2)
    last = pl.num_programs(2) - 1
    sub = x_ref.shape[0] // nsub

    def ffn(h):
        u = jnp.maximum(jnp.dot(h, up_ref[...], preferred_element_type=F32), 0.0)
        return jnp.dot((u * u).astype(BF16), dn_ref[...], preferred_element_type=F32)

    @pl.when(j == 0)
    def _():
        for sb in range(nsub):
            rows = slice(sb * sub, (sb + 1) * sub)
            h = _normmod(x_ref[rows, :], ng_ref[2:3, :], mod_ref[4:5, :], mod_ref[3:4, :])
            h = h.astype(BF16)
            h_ref[rows, :] = h
            acc_ref[rows, :] = ffn(h)

    @pl.when(jnp.logical_and(j > 0, j < last))
    def _():
        acc_ref[...] += ffn(h_ref[...])

    @pl.when(j == last)
    def _():
        for sb in range(nsub):
            rows = slice(sb * sub, (sb + 1) * sub)
            y = acc_ref[rows, :] + ffn(h_ref[rows, :])
            o_ref[rows, :] = x_ref[rows, :] + mod_ref[5:6, :] * (_rms(y) * ng_ref[3:4, :])


def _mlp(x, mod, ng, up_all, dn_all, layer, tm, tf):
    b, s, d = x.shape
    f = up_all.shape[2]
    assert f // tf >= 2
    return pl.pallas_call(
        functools.partial(_mlp_kernel, nsub=2),
        grid=(b, s // tm, f // tf),
        in_specs=[
            pl.BlockSpec((None, tm, d), lambda bi, i, j: (bi, i, 0)),
            pl.BlockSpec((None, 6, d), lambda bi, i, j: (bi, 0, 0)),
            pl.BlockSpec((4, d), lambda bi, i, j: (0, 0)),
            pl.BlockSpec((None, d, tf), lambda bi, i, j: (layer, 0, j)),
            pl.BlockSpec((None, tf, d), lambda bi, i, j: (layer, j, 0)),
        ],
        out_specs=pl.BlockSpec((None, tm, d), lambda bi, i, j: (bi, i, 0)),
        out_shape=jax.ShapeDtypeStruct((b, s, d), F32),
        scratch_shapes=[pltpu.VMEM((tm, d), BF16), pltpu.VMEM((tm, d), F32)],
        compiler_params=_params(("arbitrary", "arbitrary", "arbitrary")),
    )(x, mod, ng, up_all, dn_all)


def _mla_proj_kernel(x_ref, mod_ref, ng_ref, pos_ref, invf_ref, kvg_ref, kdc_ref, kdr_ref, kdrr_ref,
                     kvn_ref, uk_ref, uv_ref, dq_ref, qn_ref, uqn_ref, uqr_ref, uqrr_ref,
                     qnope_ref, qrope_ref, knope_ref, krope_ref, v_ref, *, scale):
    x = x_ref[...]
    xn = _rms(x)
    ang = pos_ref[...] * invf_ref[...]
    cos = jnp.cos(ang)
    sin = jnp.sin(ang)

    hs = (xn * kvg_ref[...]).astype(BF16)
    h = (xn * ng_ref[0:1, :] * (1.0 + mod_ref[1:2, :]) + mod_ref[0:1, :]).astype(BF16)
    ckv_pre = jnp.dot(hs, kdc_ref[...], preferred_element_type=F32)
    cq_pre = jnp.dot(h, dq_ref[...], preferred_element_type=F32)
    kr_a = jnp.dot(hs, kdr_ref[...], preferred_element_type=F32)
    kr_b = jnp.dot(hs, kdrr_ref[...], preferred_element_type=F32)
    ckv = (_rms(ckv_pre) * kvn_ref[...]).astype(BF16)
    cq = (_rms(cq_pre) * qn_ref[...]).astype(BF16)
    knope_ref[...] = jnp.dot(ckv, uk_ref[...], preferred_element_type=F32).astype(knope_ref.dtype)
    qnope = jnp.dot(cq, uqn_ref[...], preferred_element_type=F32) * scale
    v_ref[...] = jnp.dot(ckv, uv_ref[...], preferred_element_type=F32).astype(v_ref.dtype)
    qr = jnp.dot(cq, uqr_ref[...], preferred_element_type=F32)
    qrr = jnp.dot(cq, uqrr_ref[...], preferred_element_type=F32)
    krope_ref[...] = (kr_a * cos + kr_b * sin).astype(krope_ref.dtype)
    qnope_ref[...] = qnope.astype(qnope_ref.dtype)
    reps = qr.shape[1] // LANES
    cos_w = jnp.concatenate([cos] * reps, axis=1)
    sin_w = jnp.concatenate([sin] * reps, axis=1)
    qrope_ref[...] = ((qr * cos_w + qrr * sin_w) * scale).astype(qrope_ref.dtype)


def _mla_proj(x, mod, ng, positions, kvg, kdc, kdr, kdrr, kvn, uk, uv, dq, qn, uqn, uqr, uqrr,
              scale, tm):
    b, s, d = x.shape
    dr = uqr.shape[1]
    half = MLA_ROPE // 2
    inv_freq = 1.0 / (ROPE_THETA ** (jnp.arange(0, MLA_ROPE, 2, dtype=F32) / MLA_ROPE))
    invf = jnp.tile(inv_freq, LANES // half).reshape(1, LANES)
    pos = positions.astype(F32).reshape(b, s, 1)
    full = lambda arr: pl.BlockSpec(arr.shape, lambda bi, i: (0,) * arr.ndim)
    row = lambda w: pl.BlockSpec((None, tm, w), lambda bi, i: (bi, i, 0))
    return pl.pallas_call(
        functools.partial(_mla_proj_kernel, scale=scale),
        grid=(b, s // tm),
        in_specs=[row(d), pl.BlockSpec((None, 6, d), lambda bi, i: (bi, 0, 0)), full(ng),
                  row(1), full(invf), full(kvg), full(kdc), full(kdr), full(kdrr), full(kvn),
                  full(uk), full(uv), full(dq), full(qn), full(uqn), full(uqr), full(uqrr)],
        out_specs=[row(d), row(dr), row(d), row(LANES), row(d)],
        out_shape=[jax.ShapeDtypeStruct((b, s, d), BF16), jax.ShapeDtypeStruct((b, s, dr), BF16),
                   jax.ShapeDtypeStruct((b, s, d), BF16), jax.ShapeDtypeStruct((b, s, LANES), BF16),
                   jax.ShapeDtypeStruct((b, s, d), BF16)],
        compiler_params=_params(("arbitrary", "arbitrary")),
    )(x, mod, ng, pos, invf, kvg, kdc, kdr, kdrr, kvn, uk, uv, dq, qn, uqn, uqr, uqrr)


def _attn_kernel(qn_ref, qr_ref, kn_ref, kr_ref, v_ref, o_ref, *, tq):
    s = qn_ref.shape[0]
    neg = jnp.finfo(F32).min
    lane = lax.broadcasted_iota(jnp.int32, (1, LANES), 1)
    ri = lax.broadcasted_iota(jnp.int32, (tq, tq), 0)
    ci = lax.broadcasted_iota(jnp.int32, (tq, tq), 1)
    causal = ri >= ci
    kr = kr_ref[...]
    k_cat = [jnp.concatenate([kn_ref[:, hd * MLA_NOPE:(hd + 1) * MLA_NOPE], kr], axis=1)
             for hd in range(2)]

    def scores(hd, qi):
        rows = slice(qi * tq, (qi + 1) * tq)
        in_head = (lane >> 6) == hd
        q_rope = jnp.where(in_head, qr_ref[rows, :], jnp.zeros((), BF16))
        q_cat = jnp.concatenate([qn_ref[rows, hd * MLA_NOPE:(hd + 1) * MLA_NOPE], q_rope], axis=1)
        lo = qi * tq
        sd = jnp.where(causal, _dot_nt(q_cat, k_cat[hd][lo:lo + tq, :]), neg)
        sl = _dot_nt(q_cat, k_cat[hd][:lo, :]) if qi > 0 else None
        return sd, sl

    def finish(hd, qi, sd, sl):
        rows = slice(qi * tq, (qi + 1) * tq)
        cols = slice(hd * MLA_HEAD_V, (hd + 1) * MLA_HEAD_V)
        lo = qi * tq
        m = jnp.max(sd, axis=-1, keepdims=True)
        if sl is not None:
            m = jnp.maximum(m, jnp.max(sl, axis=-1, keepdims=True))
        pd = jnp.exp2(sd - m)
        den = jnp.sum(pd, axis=-1, keepdims=True)
        acc = _dot(pd, v_ref[lo:lo + tq, cols])
        if sl is not None:
            pl_ = jnp.exp2(sl - m)
            den = den + jnp.sum(pl_, axis=-1, keepdims=True)
            acc = acc + _dot(pl_, v_ref[:lo, cols])
        o_ref[rows, cols] = (acc / den).astype(o_ref.dtype)

    items = [(hd, qi) for qi in range(s // tq) for hd in range(2)]
    nxt = scores(*items[0])
    for idx, item in enumerate(items):
        cur = nxt
        if idx + 1 < len(items):
            nxt = scores(*items[idx + 1])
        finish(*item, *cur)


def _attention(qn, qr, kn, kr, v, tq):
    b, s, d = qn.shape
    npair = d // (2 * MLA_NOPE)
    wide = pl.BlockSpec((None, s, 2 * MLA_NOPE), lambda bi, hp: (bi, 0, hp))
    return pl.pallas_call(
        functools.partial(_attn_kernel, tq=tq),
        grid=(b, npair),
        in_specs=[wide, pl.BlockSpec((None, s, LANES), lambda bi, hp: (bi, 0, hp)), wide,
                  pl.BlockSpec((None, s, LANES), lambda bi, hp: (bi, 0, 0)), wide],
        out_specs=wide,
        out_shape=jax.ShapeDtypeStruct((b, s, d), BF16),
        compiler_params=_params(("arbitrary", "arbitrary")),
    )(qn, qr, kn, kr, v)


def _pad_cols(w, n):
    return jnp.pad(w, ((0, 0), (0, n - w.shape[1])))


def _pad_rows(w, n):
    return jnp.pad(w, ((0, n - w.shape[0]), (0, 0)))


def _rot_half_cols(w):
    k, n = w.shape
    half = MLA_ROPE // 2
    w3 = w.reshape(k, n // MLA_ROPE, MLA_ROPE)
    return jnp.concatenate([-w3[..., half:], w3[..., :half]], axis=-1).reshape(k, n)


def kernel(x, c, positions, ada_w, ada_b, norm_g, mlp_up, mlp_down, rw_mu, rw_rkv, rw_w0, rw_w1,
           rw_w2, rw_a0, rw_a1, rw_a2, rw_g1, rw_g2, rw_kk, rw_ka, rw_rk, rw_lnx, rw_o, mla_dq,
           mla_qnorm, mla_uq, mla_o, kv_in_g, kv_down, kv_norm, kv_uk, kv_uv):
    b, s, d = x.shape
    depth = ada_w.shape[0]
    n_rw = rw_mu.shape[0]
    kv_lora = kv_norm.shape[0]
    heads = d // MLA_HEAD_V
    assert d % (4 * LANES) == 0 and s % RW_CHUNK == 0

    tm = min(s, 512)
    tm_small = min(s, 256)
    tf = min(mlp_up.shape[2], 1024)
    tb = min(s, 512)
    tq = min(s, 512)
    scan_pairs = 4

    late_weights = (mlp_up, mlp_down, rw_o, mla_o)
    if n_rw == 0:
        up_all, dn_all, rwo_all, mlao_all = [_to_bf16(w) for w in late_weights]
    rkv_all = _to_bf16(rw_rkv)
    mod_all = _ada_mod(c, ada_w, ada_b).reshape(depth, b, 6, d)
    shared = None

    for l in range(depth):
        mod = mod_all[l]
        ng = norm_g[l]
        if l < n_rw:
            i = l
            lora = max(LANES, -(-rw_w1.shape[2] // LANES) * LANES)
            r, k, v, lw, a, g = _rw_in(
                x, mod, ng, rw_mu[i], rkv_all[i], rw_w0[i].reshape(1, d),
                _pad_cols(rw_w1[i], lora).astype(BF16), _pad_rows(rw_w2[i], lora).astype(BF16),
                rw_a0[i].reshape(1, d),
                _pad_cols(rw_a1[i], lora).astype(BF16), _pad_rows(rw_a2[i], lora).astype(BF16),
                rw_g1[i].astype(BF16), rw_g2[i].astype(BF16), tm_small)
            mixed, cast = _rw_scan(r, k, v, lw, a, g, rw_kk[i].reshape(1, d),
                                   rw_ka[i].reshape(1, d), rw_rk[i].reshape(1, d),
                                   rw_lnx[i, 0].reshape(1, d), rw_lnx[i, 1].reshape(1, d),
                                   tb, scan_pairs, late_weights if i == 0 else ())
            if i == 0:
                up_all, dn_all, rwo_all, mlao_all = cast
            x = _oproj(mixed, rwo_all[i], x, mod, ng, tm)
        else:
            i = l - n_rw
            uq = mla_uq[i]
            q_lora = uq.shape[0]
            uqn = uq[:, :, :MLA_NOPE].reshape(q_lora, heads * MLA_NOPE)
            uqr = uq[:, :, MLA_NOPE:].reshape(q_lora, heads * MLA_ROPE)
            kdr = kv_down[:, kv_lora:]
            kdr2 = jnp.concatenate([kdr, kdr], axis=1)
            scale = float((MLA_NOPE + MLA_ROPE) ** -0.5) * math.log2(math.e)
            qn, qr, kn, kr, v = _mla_proj(
                x, mod, ng, positions, kv_in_g.reshape(1, d),
                kv_down[:, :kv_lora].astype(BF16), kdr2.astype(BF16),
                _rot_half_cols(kdr2).astype(BF16), kv_norm.reshape(1, kv_lora),
                kv_uk.reshape(kv_lora, -1).astype(BF16), kv_uv.reshape(kv_lora, -1).astype(BF16),
                mla_dq[i].astype(BF16), mla_qnorm[i].reshape(1, q_lora), uqn.astype(BF16),
                uqr.astype(BF16), _rot_half_cols(uqr).astype(BF16), scale, tm_small)
            if shared is None:
                shared = (kn, kr, v)
            kn, kr, v = shared
            att = _attention(qn, qr, kn, kr, v, tq)
            x = _oproj(att, mlao_all[i], x, mod, ng, tm)
        x = _mlp(x, mod, ng, up_all, dn_all, l, tm, tf)
    return x
```
